```python
import math
import jax, jax.numpy as jnp
from jax import lax
import numpy as np

D_MODEL = 1024
BATCH = 2
SEQ = 16384
DEPTH = 1

M_HEADS = 4
M_QK_DIM = 128
M_V_DIM = 256
M_QK = M_HEADS * M_QK_DIM
M_WIDTH = M_HEADS * M_V_DIM
M_CHUNK = 64
CONV_WIDTH = 4
A_HEADS = 8
A_HEAD_DIM = 64
A_V_DIM = 2 * A_HEAD_DIM
A_QK = A_HEADS * 2 * A_HEAD_DIM
A_WIDTH = A_HEADS * A_V_DIM
Q_BLOCK = 128
ROPE_THETA = 500000.0
ROPE_DIMS = A_HEAD_DIM // 4
EPS = 1e-6
SPLIT_SIZES = (2 * M_QK, M_WIDTH, 4 * M_HEADS, M_WIDTH, M_WIDTH, A_QK, A_QK, A_WIDTH, A_WIDTH, 2 * D_MODEL)
IN_COLS = sum(SPLIT_SIZES)

kernel_name = "hybrid_mlstm_diffattn_gated_block"


def _rmsnorm(x, g):
    xf = x.astype(jnp.float32)
    y = xf * lax.rsqrt(jnp.mean(xf * xf, axis=-1, keepdims=True) + EPS)
    return (y * g.astype(jnp.float32)).astype(x.dtype)


def _rope_partial(x):
    S = x.shape[-2]
    half = ROPE_DIMS // 2
    inv = ROPE_THETA ** (-jnp.arange(0, ROPE_DIMS, 2, dtype=jnp.float32) / ROPE_DIMS)
    ang = jnp.arange(S, dtype=jnp.float32)[:, None] * inv[None, :]
    cos, sin = jnp.cos(ang), jnp.sin(ang)
    xf = x.astype(jnp.float32)
    x1, x2, rest = xf[..., :half], xf[..., half:ROPE_DIMS], xf[..., ROPE_DIMS:]
    out = jnp.concatenate([x1 * cos - x2 * sin, x2 * cos + x1 * sin, rest], axis=-1)
    return out.astype(x.dtype)


def _centred_dwconv(x, w, b):
    C = x.shape[-1]
    left = CONV_WIDTH // 2
    y = lax.conv_general_dilated(x, w[:, None, :].astype(x.dtype), window_strides=(1,),
                                 padding=[(left, CONV_WIDTH - 1 - left)],
                                 dimension_numbers=('NWC', 'WIO', 'NWC'), feature_group_count=C)
    return y + b.astype(x.dtype)


def _mlstm_chunkwise(q, k, v, i_pre, log_f):
    N, H, S, dk = q.shape
    dv = v.shape[-1]
    L = M_CHUNK
    nc = S // L
    f32 = jnp.float32
    q = q.astype(f32) * (dk ** -0.5)

    def chunk(t):
        return jnp.moveaxis(t.astype(f32).reshape(t.shape[:2] + (nc, L) + t.shape[3:]), 2, 0)

    qc, kc, vc, ic = chunk(q), chunk(k), chunk(v), chunk(i_pre)
    ac = jnp.cumsum(chunk(log_f), axis=-1)
    causal = jnp.tril(jnp.ones((L, L), dtype=bool))

    def body(carry, xs):
        C, n, m = carry
        qb, kb, vb, ib, ab = xs
        D = ab[..., :, None] - ab[..., None, :] + ib[..., None, :]
        D = jnp.where(causal, D, -jnp.inf)
        inter = ab + m[..., None]
        mj = jnp.maximum(inter, jnp.max(D, axis=-1))
        w_inter = jnp.exp(inter - mj)
        s = jnp.einsum('nhjd,nhsd->nhjs', qb, kb) * jnp.exp(D - mj[..., None])
        num = w_inter[..., None] * jnp.einsum('nhjd,nhde->nhje', qb, C) + jnp.einsum('nhjs,nhse->nhje', s, vb)
        den = w_inter * jnp.einsum('nhjd,nhd->nhj', qb, n) + jnp.sum(s, axis=-1)
        h = num / jnp.maximum(jnp.abs(den), jnp.exp(-mj))[..., None]
        A = ab[..., -1]
        g = A[..., None] - ab + ib
        m_new = jnp.maximum(A + m, jnp.max(g, axis=-1))
        wk = jnp.exp(g - m_new[..., None])
        decay = jnp.exp(A + m - m_new)
        C_new = decay[..., None, None] * C + jnp.einsum('nhs,nhsd,nhse->nhde', wk, kb, vb)
        n_new = decay[..., None] * n + jnp.einsum('nhs,nhsd->nhd', wk, kb)
        return (C_new, n_new, m_new), h

    init = (jnp.zeros((N, H, dk, dv), f32), jnp.zeros((N, H, dk), f32), jnp.zeros((N, H), f32))
    _, hs = lax.scan(body, init, (qc, kc, vc, ic, ac))
    return jnp.moveaxis(hs, 0, 2).reshape(N, H, S, dv)


def _diff_attention(q, k, v, lam):
    B, H, _, S, dh = q.shape
    nb = S // Q_BLOCK
    scale = dh ** -0.5
    qb = jnp.moveaxis(q.reshape(B, H, 2, nb, Q_BLOCK, dh), 3, 0)

    def block(qblk):
        s = jnp.einsum('bhcqd,bhckd->bhcqk', qblk, k).astype(jnp.float32) * scale
        p = jax.nn.softmax(s, axis=-1)
        a = p[:, :, 0] - lam * p[:, :, 1]
        return jnp.einsum('bhqk,bhke->bhqe', a, v.astype(jnp.float32))

    out = lax.map(block, qb)
    return jnp.moveaxis(out, 0, 2).reshape(B, H, S, 2 * dh).astype(v.dtype)


def _layer(x, layer_idx, pre_g, w_in, conv_w, conv_b, gate_b, m_norm_g, lambda_qk, a_norm_g,
           w_branch_m, w_branch_a, b_merge, w_out, post_g):
    B, S, _ = x.shape
    xn = _rmsnorm(x, pre_g)
    proj = xn @ w_in
    cuts = [int(c) for c in np.cumsum(SPLIT_SIZES)[:-1]]
    m_qk, m_v, m_g, m_o, m_z, a_q, a_k, a_v, a_z, g_merge = jnp.split(proj, cuts, axis=-1)

    m_qk = jax.nn.silu(_centred_dwconv(m_qk, conv_w, conv_b))
    mq, mk = jnp.split(m_qk, 2, axis=-1)
    mq = mq.reshape(B, S, M_HEADS, M_QK_DIM).transpose(0, 2, 1, 3)
    mk = mk.reshape(B, S, M_HEADS, M_QK_DIM).transpose(0, 2, 1, 3)
    mv = m_v.reshape(B, S, M_HEADS, M_V_DIM).transpose(0, 2, 1, 3)
    gates = (m_g.reshape(B, S, 4, M_HEADS) + gate_b).astype(jnp.float32).transpose(2, 0, 3, 1)
    i_fw, f_fw, i_bw, f_bw = gates[0], gates[1], gates[2], gates[3]
    q2 = jnp.concatenate([mq, jnp.flip(mq, 2)], axis=0)
    k2 = jnp.concatenate([mk, jnp.flip(mk, 2)], axis=0)
    v2 = jnp.concatenate([mv, jnp.flip(mv, 2)], axis=0)
    i2 = jnp.concatenate([i_fw, jnp.flip(i_bw, -1)], axis=0)
    lf2 = jax.nn.log_sigmoid(jnp.concatenate([f_fw, jnp.flip(f_bw, -1)], axis=0))
    h2 = _mlstm_chunkwise(q2, k2, v2, i2, lf2)
    h = h2[:B] + jnp.flip(h2[B:], 2)
    h = _rmsnorm(h.transpose(0, 2, 1, 3), m_norm_g.reshape(M_HEADS, M_V_DIM)).astype(x.dtype)
    h_a = jax.nn.sigmoid(m_o) * h.reshape(B, S, M_WIDTH) * jax.nn.silu(m_z)

    aq = _rope_partial(a_q.reshape(B, S, A_HEADS, 2, A_HEAD_DIM).transpose(0, 2, 3, 1, 4))
    ak = _rope_partial(a_k.reshape(B, S, A_HEADS, 2, A_HEAD_DIM).transpose(0, 2, 3, 1, 4))
    av = a_v.reshape(B, S, A_HEADS, A_V_DIM).transpose(0, 2, 1, 3)
    lam_init = 0.8 - 0.6 * math.exp(-0.3 * layer_idx)
    lq = lambda_qk.astype(jnp.float32)
    lam = jnp.exp(jnp.sum(lq[0] * lq[1])) - jnp.exp(jnp.sum(lq[2] * lq[3])) + lam_init
    o = _diff_attention(aq, ak, av, lam)
    o = _rmsnorm(o, a_norm_g).astype(jnp.float32) * (1.0 - lam_init)
    o = o.astype(x.dtype).transpose(0, 2, 1, 3).reshape(B, S, A_WIDTH)
    h_b = o * jax.nn.silu(a_z)

    g = jax.nn.sigmoid(g_merge + b_merge)
    g_a, g_b = jnp.split(g, 2, axis=-1)
    y = g_a * (h_a @ w_branch_m) + g_b * (h_b @ w_branch_a)
    out = y @ w_out
    return x + _rmsnorm(out, post_g)


def setup_inputs(seed: int = 0) -> dict:
    key = jax.random.key(seed)
    ks = jax.random.split(key, 16)
    f32 = jnp.float32
    nrm = lambda k, shp, s: (jax.random.normal(k, shp, f32) * s)
    x = nrm(ks[0], (BATCH, SEQ, D_MODEL), 1.0)
    pre_norm_g = 1.0 + nrm(ks[1], (DEPTH, D_MODEL), 0.02)
    w_in = nrm(ks[2], (DEPTH, D_MODEL, IN_COLS), D_MODEL ** -0.5)
    conv_w = nrm(ks[3], (DEPTH, CONV_WIDTH, 2 * M_QK), CONV_WIDTH ** -0.5)
    conv_b = nrm(ks[4], (DEPTH, 2 * M_QK), 0.02)
    f_base = jnp.linspace(3.0, 6.0, M_HEADS, dtype=f32)
    zero = jnp.zeros((M_HEADS,), f32)
    gate_base = jnp.stack([zero, f_base, zero, f_base])[None]
    gate_b = gate_base + nrm(ks[5], (DEPTH, 4, M_HEADS), 0.1)
    m_norm_g = 1.0 + nrm(ks[6], (DEPTH, M_WIDTH), 0.02)
    lambda_qk = nrm(ks[7], (DEPTH, 4, A_HEAD_DIM), 0.1)
    a_norm_g = 1.0 + nrm(ks[8], (DEPTH, A_V_DIM), 0.02)
    w_branch_m = nrm(ks[9], (DEPTH, M_WIDTH, D_MODEL), M_WIDTH ** -0.5)
    w_branch_a = nrm(ks[10], (DEPTH, A_WIDTH, D_MODEL), A_WIDTH ** -0.5)
    b_merge = nrm(ks[11], (DEPTH, 2 * D_MODEL), 0.02)
    w_out = nrm(ks[12], (DEPTH, D_MODEL, D_MODEL), D_MODEL ** -0.5)
    post_norm_g = 1.0 + nrm(ks[13], (DEPTH, D_MODEL), 0.02)
    return {"x": x, "pre_norm_g": pre_norm_g, "w_in": w_in, "conv_w": conv_w, "conv_b": conv_b,
            "gate_b": gate_b, "m_norm_g": m_norm_g, "lambda_qk": lambda_qk, "a_norm_g": a_norm_g,
            "w_branch_m": w_branch_m, "w_branch_a": w_branch_a, "b_merge": b_merge, "w_out": w_out,
            "post_norm_g": post_norm_g}


def reference(x, pre_norm_g, w_in, conv_w, conv_b, gate_b, m_norm_g, lambda_qk, a_norm_g,
              w_branch_m, w_branch_a, b_merge, w_out, post_norm_g):
    for l in range(DEPTH):
        x = _layer(x, l, pre_norm_g[l], w_in[l], conv_w[l], conv_b[l], gate_b[l], m_norm_g[l],
                   lambda_qk[l], a_norm_g[l], w_branch_m[l], w_branch_a[l], b_merge[l], w_out[l],
                   post_norm_g[l])
    return x
```

```python
import functools
import math

import jax
import jax.numpy as jnp
from jax import lax
from jax.experimental import pallas as pl
from jax.experimental.pallas import tpu as pltpu

D_MODEL = 1024
M_HEADS = 4
M_QK_DIM = 128
M_V_DIM = 256
M_QK = M_HEADS * M_QK_DIM
M_WIDTH = M_HEADS * M_V_DIM
M_CHUNK = 64
CONV_WIDTH = 4
A_HEADS = 8
A_HEAD_DIM = 64
A_V_DIM = 2 * A_HEAD_DIM
A_QK = A_HEADS * 2 * A_HEAD_DIM
A_WIDTH = A_HEADS * A_V_DIM
ROPE_THETA = 500000.0
ROPE_DIMS = A_HEAD_DIM // 4
EPS = 1e-6
SPLIT_SIZES = (2 * M_QK, M_WIDTH, 4 * M_HEADS, M_WIDTH, M_WIDTH, A_QK, A_QK, A_WIDTH, A_WIDTH, 2 * D_MODEL)

LANES = 128
HALO = 16
VMEM_LIMIT = 56 * 1024 * 1024

F32 = jnp.float32
BF16 = jnp.bfloat16
_NT = (((1,), (1,)), ((), ()))
_TN = (((0,), (0,)), ((), ()))


def _rms(xf):
    return xf * lax.rsqrt(jnp.mean(xf * xf, axis=-1, keepdims=True) + EPS)


def _silu(y):
    return y * jax.nn.sigmoid(y)


def _inproj_body(tps, tm, xprev_ref, x_ref, xnext_ref, preg_ref, wmqk_ref, w1_ref, wavt_ref, wgt_ref,
                 convw_ref, convb_ref, rc_ref, rs1_ref, rs2_ref,
                 mq_ref, mk_ref, mv_ref, gt_ref, aq_ref, ak_ref, avt_ref, xext_ref):
    t = lax.rem(pl.program_id(0), tps)
    g = preg_ref[...]
    xn = (_rms(x_ref[...]) * g).astype(BF16)
    xp = jnp.where(t != 0, _rms(xprev_ref[...]) * g, 0.0).astype(BF16)
    xq = jnp.where(t != tps - 1, _rms(xnext_ref[...]) * g, 0.0).astype(BF16)
    xext_ref[0:HALO, :] = xp
    xext_ref[HALO:HALO + tm, :] = xn
    xext_ref[HALO + tm:, :] = xq
    n_ext = tm + 2 * HALO
    ext = jnp.dot(xext_ref[...], wmqk_ref[...], preferred_element_type=F32)
    left = CONV_WIDTH // 2
    y = None
    for j in range(CONV_WIDTH):
        sh = (left - j) % n_ext
        r = ext if sh == 0 else pltpu.roll(ext, sh, axis=0)
        term = convw_ref[j:j + 1, :] * r[HALO:HALO + tm, :]
        y = term if y is None else y + term
    y = _silu(y + convb_ref[...])
    qscale = M_QK_DIM ** -0.5
    for h in range(M_HEADS):
        mq_ref[0, h] = (y[:, h * M_QK_DIM:(h + 1) * M_QK_DIM] * qscale).astype(BF16)
        mk_ref[0, h] = y[:, M_QK + h * M_QK_DIM:M_QK + (h + 1) * M_QK_DIM].astype(BF16)

    r1 = jnp.dot(xn, w1_ref[...], preferred_element_type=F32)
    for h in range(M_HEADS):
        mv_ref[0, h] = r1[:, h * M_V_DIM:(h + 1) * M_V_DIM].astype(BF16)
    rc, rs1, rs2 = rc_ref[...], rs1_ref[...], rs2_ref[...]
    half = ROPE_DIMS // 2

    def rope(a):
        return a * rc + pltpu.roll(a, LANES - half, axis=1) * rs1 + pltpu.roll(a, half, axis=1) * rs2

    ascale = A_HEAD_DIM ** -0.5
    for h in range(A_HEADS):
        a = r1[:, M_WIDTH + h * LANES:M_WIDTH + (h + 1) * LANES]
        aq_ref[0, h] = (rope(a) * ascale).astype(BF16)
        a = r1[:, M_WIDTH + A_QK + h * LANES:M_WIDTH + A_QK + (h + 1) * LANES]
        ak_ref[0, h] = rope(a).astype(BF16)

    avt = lax.dot_general(wavt_ref[...], xn, _NT, preferred_element_type=F32)
    for h in range(A_HEADS):
        avt_ref[0, 0, h] = avt[h * A_V_DIM:(h + 1) * A_V_DIM, :].astype(BF16)
    gt_ref[...] = lax.dot_general(wgt_ref[...], xn, _NT, preferred_element_type=F32)


def _inproj_call(x2d, preg, wmqk, w1, wavt, wgt, convw, convb, rc, rs1, rs2, B, S, tm):
    T = B * S
    tps = S // tm
    nh = tm // HALO
    const = lambda i: (0, 0)
    in_specs = [
        pl.BlockSpec((HALO, D_MODEL), lambda i: (jnp.maximum(i * nh - 1, 0), 0)),
        pl.BlockSpec((tm, D_MODEL), lambda i: (i, 0)),
        pl.BlockSpec((HALO, D_MODEL), lambda i: (jnp.minimum((i + 1) * nh, T // HALO - 1), 0)),
        pl.BlockSpec((1, D_MODEL), const),
        pl.BlockSpec(wmqk.shape, const),
        pl.BlockSpec(w1.shape, const),
        pl.BlockSpec(wavt.shape, const),
        pl.BlockSpec(wgt.shape, const),
        pl.BlockSpec(convw.shape, const),
        pl.BlockSpec(convb.shape, const),
        pl.BlockSpec((tm, LANES), lambda i: (lax.rem(i, tps), 0)),
        pl.BlockSpec((tm, LANES), lambda i: (lax.rem(i, tps), 0)),
        pl.BlockSpec((tm, LANES), lambda i: (lax.rem(i, tps), 0)),
    ]
    bt = lambda i: (i // tps, 0, lax.rem(i, tps), 0)
    out_shape = [
        jax.ShapeDtypeStruct((B, M_HEADS, S, M_QK_DIM), BF16),
        jax.ShapeDtypeStruct((B, M_HEADS, S, M_QK_DIM), BF16),
        jax.ShapeDtypeStruct((B, M_HEADS, S, M_V_DIM), BF16),
        jax.ShapeDtypeStruct((8 * M_HEADS, T), F32),
        jax.ShapeDtypeStruct((B, A_HEADS, S, LANES), BF16),
        jax.ShapeDtypeStruct((B, A_HEADS, S, LANES), BF16),
        jax.ShapeDtypeStruct((B, tps, A_HEADS, A_V_DIM, tm), BF16),
    ]
    out_specs = [
        pl.BlockSpec((1, M_HEADS, tm, M_QK_DIM), bt),
        pl.BlockSpec((1, M_HEADS, tm, M_QK_DIM), bt),
        pl.BlockSpec((1, M_HEADS, tm, M_V_DIM), bt),
        pl.BlockSpec((8 * M_HEADS, tm), lambda i: (0, i)),
        pl.BlockSpec((1, A_HEADS, tm, LANES), bt),
        pl.BlockSpec((1, A_HEADS, tm, LANES), bt),
        pl.BlockSpec((1, 1, A_HEADS, A_V_DIM, tm), lambda i: (i // tps, lax.rem(i, tps), 0, 0, 0)),
    ]
    return pl.pallas_call(
        functools.partial(_inproj_body, tps, tm),
        grid=(T // tm,),
        in_specs=in_specs,
        out_specs=out_specs,
        out_shape=out_shape,
        scratch_shapes=[pltpu.VMEM((tm + 2 * HALO, D_MODEL), BF16)],
        compiler_params=pltpu.CompilerParams(dimension_semantics=("arbitrary",),
                                             vmem_limit_bytes=VMEM_LIMIT),
        name="inproj",
    )(x2d, x2d, x2d, preg, wmqk, w1, wavt, wgt, convw, convb, rc, rs1, rs2)


def _gates_body(tg, gt_ref, gb_ref, rows_ref, cols_ref):
    g = gt_ref[...] + gb_ref[...]
    lf = jnp.minimum(g, 0.0) - jnp.log1p(jnp.exp(-jnp.abs(g)))
    pos = lax.broadcasted_iota(jnp.int32, g.shape, 1) & (M_CHUNK - 1)
    pre, suf = lf, lf
    s = 1
    while s < M_CHUNK:
        pre = pre + jnp.where(pos >= s, pltpu.roll(pre, s, axis=1), 0.0)
        suf = suf + jnp.where(pos < M_CHUNK - s, pltpu.roll(suf, tg - s, axis=1), 0.0)
        s *= 2
    rid = lax.broadcasted_iota(jnp.int32, g.shape, 0) & 7
    r = jnp.where(rid == 0, pre, jnp.where(rid == 2, suf, jnp.where((rid == 1) | (rid == 3), g, 0.0)))
    pad = jnp.zeros((LANES - 8, tg), F32)
    for h in range(M_HEADS):
        r8 = r[8 * h:8 * h + 8, :]
        rows_ref[0, h] = r8
        cols_ref[0, h] = jnp.concatenate([r8, pad], axis=0).T


def _gates_call(gt, gb, B, S, tg):
    nt = S // tg
    return pl.pallas_call(
        functools.partial(_gates_body, tg),
        grid=(B, nt),
        in_specs=[pl.BlockSpec((8 * M_HEADS, tg), lambda b, i: (0, b * nt + i)),
                  pl.BlockSpec((8 * M_HEADS, 1), lambda b, i: (0, 0))],
        out_specs=[pl.BlockSpec((1, M_HEADS, 8, tg), lambda b, i: (b, 0, 0, i)),
                   pl.BlockSpec((1, M_HEADS, tg, LANES), lambda b, i: (b, 0, i, 0))],
        out_shape=[jax.ShapeDtypeStruct((B, M_HEADS, 8, S), F32),
                   jax.ShapeDtypeStruct((B, M_HEADS, S, LANES), F32)],
        compiler_params=pltpu.CompilerParams(dimension_semantics=("arbitrary", "arbitrary"),
                                             vmem_limit_bytes=VMEM_LIMIT),
        name="gates",
    )(gt, gb)


def _mlstm_body(rev, nchunk, q_ref, k_ref, v_ref, rows_ref, cols_ref, h_ref, c_ref, n_ref, m_ref):
    @pl.when(pl.program_id(2) == 0)
    def _():
        c_ref[...] = jnp.zeros_like(c_ref)
        n_ref[...] = jnp.zeros_like(n_ref)
        m_ref[...] = jnp.zeros_like(m_ref)

    L = M_CHUNK
    ar, ir = (2, 3) if rev else (0, 1)
    row_i = lax.broadcasted_iota(jnp.int32, (L, L), 0)
    col_i = lax.broadcasted_iota(jnp.int32, (L, L), 1)
    visible = (col_i >= row_i) if rev else (col_i <= row_i)
    C, n, m = c_ref[...], n_ref[...], m_ref[...]
    order = range(nchunk - 1, -1, -1) if rev else range(nchunk)
    for c in order:
        r0 = c * L
        qc = q_ref[0, 0, r0:r0 + L, :]
        kc = k_ref[0, 0, r0:r0 + L, :]
        vc = v_ref[0, 0, r0:r0 + L, :]
        a_col = cols_ref[0, 0, r0:r0 + L, ar:ar + 1]
        i_col = cols_ref[0, 0, r0:r0 + L, ir:ir + 1]
        a_row = rows_ref[0, 0, ar:ar + 1, r0:r0 + L]
        i_row = rows_ref[0, 0, ir:ir + 1, r0:r0 + L]
        d = jnp.where(visible, a_col - a_row + i_row, -jnp.inf)
        inter = a_col + m
        mj = jnp.maximum(inter, jnp.max(d, axis=1, keepdims=True))
        w_inter = jnp.exp(inter - mj)
        s = lax.dot_general(qc, kc, _NT, preferred_element_type=F32) * jnp.exp(d - mj)
        num = (w_inter * jnp.dot(qc, C.astype(BF16), preferred_element_type=F32)
               + jnp.dot(s.astype(BF16), vc, preferred_element_type=F32))
        den = (w_inter * jnp.sum(qc.astype(F32) * n, axis=1, keepdims=True)
               + jnp.sum(s, axis=1, keepdims=True))
        h_ref[0, r0:r0 + L, :] = num / jnp.maximum(jnp.abs(den), jnp.exp(-mj))
        a_tot = a_col[0:1, :] if rev else a_col[L - 1:L, :]
        g_row = a_tot - a_row + i_row
        g_col = a_tot - a_col + i_col
        m_new = jnp.maximum(a_tot + m, jnp.max(g_row, axis=1, keepdims=True))
        wk = jnp.exp(g_col - m_new)
        decay = jnp.exp(a_tot + m - m_new)
        wv = (wk * vc.astype(F32)).astype(BF16)
        C = decay * C + lax.dot_general(kc, wv, _TN, preferred_element_type=F32)
        n = decay * n + jnp.sum(wk * kc.astype(F32), axis=0, keepdims=True)
        m = m_new
    c_ref[...] = C
    n_ref[...] = n
    m_ref[...] = m


def _mlstm_call(rev, mq, mk, mv, rows, cols, B, S, tr):
    ns = S // tr
    pos = (lambda s: ns - 1 - s) if rev else (lambda s: s)
    return pl.pallas_call(
        functools.partial(_mlstm_body, rev, tr // M_CHUNK),
        grid=(B, M_HEADS, ns),
        in_specs=[pl.BlockSpec((1, 1, tr, M_QK_DIM), lambda b, h, s: (b, h, pos(s), 0)),
                  pl.BlockSpec((1, 1, tr, M_QK_DIM), lambda b, h, s: (b, h, pos(s), 0)),
                  pl.BlockSpec((1, 1, tr, M_V_DIM), lambda b, h, s: (b, h, pos(s), 0)),
                  pl.BlockSpec((1, 1, 8, tr), lambda b, h, s: (b, h, 0, pos(s))),
                  pl.BlockSpec((1, 1, tr, LANES), lambda b, h, s: (b, h, pos(s), 0))],
        out_specs=pl.BlockSpec((1, tr, M_V_DIM), lambda b, h, s: (b, pos(s), h)),
        out_shape=jax.ShapeDtypeStruct((B, S, M_WIDTH), F32),
        scratch_shapes=[pltpu.VMEM((M_QK_DIM, M_V_DIM), F32),
                        pltpu.VMEM((1, M_QK_DIM), F32),
                        pltpu.VMEM((1, 1), F32)],
        compiler_params=pltpu.CompilerParams(dimension_semantics=("arbitrary", "arbitrary", "arbitrary"),
                                             vmem_limit_bytes=VMEM_LIMIT),
        name="mlstm_bwd" if rev else "mlstm_fwd",
    )(mq, mk, mv, rows, cols)


def _attn_body(nkc, tq, tk, lam_init, q_ref, k_ref, vt_ref, lq_ref, g_ref, o_ref,
               qz_ref, m_ref, l_ref, acc_ref):
    q = q_ref[0, 0]
    lane = lax.broadcasted_iota(jnp.int32, q.shape, 1)
    zero = jnp.zeros_like(q)
    qz_ref[0:tq, :] = jnp.where(lane < A_HEAD_DIM, q, zero)
    qz_ref[tq:2 * tq, :] = jnp.where(lane >= A_HEAD_DIM, q, zero)
    m_ref[...] = jnp.full(m_ref.shape, -jnp.inf, F32)
    l_ref[...] = jnp.zeros_like(l_ref)
    acc_ref[...] = jnp.zeros_like(acc_ref)

    def chunk(j, carry):
        kj = k_ref[0, 0, pl.ds(pl.multiple_of(j * tk, tk), tk), :]
        st = lax.dot_general(kj, qz_ref[...], _NT, preferred_element_type=F32)
        m_old = m_ref[...]
        m_new = jnp.maximum(m_old, jnp.max(st, axis=0, keepdims=True))
        alpha = jnp.exp(m_old - m_new)
        p = jnp.exp(st - m_new)
        l_ref[...] = alpha * l_ref[...] + jnp.sum(p, axis=0, keepdims=True)
        acc_ref[...] = alpha * acc_ref[...] + jnp.dot(vt_ref[0, j, 0], p.astype(BF16),
                                                      preferred_element_type=F32)
        m_ref[...] = m_new
        return carry

    lax.fori_loop(0, nkc, chunk, 0)

    lq = lq_ref[...]
    lam = (jnp.exp(jnp.sum(lq[0:1] * lq[1:2], axis=1, keepdims=True))
           - jnp.exp(jnp.sum(lq[2:3] * lq[3:4], axis=1, keepdims=True)) + lam_init)
    accn = acc_ref[...] / l_ref[...]
    ot = accn[:, 0:tq] - lam * accn[:, tq:2 * tq]
    otn = ot * lax.rsqrt(jnp.mean(ot * ot, axis=0, keepdims=True) + EPS)
    o_ref[...] = (otn.T * g_ref[...]) * (1.0 - lam_init)


def _attn_call(aq, ak, avt, lq, ag, B, S, tq, tk, lam_init):
    nkc = S // tk
    nq = S // tq
    return pl.pallas_call(
        functools.partial(_attn_body, nkc, tq, tk, lam_init),
        grid=(B, A_HEADS, nq),
        in_specs=[pl.BlockSpec((1, 1, tq, LANES), lambda b, h, i: (b, h, i, 0)),
                  pl.BlockSpec((1, 1, S, LANES), lambda b, h, i: (b, h, 0, 0)),
                  pl.BlockSpec((1, nkc, 1, A_V_DIM, tk), lambda b, h, i: (b, 0, h, 0, 0)),
                  pl.BlockSpec(lq.shape, lambda b, h, i: (0, 0)),
                  pl.BlockSpec(ag.shape, lambda b, h, i: (0, 0))],
        out_specs=pl.BlockSpec((tq, A_V_DIM), lambda b, h, i: (b * nq + i, h)),
        out_shape=jax.ShapeDtypeStruct((B * S, A_WIDTH), F32),
        scratch_shapes=[pltpu.VMEM((2 * tq, LANES), BF16),
                        pltpu.VMEM((1, 2 * tq), F32),
                        pltpu.VMEM((1, 2 * tq), F32),
                        pltpu.VMEM((A_V_DIM, 2 * tq), F32)],
        compiler_params=pltpu.CompilerParams(dimension_semantics=("arbitrary", "arbitrary", "arbitrary"),
                                             vmem_limit_bytes=VMEM_LIMIT),
        name="attn",
    )(aq, ak, avt, lq, ag)


def _epilogue_body(x_ref, hf_ref, hb_ref, o_ref, preg_ref, w4_ref, mng_ref, bm_ref, wbm_ref, wba_ref,
                   wout_ref, postg_ref, out_ref):
    x = x_ref[...]
    xn = (_rms(x) * preg_ref[...]).astype(BF16)

    def proj(lo, hi):
        return jnp.dot(xn, w4_ref[:, lo:hi], preferred_element_type=F32)

    h = hf_ref[...] + hb_ref[...]
    mng = mng_ref[...]
    hn = jnp.concatenate(
        [_rms(h[:, k * M_V_DIM:(k + 1) * M_V_DIM]) * mng[:, k * M_V_DIM:(k + 1) * M_V_DIM]
         for k in range(M_HEADS)], axis=1)
    h_a = jax.nn.sigmoid(proj(0, M_WIDTH)) * hn * _silu(proj(M_WIDTH, 2 * M_WIDTH))
    ya = jnp.dot(h_a.astype(BF16), wbm_ref[...], preferred_element_type=F32)
    h_b = o_ref[...] * _silu(proj(2 * M_WIDTH, 2 * M_WIDTH + A_WIDTH))
    yb = jnp.dot(h_b.astype(BF16), wba_ref[...], preferred_element_type=F32)
    g0 = 2 * M_WIDTH + A_WIDTH
    bm = bm_ref[...]
    g_a = jax.nn.sigmoid(proj(g0, g0 + D_MODEL) + bm[:, 0:D_MODEL])
    g_b = jax.nn.sigmoid(proj(g0 + D_MODEL, g0 + 2 * D_MODEL) + bm[:, D_MODEL:2 * D_MODEL])
    y = g_a * ya + g_b * yb
    out = jnp.dot(y.astype(BF16), wout_ref[...], preferred_element_type=F32)
    out_ref[...] = x + _rms(out) * postg_ref[...]


def _epilogue_call(x2d, hf, hb, o, preg, w4, mng, bm, wbm, wba, wout, postg, te):
    T = x2d.shape[0]
    row = lambda i: (i, 0)
    const = lambda i: (0, 0)
    single = pl.Buffered(1)
    return pl.pallas_call(
        _epilogue_body,
        grid=(T // te,),
        in_specs=[pl.BlockSpec((te, D_MODEL), row),
                  pl.BlockSpec((te, M_WIDTH), row),
                  pl.BlockSpec((te, M_WIDTH), row),
                  pl.BlockSpec((te, A_WIDTH), row),
                  pl.BlockSpec(preg.shape, const),
                  pl.BlockSpec(w4.shape, const, pipeline_mode=single),
                  pl.BlockSpec(mng.shape, const),
                  pl.BlockSpec(bm.shape, const),
                  pl.BlockSpec(wbm.shape, const, pipeline_mode=single),
                  pl.BlockSpec(wba.shape, const, pipeline_mode=single),
                  pl.BlockSpec(wout.shape, const, pipeline_mode=single),
                  pl.BlockSpec(postg.shape, const)],
        out_specs=pl.BlockSpec((te, D_MODEL), row),
        out_shape=jax.ShapeDtypeStruct((T, D_MODEL), F32),
        compiler_params=pltpu.CompilerParams(dimension_semantics=("arbitrary",),
                                             vmem_limit_bytes=VMEM_LIMIT),
        name="epilogue",
    )(x2d, hf, hb, o, preg, w4, mng, bm, wbm, wba, wout, postg)


def _rope_tables(S):
    half = ROPE_DIMS // 2
    inv = ROPE_THETA ** (-jnp.arange(0, ROPE_DIMS, 2, dtype=F32) / ROPE_DIMS)
    ang = jnp.arange(S, dtype=F32)[:, None] * inv[None, :]
    cos, sin = jnp.cos(ang), jnp.sin(ang)
    one = jnp.ones((S, A_HEAD_DIM - ROPE_DIMS), F32)
    zero = jnp.zeros((S, A_HEAD_DIM - ROPE_DIMS), F32)
    zh = jnp.zeros((S, half), F32)
    rc = jnp.concatenate([cos, cos, one], axis=1)
    rs1 = jnp.concatenate([-sin, zh, zero], axis=1)
    rs2 = jnp.concatenate([zh, sin, zero], axis=1)
    rep = LANES // A_HEAD_DIM
    return jnp.tile(rc, (1, rep)), jnp.tile(rs1, (1, rep)), jnp.tile(rs2, (1, rep))


def _gate_layout(m):
    kinds = jnp.stack([m[..., 1, :], m[..., 0, :], m[..., 3, :], m[..., 2, :]], axis=-1)
    kinds = jnp.concatenate([kinds, jnp.zeros_like(kinds)], axis=-1)
    return kinds.reshape(kinds.shape[:-2] + (8 * M_HEADS,))


def _layer(x, layer_idx, pre_g, w_in, conv_w, conv_b, gate_b, m_norm_g, lambda_qk, a_norm_g,
           w_branch_m, w_branch_a, b_merge, w_out, post_g):
    B, S, _ = x.shape
    T = B * S
    tm = min(512, S)
    tq = min(512, S)
    tr = min(512, S)
    tg = min(2048, S)
    te = min(256, T)
    lam_init = 0.8 - 0.6 * math.exp(-0.3 * layer_idx)

    cuts = [0]
    for sz in SPLIT_SIZES:
        cuts.append(cuts[-1] + sz)
    col = lambda k: w_in[:, cuts[k]:cuts[k + 1]]
    wmqk = col(0).astype(BF16)
    w1 = jnp.concatenate([col(1), col(5), col(6)], axis=1).astype(BF16)
    wavt = col(7).T.astype(BF16)
    wgt = _gate_layout(col(2).reshape(D_MODEL, 4, M_HEADS)).T.astype(BF16)
    gb = _gate_layout(gate_b).reshape(8 * M_HEADS, 1).astype(F32)
    w4 = jnp.concatenate([col(3), col(4), col(8), col(9)], axis=1).astype(BF16)
    rc, rs1, rs2 = _rope_tables(S)

    x2d = x.reshape(T, D_MODEL)
    preg = pre_g.reshape(1, D_MODEL)
    mq, mk, mv, gt, aq, ak, avt = _inproj_call(
        x2d, preg, wmqk, w1, wavt, wgt, conv_w, conv_b.reshape(1, 2 * M_QK), rc, rs1, rs2, B, S, tm)
    rows, cols = _gates_call(gt, gb, B, S, tg)
    hf = _mlstm_call(False, mq, mk, mv, rows, cols, B, S, tr)
    hb = _mlstm_call(True, mq, mk, mv, rows, cols, B, S, tr)
    o = _attn_call(aq, ak, avt, lambda_qk.astype(F32), a_norm_g.reshape(1, A_V_DIM), B, S, tq, tm, lam_init)
    out = _epilogue_call(
        x2d, hf.reshape(T, M_WIDTH), hb.reshape(T, M_WIDTH), o, preg, w4,
        m_norm_g.reshape(1, M_WIDTH), b_merge.reshape(1, 2 * D_MODEL),
        w_branch_m.astype(BF16), w_branch_a.astype(BF16), w_out.astype(BF16),
        post_g.reshape(1, D_MODEL), te)
    return out.reshape(B, S, D_MODEL)


def kernel(x, pre_norm_g, w_in, conv_w, conv_b, gate_b, m_norm_g, lambda_qk, a_norm_g, w_branch_m,
           w_branch_a, b_merge, w_out, post_norm_g):
    for l in range(pre_norm_g.shape[0]):
        x = _layer(x, l, pre_norm_g[l], w_in[l], conv_w[l], conv_b[l], gate_b[l], m_norm_g[l],
                   lambda_qk[l], a_norm_g[l], w_branch_m[l], w_branch_a[l], b_merge[l], w_out[l],
                   post_norm_g[l])
    return x
```

```python
import functools
import math

import jax
import jax.numpy as jnp
from jax import lax
from jax.experimental import pallas as pl
from jax.experimental.pallas import tpu as pltpu

D_MODEL = 1024
M_HEADS = 4
M_QK_DIM = 128
M_V_DIM = 256
M_QK = M_HEADS * M_QK_DIM
M_WIDTH = M_HEADS * M_V_DIM
M_CHUNK = 64
CONV_WIDTH = 4
A_HEADS = 8
A_HEAD_DIM = 64
A_V_DIM = 2 * A_HEAD_DIM
A_QK = A_HEADS * 2 * A_HEAD_DIM
A_WIDTH = A_HEADS * A_V_DIM
ROPE_THETA = 500000.0
ROPE_DIMS = A_HEAD_DIM // 4
EPS = 1e-6
SPLIT_SIZES = (2 * M_QK, M_WIDTH, 4 * M_HEADS, M_WIDTH, M_WIDTH, A_QK, A_QK, A_WIDTH, A_WIDTH, 2 * D_MODEL)

LANES = 128
HALO = 16
VMEM_LIMIT = 56 * 1024 * 1024

F32 = jnp.float32
BF16 = jnp.bfloat16
_NT = (((1,), (1,)), ((), ()))
_TN = (((0,), (0,)), ((), ()))


def _rms(xf):
    return xf * lax.rsqrt(jnp.mean(xf * xf, axis=-1, keepdims=True) + EPS)


def _silu(y):
    return y * jax.nn.sigmoid(y)


def _inproj_body(tps, tm, xprev_ref, x_ref, xnext_ref, preg_ref, wmqk_ref, w1_ref, wavt_ref, wgt_ref,
                 convw_ref, convb_ref, rc_ref, rs1_ref, rs2_ref,
                 mq_ref, mk_ref, mv_ref, gt_ref, aq_ref, ak_ref, avt_ref, xext_ref):
    t = lax.rem(pl.program_id(0), tps)
    g = preg_ref[...]
    xn = (_rms(x_ref[...]) * g).astype(BF16)
    xp = jnp.where(t != 0, _rms(xprev_ref[...]) * g, 0.0).astype(BF16)
    xq = jnp.where(t != tps - 1, _rms(xnext_ref[...]) * g, 0.0).astype(BF16)
    xext_ref[0:HALO, :] = xp
    xext_ref[HALO:HALO + tm, :] = xn
    xext_ref[HALO + tm:, :] = xq
    n_ext = tm + 2 * HALO
    ext = jnp.dot(xext_ref[...], wmqk_ref[...], preferred_element_type=F32)
    left = CONV_WIDTH // 2
    y = None
    for j in range(CONV_WIDTH):
        sh = (left - j) % n_ext
        r = ext if sh == 0 else pltpu.roll(ext, sh, axis=0)
        term = convw_ref[j:j + 1, :] * r[HALO:HALO + tm, :]
        y = term if y is None else y + term
    y = _silu(y + convb_ref[...])
    qscale = M_QK_DIM ** -0.5
    for h in range(M_HEADS):
        mq_ref[0, h] = (y[:, h * M_QK_DIM:(h + 1) * M_QK_DIM] * qscale).astype(BF16)
        mk_ref[0, h] = y[:, M_QK + h * M_QK_DIM:M_QK + (h + 1) * M_QK_DIM].astype(BF16)

    r1 = jnp.dot(xn, w1_ref[...], preferred_element_type=F32)
    for h in range(M_HEADS):
        mv_ref[0, h] = r1[:, h * M_V_DIM:(h + 1) * M_V_DIM].astype(BF16)
    rc, rs1, rs2 = rc_ref[...], rs1_ref[...], rs2_ref[...]
    half = ROPE_DIMS // 2

    def rope(a):
        return a * rc + pltpu.roll(a, LANES - half, axis=1) * rs1 + pltpu.roll(a, half, axis=1) * rs2

    ascale = A_HEAD_DIM ** -0.5 * math.log2(math.e)
    for h in range(A_HEADS):
        a = r1[:, M_WIDTH + h * LANES:M_WIDTH + (h + 1) * LANES]
        aq_ref[0, h] = (rope(a) * ascale).astype(BF16)
        a = r1[:, M_WIDTH + A_QK + h * LANES:M_WIDTH + A_QK + (h + 1) * LANES]
        ak_ref[0, h] = rope(a).astype(BF16)

    avt = lax.dot_general(wavt_ref[...], xn, _NT, preferred_element_type=F32)
    for h in range(A_HEADS):
        avt_ref[0, 0, h] = avt[h * A_V_DIM:(h + 1) * A_V_DIM, :].astype(BF16)
    gt_ref[...] = lax.dot_general(wgt_ref[...], xn, _NT, preferred_element_type=F32)


def _inproj_call(x2d, preg, wmqk, w1, wavt, wgt, convw, convb, rc, rs1, rs2, B, S, tm):
    T = B * S
    tps = S // tm
    nh = tm // HALO
    const = lambda i: (0, 0)
    in_specs = [
        pl.BlockSpec((HALO, D_MODEL), lambda i: (jnp.maximum(i * nh - 1, 0), 0)),
        pl.BlockSpec((tm, D_MODEL), lambda i: (i, 0)),
        pl.BlockSpec((HALO, D_MODEL), lambda i: (jnp.minimum((i + 1) * nh, T // HALO - 1), 0)),
        pl.BlockSpec((1, D_MODEL), const),
        pl.BlockSpec(wmqk.shape, const),
        pl.BlockSpec(w1.shape, const),
        pl.BlockSpec(wavt.shape, const),
        pl.BlockSpec(wgt.shape, const),
        pl.BlockSpec(convw.shape, const),
        pl.BlockSpec(convb.shape, const),
        pl.BlockSpec((tm, LANES), lambda i: (lax.rem(i, tps), 0)),
        pl.BlockSpec((tm, LANES), lambda i: (lax.rem(i, tps), 0)),
        pl.BlockSpec((tm, LANES), lambda i: (lax.rem(i, tps), 0)),
    ]
    bt = lambda i: (i // tps, 0, lax.rem(i, tps), 0)
    out_shape = [
        jax.ShapeDtypeStruct((B, M_HEADS, S, M_QK_DIM), BF16),
        jax.ShapeDtypeStruct((B, M_HEADS, S, M_QK_DIM), BF16),
        jax.ShapeDtypeStruct((B, M_HEADS, S, M_V_DIM), BF16),
        jax.ShapeDtypeStruct((8 * M_HEADS, T), F32),
        jax.ShapeDtypeStruct((B, A_HEADS, S, LANES), BF16),
        jax.ShapeDtypeStruct((B, A_HEADS, S, LANES), BF16),
        jax.ShapeDtypeStruct((B, tps, A_HEADS, A_V_DIM, tm), BF16),
    ]
    out_specs = [
        pl.BlockSpec((1, M_HEADS, tm, M_QK_DIM), bt),
        pl.BlockSpec((1, M_HEADS, tm, M_QK_DIM), bt),
        pl.BlockSpec((1, M_HEADS, tm, M_V_DIM), bt),
        pl.BlockSpec((8 * M_HEADS, tm), lambda i: (0, i)),
        pl.BlockSpec((1, A_HEADS, tm, LANES), bt),
        pl.BlockSpec((1, A_HEADS, tm, LANES), bt),
        pl.BlockSpec((1, 1, A_HEADS, A_V_DIM, tm), lambda i: (i // tps, lax.rem(i, tps), 0, 0, 0)),
    ]
    return pl.pallas_call(
        functools.partial(_inproj_body, tps, tm),
        grid=(T // tm,),
        in_specs=in_specs,
        out_specs=out_specs,
        out_shape=out_shape,
        scratch_shapes=[pltpu.VMEM((tm + 2 * HALO, D_MODEL), BF16)],
        compiler_params=pltpu.CompilerParams(dimension_semantics=("arbitrary",),
                                             vmem_limit_bytes=VMEM_LIMIT),
        name="inproj",
    )(x2d, x2d, x2d, preg, wmqk, w1, wavt, wgt, convw, convb, rc, rs1, rs2)


def _gates_body(tg, gt_ref, gb_ref, rows_ref, cols_ref):
    g = gt_ref[...] + gb_ref[...]
    lf = jnp.minimum(g, 0.0) - jnp.log1p(jnp.exp(-jnp.abs(g)))
    pos = lax.broadcasted_iota(jnp.int32, g.shape, 1) & (M_CHUNK - 1)
    pre, suf = lf, lf
    s = 1
    while s < M_CHUNK:
        pre = pre + jnp.where(pos >= s, pltpu.roll(pre, s, axis=1), 0.0)
        suf = suf + jnp.where(pos < M_CHUNK - s, pltpu.roll(suf, tg - s, axis=1), 0.0)
        s *= 2
    rid = lax.broadcasted_iota(jnp.int32, g.shape, 0) & 7
    r = jnp.where(rid == 0, pre, jnp.where(rid == 2, suf, jnp.where((rid == 1) | (rid == 3), g, 0.0)))
    pad = jnp.zeros((LANES - 8, tg), F32)
    for h in range(M_HEADS):
        r8 = r[8 * h:8 * h + 8, :]
        rows_ref[0, h] = r8
        cols_ref[0, h] = jnp.concatenate([r8, pad], axis=0).T


def _gates_call(gt, gb, B, S, tg):
    nt = S // tg
    return pl.pallas_call(
        functools.partial(_gates_body, tg),
        grid=(B, nt),
        in_specs=[pl.BlockSpec((8 * M_HEADS, tg), lambda b, i: (0, b * nt + i)),
                  pl.BlockSpec((8 * M_HEADS, 1), lambda b, i: (0, 0))],
        out_specs=[pl.BlockSpec((1, M_HEADS, 8, tg), lambda b, i: (b, 0, 0, i)),
                   pl.BlockSpec((1, M_HEADS, tg, LANES), lambda b, i: (b, 0, i, 0))],
        out_shape=[jax.ShapeDtypeStruct((B, M_HEADS, 8, S), F32),
                   jax.ShapeDtypeStruct((B, M_HEADS, S, LANES), F32)],
        compiler_params=pltpu.CompilerParams(dimension_semantics=("arbitrary", "arbitrary"),
                                             vmem_limit_bytes=VMEM_LIMIT),
        name="gates",
    )(gt, gb)


def _mlstm_body(rev, nchunk, q_ref, k_ref, v_ref, rows_ref, cols_ref, h_ref, c_ref, n_ref, m_ref):
    @pl.when(pl.program_id(2) == 0)
    def _():
        c_ref[...] = jnp.zeros_like(c_ref)
        n_ref[...] = jnp.zeros_like(n_ref)
        m_ref[...] = jnp.zeros_like(m_ref)

    L = M_CHUNK
    ar, ir = (2, 3) if rev else (0, 1)
    row_i = lax.broadcasted_iota(jnp.int32, (L, L), 0)
    col_i = lax.broadcasted_iota(jnp.int32, (L, L), 1)
    visible = (col_i >= row_i) if rev else (col_i <= row_i)
    C, n, m = c_ref[...], n_ref[...], m_ref[...]
    order = range(nchunk - 1, -1, -1) if rev else range(nchunk)
    for c in order:
        r0 = c * L
        qc = q_ref[0, 0, r0:r0 + L, :]
        kc = k_ref[0, 0, r0:r0 + L, :]
        vc = v_ref[0, 0, r0:r0 + L, :]
        a_col = cols_ref[0, 0, r0:r0 + L, ar:ar + 1]
        i_col = cols_ref[0, 0, r0:r0 + L, ir:ir + 1]
        a_row = rows_ref[0, 0, ar:ar + 1, r0:r0 + L]
        i_row = rows_ref[0, 0, ir:ir + 1, r0:r0 + L]
        d = jnp.where(visible, a_col - a_row + i_row, -jnp.inf)
        inter = a_col + m
        mj = jnp.maximum(inter, jnp.max(d, axis=1, keepdims=True))
        w_inter = jnp.exp(inter - mj)
        s = lax.dot_general(qc, kc, _NT, preferred_element_type=F32) * jnp.exp(d - mj)
        num = (w_inter * jnp.dot(qc, C.astype(BF16), preferred_element_type=F32)
               + jnp.dot(s.astype(BF16), vc, preferred_element_type=F32))
        den = (w_inter * jnp.sum(qc.astype(F32) * n, axis=1, keepdims=True)
               + jnp.sum(s, axis=1, keepdims=True))
        h_ref[0, r0:r0 + L, :] = num / jnp.maximum(jnp.abs(den), jnp.exp(-mj))
        a_tot = a_col[0:1, :] if rev else a_col[L - 1:L, :]
        g_row = a_tot - a_row + i_row
        g_col = a_tot - a_col + i_col
        m_new = jnp.maximum(a_tot + m, jnp.max(g_row, axis=1, keepdims=True))
        wk = jnp.exp(g_col - m_new)
        decay = jnp.exp(a_tot + m - m_new)
        wv = (wk * vc.astype(F32)).astype(BF16)
        C = decay * C + lax.dot_general(kc, wv, _TN, preferred_element_type=F32)
        n = decay * n + jnp.sum(wk * kc.astype(F32), axis=0, keepdims=True)
        m = m_new
    c_ref[...] = C
    n_ref[...] = n
    m_ref[...] = m


def _mlstm_call(rev, mq, mk, mv, rows, cols, B, S, tr):
    ns = S // tr
    pos = (lambda s: ns - 1 - s) if rev else (lambda s: s)
    return pl.pallas_call(
        functools.partial(_mlstm_body, rev, tr // M_CHUNK),
        grid=(B, M_HEADS, ns),
        in_specs=[pl.BlockSpec((1, 1, tr, M_QK_DIM), lambda b, h, s: (b, h, pos(s), 0)),
                  pl.BlockSpec((1, 1, tr, M_QK_DIM), lambda b, h, s: (b, h, pos(s), 0)),
                  pl.BlockSpec((1, 1, tr, M_V_DIM), lambda b, h, s: (b, h, pos(s), 0)),
                  pl.BlockSpec((1, 1, 8, tr), lambda b, h, s: (b, h, 0, pos(s))),
                  pl.BlockSpec((1, 1, tr, LANES), lambda b, h, s: (b, h, pos(s), 0))],
        out_specs=pl.BlockSpec((1, tr, M_V_DIM), lambda b, h, s: (b, pos(s), h)),
        out_shape=jax.ShapeDtypeStruct((B, S, M_WIDTH), F32),
        scratch_shapes=[pltpu.VMEM((M_QK_DIM, M_V_DIM), F32),
                        pltpu.VMEM((1, M_QK_DIM), F32),
                        pltpu.VMEM((1, 1), F32)],
        compiler_params=pltpu.CompilerParams(dimension_semantics=("arbitrary", "arbitrary", "arbitrary"),
                                             vmem_limit_bytes=VMEM_LIMIT),
        name="mlstm_bwd" if rev else "mlstm_fwd",
    )(mq, mk, mv, rows, cols)


def _attn_body(nkc, tq, tk, lam_init, q_ref, k_ref, vt_ref, lq_ref, g_ref, o_ref,
               qz_ref, st0_ref, st1_ref, m_ref, l_ref, acc_ref):
    q = q_ref[0, 0]
    lane = lax.broadcasted_iota(jnp.int32, q.shape, 1)
    zero = jnp.zeros_like(q)
    qz_ref[0:tq, :] = jnp.where(lane < A_HEAD_DIM, q, zero)
    qz_ref[tq:2 * tq, :] = jnp.where(lane >= A_HEAD_DIM, q, zero)
    m_ref[...] = jnp.full(m_ref.shape, -jnp.inf, F32)
    l_ref[...] = jnp.zeros_like(l_ref)
    acc_ref[...] = jnp.zeros_like(acc_ref)

    def scores(j, st_ref):
        kj = k_ref[0, 0, pl.ds(pl.multiple_of(j * tk, tk), tk), :]
        st = lax.dot_general(kj, qz_ref[...], _NT, preferred_element_type=F32)
        st_ref[...] = st
        return jnp.max(st, axis=0, keepdims=True)

    def accumulate(j, st_ref, cmax):
        m_old = m_ref[...]
        m_new = jnp.maximum(m_old, cmax)
        alpha = jnp.exp2(m_old - m_new)
        p = jnp.exp2(st_ref[...] - m_new)
        l_ref[...] = alpha * l_ref[...] + jnp.sum(p, axis=0, keepdims=True)
        acc_ref[...] = alpha * acc_ref[...] + jnp.dot(vt_ref[0, j, 0], p.astype(BF16),
                                                      preferred_element_type=F32)
        m_ref[...] = m_new

    def pair(i, cmax0):
        j = 2 * i
        cmax1 = scores(j + 1, st1_ref)
        accumulate(j, st0_ref, cmax0)
        cmax0 = scores(j + 2, st0_ref)
        accumulate(j + 1, st1_ref, cmax1)
        return cmax0

    cmax0 = scores(0, st0_ref)
    cmax0 = lax.fori_loop(0, nkc // 2 - 1, pair, cmax0)
    cmax1 = scores(nkc - 1, st1_ref)
    accumulate(nkc - 2, st0_ref, cmax0)
    accumulate(nkc - 1, st1_ref, cmax1)

    lq = lq_ref[...]
    lam = (jnp.exp(jnp.sum(lq[0:1] * lq[1:2], axis=1, keepdims=True))
           - jnp.exp(jnp.sum(lq[2:3] * lq[3:4], axis=1, keepdims=True)) + lam_init)
    accn = acc_ref[...] / l_ref[...]
    ot = accn[:, 0:tq] - lam * accn[:, tq:2 * tq]
    otn = ot * lax.rsqrt(jnp.mean(ot * ot, axis=0, keepdims=True) + EPS)
    o_ref[...] = (otn.T * g_ref[...]) * (1.0 - lam_init)


def _attn_call(aq, ak, avt, lq, ag, B, S, tq, tk, lam_init):
    nkc = S // tk
    nq = S // tq
    assert nkc >= 2 and nkc % 2 == 0, "key chunks are processed in pairs"
    return pl.pallas_call(
        functools.partial(_attn_body, nkc, tq, tk, lam_init),
        grid=(B, A_HEADS, nq),
        in_specs=[pl.BlockSpec((1, 1, tq, LANES), lambda b, h, i: (b, h, i, 0)),
                  pl.BlockSpec((1, 1, S, LANES), lambda b, h, i: (b, h, 0, 0)),
                  pl.BlockSpec((1, nkc, 1, A_V_DIM, tk), lambda b, h, i: (b, 0, h, 0, 0)),
                  pl.BlockSpec(lq.shape, lambda b, h, i: (0, 0)),
                  pl.BlockSpec(ag.shape, lambda b, h, i: (0, 0))],
        out_specs=pl.BlockSpec((tq, A_V_DIM), lambda b, h, i: (b * nq + i, h)),
        out_shape=jax.ShapeDtypeStruct((B * S, A_WIDTH), F32),
        scratch_shapes=[pltpu.VMEM((2 * tq, LANES), BF16),
                        pltpu.VMEM((tk, 2 * tq), F32),
                        pltpu.VMEM((tk, 2 * tq), F32),
                        pltpu.VMEM((1, 2 * tq), F32),
                        pltpu.VMEM((1, 2 * tq), F32),
                        pltpu.VMEM((A_V_DIM, 2 * tq), F32)],
        compiler_params=pltpu.CompilerParams(dimension_semantics=("arbitrary", "arbitrary", "arbitrary"),
                                             vmem_limit_bytes=VMEM_LIMIT),
        name="attn",
    )(aq, ak, avt, lq, ag)


def _epilogue_body(x_ref, hf_ref, hb_ref, o_ref, preg_ref, w4_ref, mng_ref, bm_ref, wbm_ref, wba_ref,
                   wout_ref, postg_ref, out_ref):
    x = x_ref[...]
    xn = (_rms(x) * preg_ref[...]).astype(BF16)

    def proj(lo, hi):
        return jnp.dot(xn, w4_ref[:, lo:hi], preferred_element_type=F32)

    h = hf_ref[...] + hb_ref[...]
    mng = mng_ref[...]
    hn = jnp.concatenate(
        [_rms(h[:, k * M_V_DIM:(k + 1) * M_V_DIM]) * mng[:, k * M_V_DIM:(k + 1) * M_V_DIM]
         for k in range(M_HEADS)], axis=1)
    h_a = jax.nn.sigmoid(proj(0, M_WIDTH)) * hn * _silu(proj(M_WIDTH, 2 * M_WIDTH))
    ya = jnp.dot(h_a.astype(BF16), wbm_ref[...], preferred_element_type=F32)
    h_b = o_ref[...] * _silu(proj(2 * M_WIDTH, 2 * M_WIDTH + A_WIDTH))
    yb = jnp.dot(h_b.astype(BF16), wba_ref[...], preferred_element_type=F32)
    g0 = 2 * M_WIDTH + A_WIDTH
    bm = bm_ref[...]
    g_a = jax.nn.sigmoid(proj(g0, g0 + D_MODEL) + bm[:, 0:D_MODEL])
    g_b = jax.nn.sigmoid(proj(g0 + D_MODEL, g0 + 2 * D_MODEL) + bm[:, D_MODEL:2 * D_MODEL])
    y = g_a * ya + g_b * yb
    out = jnp.dot(y.astype(BF16), wout_ref[...], preferred_element_type=F32)
    out_ref[...] = x + _rms(out) * postg_ref[...]


def _epilogue_call(x2d, hf, hb, o, preg, w4, mng, bm, wbm, wba, wout, postg, te):
    T = x2d.shape[0]
    row = lambda i: (i, 0)
    const = lambda i: (0, 0)
    single = pl.Buffered(1)
    return pl.pallas_call(
        _epilogue_body,
        grid=(T // te,),
        in_specs=[pl.BlockSpec((te, D_MODEL), row),
                  pl.BlockSpec((te, M_WIDTH), row),
                  pl.BlockSpec((te, M_WIDTH), row),
                  pl.BlockSpec((te, A_WIDTH), row),
                  pl.BlockSpec(preg.shape, const),
                  pl.BlockSpec(w4.shape, const, pipeline_mode=single),
                  pl.BlockSpec(mng.shape, const),
                  pl.BlockSpec(bm.shape, const),
                  pl.BlockSpec(wbm.shape, const, pipeline_mode=single),
                  pl.BlockSpec(wba.shape, const, pipeline_mode=single),
                  pl.BlockSpec(wout.shape, const, pipeline_mode=single),
                  pl.BlockSpec(postg.shape, const)],
        out_specs=pl.BlockSpec((te, D_MODEL), row),
        out_shape=jax.ShapeDtypeStruct((T, D_MODEL), F32),
        compiler_params=pltpu.CompilerParams(dimension_semantics=("arbitrary",),
                                             vmem_limit_bytes=VMEM_LIMIT),
        name="epilogue",
    )(x2d, hf, hb, o, preg, w4, mng, bm, wbm, wba, wout, postg)


def _rope_tables(S):
    half = ROPE_DIMS // 2
    inv = ROPE_THETA ** (-jnp.arange(0, ROPE_DIMS, 2, dtype=F32) / ROPE_DIMS)
    ang = jnp.arange(S, dtype=F32)[:, None] * inv[None, :]
    cos, sin = jnp.cos(ang), jnp.sin(ang)
    one = jnp.ones((S, A_HEAD_DIM - ROPE_DIMS), F32)
    zero = jnp.zeros((S, A_HEAD_DIM - ROPE_DIMS), F32)
    zh = jnp.zeros((S, half), F32)
    rc = jnp.concatenate([cos, cos, one], axis=1)
    rs1 = jnp.concatenate([-sin, zh, zero], axis=1)
    rs2 = jnp.concatenate([zh, sin, zero], axis=1)
    rep = LANES // A_HEAD_DIM
    return jnp.tile(rc, (1, rep)), jnp.tile(rs1, (1, rep)), jnp.tile(rs2, (1, rep))


def _gate_layout(m):
    kinds = jnp.stack([m[..., 1, :], m[..., 0, :], m[..., 3, :], m[..., 2, :]], axis=-1)
    kinds = jnp.concatenate([kinds, jnp.zeros_like(kinds)], axis=-1)
    return kinds.reshape(kinds.shape[:-2] + (8 * M_HEADS,))


def _layer(x, layer_idx, pre_g, w_in, conv_w, conv_b, gate_b, m_norm_g, lambda_qk, a_norm_g,
           w_branch_m, w_branch_a, b_merge, w_out, post_g):
    B, S, _ = x.shape
    T = B * S
    tm = min(512, S)
    tq = min(512, S)
    tr = min(512, S)
    tg = min(2048, S)
    te = min(256, T)
    lam_init = 0.8 - 0.6 * math.exp(-0.3 * layer_idx)

    cuts = [0]
    for sz in SPLIT_SIZES:
        cuts.append(cuts[-1] + sz)
    col = lambda k: w_in[:, cuts[k]:cuts[k + 1]]
    wmqk = col(0).astype(BF16)
    w1 = jnp.concatenate([col(1), col(5), col(6)], axis=1).astype(BF16)
    wavt = col(7).T.astype(BF16)
    wgt = _gate_layout(col(2).reshape(D_MODEL, 4, M_HEADS)).T.astype(BF16)
    gb = _gate_layout(gate_b).reshape(8 * M_HEADS, 1).astype(F32)
    w4 = jnp.concatenate([col(3), col(4), col(8), col(9)], axis=1).astype(BF16)
    rc, rs1, rs2 = _rope_tables(S)

    x2d = x.reshape(T, D_MODEL)
    preg = pre_g.reshape(1, D_MODEL)
    mq, mk, mv, gt, aq, ak, avt = _inproj_call(
        x2d, preg, wmqk, w1, wavt, wgt, conv_w, conv_b.reshape(1, 2 * M_QK), rc, rs1, rs2, B, S, tm)
    rows, cols = _gates_call(gt, gb, B, S, tg)
    hf = _mlstm_call(False, mq, mk, mv, rows, cols, B, S, tr)
    hb = _mlstm_call(True, mq, mk, mv, rows, cols, B, S, tr)
    o = _attn_call(aq, ak, avt, lambda_qk.astype(F32), a_norm_g.reshape(1, A_V_DIM), B, S, tq, tm, lam_init)
    out = _epilogue_call(
        x2d, hf.reshape(T, M_WIDTH), hb.reshape(T, M_WIDTH), o, preg, w4,
        m_norm_g.reshape(1, M_WIDTH), b_merge.reshape(1, 2 * D_MODEL),
        w_branch_m.astype(BF16), w_branch_a.astype(BF16), w_out.astype(BF16),
        post_g.reshape(1, D_MODEL), te)
    return out.reshape(B, S, D_MODEL)


def kernel(x, pre_norm_g, w_in, conv_w, conv_b, gate_b, m_norm_g, lambda_qk, a_norm_g, w_branch_m,
           w_branch_a, b_merge, w_out, post_norm_g):
    for l in range(pre_norm_g.shape[0]):
        x = _layer(x, l, pre_norm_g[l], w_in[l], conv_w[l], conv_b[l], gate_b[l], m_norm_g[l],
                   lambda_qk[l], a_norm_g[l], w_branch_m[l], w_branch_a[l], b_merge[l], w_out[l],
                   post_norm_g[l])
    return x
```

```python
import functools
import math

import jax
import jax.numpy as jnp
from jax import lax
from jax.experimental import pallas as pl
from jax.experimental.pallas import tpu as pltpu

D_MODEL = 1024
M_HEADS = 4
M_QK_DIM = 128
M_V_DIM = 256
M_QK = M_HEADS * M_QK_DIM
M_WIDTH = M_HEADS * M_V_DIM
M_CHUNK = 64
CONV_WIDTH = 4
A_HEADS = 8
A_HEAD_DIM = 64
A_V_DIM = 2 * A_HEAD_DIM
A_QK = A_HEADS * 2 * A_HEAD_DIM
A_WIDTH = A_HEADS * A_V_DIM
ROPE_THETA = 500000.0
ROPE_DIMS = A_HEAD_DIM // 4
EPS = 1e-6
SPLIT_SIZES = (2 * M_QK, M_WIDTH, 4 * M_HEADS, M_WIDTH, M_WIDTH, A_QK, A_QK, A_WIDTH, A_WIDTH, 2 * D_MODEL)

LANES = 128
MXU_N = 256
BF16_ROWS = 16
A_V_ROWS = A_V_DIM + BF16_ROWS
HALO = 16
VMEM_LIMIT = 56 * 1024 * 1024

F32 = jnp.float32
BF16 = jnp.bfloat16
_NT = (((1,), (1,)), ((), ()))
_TN = (((0,), (0,)), ((), ()))


def _rms(xf):
    return xf * lax.rsqrt(jnp.mean(xf * xf, axis=-1, keepdims=True) + EPS)


def _silu(y):
    return y * jax.nn.sigmoid(y)


def _inproj_body(tps, tm, xprev_ref, x_ref, xnext_ref, preg_ref, wmqk_ref, w1_ref, wavt_ref, wgt_ref,
                 convw_ref, convb_ref, rc_ref, rs1_ref, rs2_ref,
                 mq_ref, mk_ref, mv_ref, gt_ref, aq_ref, ak_ref, avt_ref, xext_ref):
    t = lax.rem(pl.program_id(0), tps)
    g = preg_ref[...]
    xn = (_rms(x_ref[...]) * g).astype(BF16)
    xp = jnp.where(t != 0, _rms(xprev_ref[...]) * g, 0.0).astype(BF16)
    xq = jnp.where(t != tps - 1, _rms(xnext_ref[...]) * g, 0.0).astype(BF16)
    xext_ref[0:HALO, :] = xp
    xext_ref[HALO:HALO + tm, :] = xn
    xext_ref[HALO + tm:, :] = xq
    n_ext = tm + 2 * HALO
    ext = jnp.dot(xext_ref[...], wmqk_ref[...], preferred_element_type=F32)
    left = CONV_WIDTH // 2
    y = None
    for j in range(CONV_WIDTH):
        sh = (left - j) % n_ext
        r = ext if sh == 0 else pltpu.roll(ext, sh, axis=0)
        term = convw_ref[j:j + 1, :] * r[HALO:HALO + tm, :]
        y = term if y is None else y + term
    y = _silu(y + convb_ref[...])
    qscale = M_QK_DIM ** -0.5
    for h in range(M_HEADS):
        mq_ref[0, h] = (y[:, h * M_QK_DIM:(h + 1) * M_QK_DIM] * qscale).astype(BF16)
        mk_ref[0, h] = y[:, M_QK + h * M_QK_DIM:M_QK + (h + 1) * M_QK_DIM].astype(BF16)

    r1 = jnp.dot(xn, w1_ref[...], preferred_element_type=F32)
    for h in range(M_HEADS):
        mv_ref[0, h] = r1[:, h * M_V_DIM:(h + 1) * M_V_DIM].astype(BF16)
    rc, rs1, rs2 = rc_ref[...], rs1_ref[...], rs2_ref[...]
    half = ROPE_DIMS // 2

    def rope(a):
        return a * rc + pltpu.roll(a, LANES - half, axis=1) * rs1 + pltpu.roll(a, half, axis=1) * rs2

    ascale = A_HEAD_DIM ** -0.5 * math.log2(math.e)
    for h in range(A_HEADS):
        a = r1[:, M_WIDTH + h * LANES:M_WIDTH + (h + 1) * LANES]
        aq_ref[0, h] = (rope(a) * ascale).astype(BF16)
        a = r1[:, M_WIDTH + A_QK + h * LANES:M_WIDTH + A_QK + (h + 1) * LANES]
        ak_ref[0, h] = rope(a).astype(BF16)

    avt = lax.dot_general(wavt_ref[...], xn, _NT, preferred_element_type=F32)
    ones_tile = (lax.broadcasted_iota(jnp.int32, (BF16_ROWS, tm), 0) == 0).astype(BF16)
    for h in range(A_HEADS):
        avt_ref[0, 0, h, 0:A_V_DIM, :] = avt[h * A_V_DIM:(h + 1) * A_V_DIM, :].astype(BF16)
        avt_ref[0, 0, h, A_V_DIM:A_V_ROWS, :] = ones_tile
    gt_ref[...] = lax.dot_general(wgt_ref[...], xn, _NT, preferred_element_type=F32)


def _inproj_call(x2d, preg, wmqk, w1, wavt, wgt, convw, convb, rc, rs1, rs2, B, S, tm):
    T = B * S
    tps = S // tm
    nh = tm // HALO
    const = lambda i: (0, 0)
    in_specs = [
        pl.BlockSpec((HALO, D_MODEL), lambda i: (jnp.maximum(i * nh - 1, 0), 0)),
        pl.BlockSpec((tm, D_MODEL), lambda i: (i, 0)),
        pl.BlockSpec((HALO, D_MODEL), lambda i: (jnp.minimum((i + 1) * nh, T // HALO - 1), 0)),
        pl.BlockSpec((1, D_MODEL), const),
        pl.BlockSpec(wmqk.shape, const),
        pl.BlockSpec(w1.shape, const),
        pl.BlockSpec(wavt.shape, const),
        pl.BlockSpec(wgt.shape, const),
        pl.BlockSpec(convw.shape, const),
        pl.BlockSpec(convb.shape, const),
        pl.BlockSpec((tm, LANES), lambda i: (lax.rem(i, tps), 0)),
        pl.BlockSpec((tm, LANES), lambda i: (lax.rem(i, tps), 0)),
        pl.BlockSpec((tm, LANES), lambda i: (lax.rem(i, tps), 0)),
    ]
    bt = lambda i: (i // tps, 0, lax.rem(i, tps), 0)
    out_shape = [
        jax.ShapeDtypeStruct((B, M_HEADS, S, M_QK_DIM), BF16),
        jax.ShapeDtypeStruct((B, M_HEADS, S, M_QK_DIM), BF16),
        jax.ShapeDtypeStruct((B, M_HEADS, S, M_V_DIM), BF16),
        jax.ShapeDtypeStruct((8 * M_HEADS, T), F32),
        jax.ShapeDtypeStruct((B, A_HEADS, S, LANES), BF16),
        jax.ShapeDtypeStruct((B, A_HEADS, S, LANES), BF16),
        jax.ShapeDtypeStruct((B, tps, A_HEADS, A_V_ROWS, tm), BF16),
    ]
    out_specs = [
        pl.BlockSpec((1, M_HEADS, tm, M_QK_DIM), bt),
        pl.BlockSpec((1, M_HEADS, tm, M_QK_DIM), bt),
        pl.BlockSpec((1, M_HEADS, tm, M_V_DIM), bt),
        pl.BlockSpec((8 * M_HEADS, tm), lambda i: (0, i)),
        pl.BlockSpec((1, A_HEADS, tm, LANES), bt),
        pl.BlockSpec((1, A_HEADS, tm, LANES), bt),
        pl.BlockSpec((1, 1, A_HEADS, A_V_ROWS, tm), lambda i: (i // tps, lax.rem(i, tps), 0, 0, 0)),
    ]
    return pl.pallas_call(
        functools.partial(_inproj_body, tps, tm),
        grid=(T // tm,),
        in_specs=in_specs,
        out_specs=out_specs,
        out_shape=out_shape,
        scratch_shapes=[pltpu.VMEM((tm + 2 * HALO, D_MODEL), BF16)],
        compiler_params=pltpu.CompilerParams(dimension_semantics=("arbitrary",),
                                             vmem_limit_bytes=VMEM_LIMIT),
        name="inproj",
    )(x2d, x2d, x2d, preg, wmqk, w1, wavt, wgt, convw, convb, rc, rs1, rs2)


def _gates_body(tg, gt_ref, gb_ref, rows_ref, cols_ref):
    g = gt_ref[...] + gb_ref[...]
    lf = jnp.minimum(g, 0.0) - jnp.log1p(jnp.exp(-jnp.abs(g)))
    pos = lax.broadcasted_iota(jnp.int32, g.shape, 1) & (M_CHUNK - 1)
    pre, suf = lf, lf
    s = 1
    while s < M_CHUNK:
        pre = pre + jnp.where(pos >= s, pltpu.roll(pre, s, axis=1), 0.0)
        suf = suf + jnp.where(pos < M_CHUNK - s, pltpu.roll(suf, tg - s, axis=1), 0.0)
        s *= 2
    rid = lax.broadcasted_iota(jnp.int32, g.shape, 0) & 7
    r = jnp.where(rid == 0, pre, jnp.where(rid == 2, suf, jnp.where((rid == 1) | (rid == 3), g, 0.0)))
    pad = jnp.zeros((LANES - 8, tg), F32)
    for h in range(M_HEADS):
        r8 = r[8 * h:8 * h + 8, :]
        rows_ref[0, h] = r8
        cols_ref[0, h] = jnp.concatenate([r8, pad], axis=0).T


def _gates_call(gt, gb, B, S, tg):
    nt = S // tg
    return pl.pallas_call(
        functools.partial(_gates_body, tg),
        grid=(B, nt),
        in_specs=[pl.BlockSpec((8 * M_HEADS, tg), lambda b, i: (0, b * nt + i)),
                  pl.BlockSpec((8 * M_HEADS, 1), lambda b, i: (0, 0))],
        out_specs=[pl.BlockSpec((1, M_HEADS, 8, tg), lambda b, i: (b, 0, 0, i)),
                   pl.BlockSpec((1, M_HEADS, tg, LANES), lambda b, i: (b, 0, i, 0))],
        out_shape=[jax.ShapeDtypeStruct((B, M_HEADS, 8, S), F32),
                   jax.ShapeDtypeStruct((B, M_HEADS, S, LANES), F32)],
        compiler_params=pltpu.CompilerParams(dimension_semantics=("arbitrary", "arbitrary"),
                                             vmem_limit_bytes=VMEM_LIMIT),
        name="gates",
    )(gt, gb)


def _mlstm_body(rev, nchunk, q_ref, k_ref, v_ref, rows_ref, cols_ref, h_ref, c_ref, n_ref, m_ref):
    @pl.when(pl.program_id(2) == 0)
    def _():
        c_ref[...] = jnp.zeros_like(c_ref)
        n_ref[...] = jnp.zeros_like(n_ref)
        m_ref[...] = jnp.zeros_like(m_ref)

    L = M_CHUNK
    ar, ir = (2, 3) if rev else (0, 1)
    row_i = lax.broadcasted_iota(jnp.int32, (L, L), 0)
    col_i = lax.broadcasted_iota(jnp.int32, (L, L), 1)
    visible = (col_i >= row_i) if rev else (col_i <= row_i)
    C, n, m = c_ref[...], n_ref[...], m_ref[...]
    order = range(nchunk - 1, -1, -1) if rev else range(nchunk)
    for c in order:
        r0 = c * L
        qc = q_ref[0, 0, r0:r0 + L, :]
        kc = k_ref[0, 0, r0:r0 + L, :]
        vc = v_ref[0, 0, r0:r0 + L, :]
        a_col = cols_ref[0, 0, r0:r0 + L, ar:ar + 1]
        i_col = cols_ref[0, 0, r0:r0 + L, ir:ir + 1]
        a_row = rows_ref[0, 0, ar:ar + 1, r0:r0 + L]
        i_row = rows_ref[0, 0, ir:ir + 1, r0:r0 + L]
        d = jnp.where(visible, a_col - a_row + i_row, -jnp.inf)
        inter = a_col + m
        mj = jnp.maximum(inter, jnp.max(d, axis=1, keepdims=True))
        w_inter = jnp.exp(inter - mj)
        s = lax.dot_general(qc, kc, _NT, preferred_element_type=F32) * jnp.exp(d - mj)
        num = (w_inter * jnp.dot(qc, C.astype(BF16), preferred_element_type=F32)
               + jnp.dot(s.astype(BF16), vc, preferred_element_type=F32))
        den = (w_inter * jnp.sum(qc.astype(F32) * n, axis=1, keepdims=True)
               + jnp.sum(s, axis=1, keepdims=True))
        h_ref[0, r0:r0 + L, :] = num / jnp.maximum(jnp.abs(den), jnp.exp(-mj))
        a_tot = a_col[0:1, :] if rev else a_col[L - 1:L, :]
        g_row = a_tot - a_row + i_row
        g_col = a_tot - a_col + i_col
        m_new = jnp.maximum(a_tot + m, jnp.max(g_row, axis=1, keepdims=True))
        wk = jnp.exp(g_col - m_new)
        decay = jnp.exp(a_tot + m - m_new)
        wv = (wk * vc.astype(F32)).astype(BF16)
        C = decay * C + lax.dot_general(kc, wv, _TN, preferred_element_type=F32)
        n = decay * n + jnp.sum(wk * kc.astype(F32), axis=0, keepdims=True)
        m = m_new
    c_ref[...] = C
    n_ref[...] = n
    m_ref[...] = m


def _mlstm_call(rev, mq, mk, mv, rows, cols, B, S, tr):
    ns = S // tr
    pos = (lambda s: ns - 1 - s) if rev else (lambda s: s)
    return pl.pallas_call(
        functools.partial(_mlstm_body, rev, tr // M_CHUNK),
        grid=(B, M_HEADS, ns),
        in_specs=[pl.BlockSpec((1, 1, tr, M_QK_DIM), lambda b, h, s: (b, h, pos(s), 0)),
                  pl.BlockSpec((1, 1, tr, M_QK_DIM), lambda b, h, s: (b, h, pos(s), 0)),
                  pl.BlockSpec((1, 1, tr, M_V_DIM), lambda b, h, s: (b, h, pos(s), 0)),
                  pl.BlockSpec((1, 1, 8, tr), lambda b, h, s: (b, h, 0, pos(s))),
                  pl.BlockSpec((1, 1, tr, LANES), lambda b, h, s: (b, h, pos(s), 0))],
        out_specs=pl.BlockSpec((1, tr, M_V_DIM), lambda b, h, s: (b, pos(s), h)),
        out_shape=jax.ShapeDtypeStruct((B, S, M_WIDTH), F32),
        scratch_shapes=[pltpu.VMEM((M_QK_DIM, M_V_DIM), F32),
                        pltpu.VMEM((1, M_QK_DIM), F32),
                        pltpu.VMEM((1, 1), F32)],
        compiler_params=pltpu.CompilerParams(dimension_semantics=("arbitrary", "arbitrary", "arbitrary"),
                                             vmem_limit_bytes=VMEM_LIMIT),
        name="mlstm_bwd" if rev else "mlstm_fwd",
    )(mq, mk, mv, rows, cols)


def _attn_body(nkc, tq, tk, tv, lam_init, q_ref, k_ref, vt_ref, lq_ref, g_ref, o_ref,
               qz_ref, st0_ref, st1_ref, m_ref, acc_ref):
    q = q_ref[0, 0]
    lane = lax.broadcasted_iota(jnp.int32, q.shape, 1)
    zero = jnp.zeros_like(q)
    qz_ref[0:tq, :] = jnp.where(lane < A_HEAD_DIM, q, zero)
    qz_ref[tq:2 * tq, :] = jnp.where(lane >= A_HEAD_DIM, q, zero)
    m_ref[...] = jnp.full(m_ref.shape, -jnp.inf, F32)
    acc_ref[...] = jnp.zeros_like(acc_ref)

    strips = [slice(c * MXU_N, (c + 1) * MXU_N) for c in range(2 * tq // MXU_N)]
    nv = tk // tv

    def score_strip(j, st_ref, sl):
        kj = k_ref[0, 0, pl.ds(pl.multiple_of(j * tk, tk), tk), :]
        st = lax.dot_general(kj, qz_ref[sl, :], _NT, preferred_element_type=F32)
        st_ref[:, sl] = st
        return jnp.max(st, axis=0, keepdims=True)

    def accumulate_strip(j, st_ref, sl, cm):
        m_old = m_ref[:, sl]
        m_new = jnp.maximum(m_old, cm)
        alpha = jnp.exp2(m_old - m_new)
        p = jnp.exp2(st_ref[:, sl] - m_new).astype(BF16)
        pv = jnp.dot(vt_ref[0, j * nv, 0], p[0:tv], preferred_element_type=F32)
        for u in range(1, nv):
            pv = pv + jnp.dot(vt_ref[0, j * nv + u, 0], p[u * tv:(u + 1) * tv],
                              preferred_element_type=F32)
        acc_ref[:, sl] = alpha * acc_ref[:, sl] + pv
        m_ref[:, sl] = m_new

    def step(j_next, st_next_ref, j_cur, st_cur_ref, cmax_cur):
        cmax_next = []
        for sl, cm in zip(strips, cmax_cur):
            cmax_next.append(score_strip(j_next, st_next_ref, sl))
            accumulate_strip(j_cur, st_cur_ref, sl, cm)
        return tuple(cmax_next)

    def pair(i, cmax0):
        j = 2 * i
        cmax1 = step(j + 1, st1_ref, j, st0_ref, cmax0)
        return step(j + 2, st0_ref, j + 1, st1_ref, cmax1)

    cmax0 = tuple(score_strip(0, st0_ref, sl) for sl in strips)
    cmax0 = lax.fori_loop(0, nkc // 2 - 1, pair, cmax0)
    cmax1 = step(nkc - 1, st1_ref, nkc - 2, st0_ref, cmax0)
    for sl, cm in zip(strips, cmax1):
        accumulate_strip(nkc - 1, st1_ref, sl, cm)

    lq = lq_ref[...]
    lam = (jnp.exp(jnp.sum(lq[0:1] * lq[1:2], axis=1, keepdims=True))
           - jnp.exp(jnp.sum(lq[2:3] * lq[3:4], axis=1, keepdims=True)) + lam_init)
    accn = acc_ref[0:A_V_DIM, :] / acc_ref[A_V_DIM:A_V_DIM + 1, :]
    ot = accn[:, 0:tq] - lam * accn[:, tq:2 * tq]
    otn = ot * lax.rsqrt(jnp.mean(ot * ot, axis=0, keepdims=True) + EPS)
    o_ref[...] = (otn.T * g_ref[...]) * (1.0 - lam_init)


def _attn_call(aq, ak, avt, lq, ag, B, S, tq, tk, tv, lam_init):
    nkc = S // tk
    nq = S // tq
    assert nkc >= 2 and nkc % 2 == 0, "key chunks are processed in pairs"
    assert tk % tv == 0
    return pl.pallas_call(
        functools.partial(_attn_body, nkc, tq, tk, tv, lam_init),
        grid=(B, A_HEADS, nq),
        in_specs=[pl.BlockSpec((1, 1, tq, LANES), lambda b, h, i: (b, h, i, 0)),
                  pl.BlockSpec((1, 1, S, LANES), lambda b, h, i: (b, h, 0, 0)),
                  pl.BlockSpec((1, S // tv, 1, A_V_ROWS, tv), lambda b, h, i: (b, 0, h, 0, 0)),
                  pl.BlockSpec(lq.shape, lambda b, h, i: (0, 0)),
                  pl.BlockSpec(ag.shape, lambda b, h, i: (0, 0))],
        out_specs=pl.BlockSpec((tq, A_V_DIM), lambda b, h, i: (b * nq + i, h)),
        out_shape=jax.ShapeDtypeStruct((B * S, A_WIDTH), F32),
        scratch_shapes=[pltpu.VMEM((2 * tq, LANES), BF16),
                        pltpu.VMEM((tk, 2 * tq), F32),
                        pltpu.VMEM((tk, 2 * tq), F32),
                        pltpu.VMEM((1, 2 * tq), F32),
                        pltpu.VMEM((A_V_ROWS, 2 * tq), F32)],
        compiler_params=pltpu.CompilerParams(dimension_semantics=("arbitrary", "arbitrary", "arbitrary"),
                                             vmem_limit_bytes=VMEM_LIMIT),
        name="attn",
    )(aq, ak, avt, lq, ag)


def _epilogue_body(x_ref, hf_ref, hb_ref, o_ref, preg_ref, w4_ref, mng_ref, bm_ref, wbm_ref, wba_ref,
                   wout_ref, postg_ref, out_ref):
    x = x_ref[...]
    xn = (_rms(x) * preg_ref[...]).astype(BF16)

    def proj(lo, hi):
        return jnp.dot(xn, w4_ref[:, lo:hi], preferred_element_type=F32)

    h = hf_ref[...] + hb_ref[...]
    mng = mng_ref[...]
    hn = jnp.concatenate(
        [_rms(h[:, k * M_V_DIM:(k + 1) * M_V_DIM]) * mng[:, k * M_V_DIM:(k + 1) * M_V_DIM]
         for k in range(M_HEADS)], axis=1)
    h_a = jax.nn.sigmoid(proj(0, M_WIDTH)) * hn * _silu(proj(M_WIDTH, 2 * M_WIDTH))
    ya = jnp.dot(h_a.astype(BF16), wbm_ref[...], preferred_element_type=F32)
    h_b = o_ref[...] * _silu(proj(2 * M_WIDTH, 2 * M_WIDTH + A_WIDTH))
    yb = jnp.dot(h_b.astype(BF16), wba_ref[...], preferred_element_type=F32)
    g0 = 2 * M_WIDTH + A_WIDTH
    bm = bm_ref[...]
    g_a = jax.nn.sigmoid(proj(g0, g0 + D_MODEL) + bm[:, 0:D_MODEL])
    g_b = jax.nn.sigmoid(proj(g0 + D_MODEL, g0 + 2 * D_MODEL) + bm[:, D_MODEL:2 * D_MODEL])
    y = g_a * ya + g_b * yb
    out = jnp.dot(y.astype(BF16), wout_ref[...], preferred_element_type=F32)
    out_ref[...] = x + _rms(out) * postg_ref[...]


def _epilogue_call(x2d, hf, hb, o, preg, w4, mng, bm, wbm, wba, wout, postg, te):
    T = x2d.shape[0]
    row = lambda i: (i, 0)
    const = lambda i: (0, 0)
    single = pl.Buffered(1)
    return pl.pallas_call(
        _epilogue_body,
        grid=(T // te,),
        in_specs=[pl.BlockSpec((te, D_MODEL), row),
                  pl.BlockSpec((te, M_WIDTH), row),
                  pl.BlockSpec((te, M_WIDTH), row),
                  pl.BlockSpec((te, A_WIDTH), row),
                  pl.BlockSpec(preg.shape, const),
                  pl.BlockSpec(w4.shape, const, pipeline_mode=single),
                  pl.BlockSpec(mng.shape, const),
                  pl.BlockSpec(bm.shape, const),
                  pl.BlockSpec(wbm.shape, const, pipeline_mode=single),
                  pl.BlockSpec(wba.shape, const, pipeline_mode=single),
                  pl.BlockSpec(wout.shape, const, pipeline_mode=single),
                  pl.BlockSpec(postg.shape, const)],
        out_specs=pl.BlockSpec((te, D_MODEL), row),
        out_shape=jax.ShapeDtypeStruct((T, D_MODEL), F32),
        compiler_params=pltpu.CompilerParams(dimension_semantics=("arbitrary",),
                                             vmem_limit_bytes=VMEM_LIMIT),
        name="epilogue",
    )(x2d, hf, hb, o, preg, w4, mng, bm, wbm, wba, wout, postg)


def _rope_tables(S):
    half = ROPE_DIMS // 2
    inv = ROPE_THETA ** (-jnp.arange(0, ROPE_DIMS, 2, dtype=F32) / ROPE_DIMS)
    ang = jnp.arange(S, dtype=F32)[:, None] * inv[None, :]
    cos, sin = jnp.cos(ang), jnp.sin(ang)
    one = jnp.ones((S, A_HEAD_DIM - ROPE_DIMS), F32)
    zero = jnp.zeros((S, A_HEAD_DIM - ROPE_DIMS), F32)
    zh = jnp.zeros((S, half), F32)
    rc = jnp.concatenate([cos, cos, one], axis=1)
    rs1 = jnp.concatenate([-sin, zh, zero], axis=1)
    rs2 = jnp.concatenate([zh, sin, zero], axis=1)
    rep = LANES // A_HEAD_DIM
    return jnp.tile(rc, (1, rep)), jnp.tile(rs1, (1, rep)), jnp.tile(rs2, (1, rep))


def _gate_layout(m):
    kinds = jnp.stack([m[..., 1, :], m[..., 0, :], m[..., 3, :], m[..., 2, :]], axis=-1)
    kinds = jnp.concatenate([kinds, jnp.zeros_like(kinds)], axis=-1)
    return kinds.reshape(kinds.shape[:-2] + (8 * M_HEADS,))


def _layer(x, layer_idx, pre_g, w_in, conv_w, conv_b, gate_b, m_norm_g, lambda_qk, a_norm_g,
           w_branch_m, w_branch_a, b_merge, w_out, post_g):
    B, S, _ = x.shape
    T = B * S
    tm = min(512, S)
    tq = min(512, S)
    tk = min(1024, S // 2)
    tr = min(512, S)
    tg = min(2048, S)
    te = min(256, T)
    lam_init = 0.8 - 0.6 * math.exp(-0.3 * layer_idx)

    cuts = [0]
    for sz in SPLIT_SIZES:
        cuts.append(cuts[-1] + sz)
    col = lambda k: w_in[:, cuts[k]:cuts[k + 1]]
    wmqk = col(0).astype(BF16)
    w1 = jnp.concatenate([col(1), col(5), col(6)], axis=1).astype(BF16)
    wavt = col(7).T.astype(BF16)
    wgt = _gate_layout(col(2).reshape(D_MODEL, 4, M_HEADS)).T.astype(BF16)
    gb = _gate_layout(gate_b).reshape(8 * M_HEADS, 1).astype(F32)
    w4 = jnp.concatenate([col(3), col(4), col(8), col(9)], axis=1).astype(BF16)
    rc, rs1, rs2 = _rope_tables(S)

    x2d = x.reshape(T, D_MODEL)
    preg = pre_g.reshape(1, D_MODEL)
    mq, mk, mv, gt, aq, ak, avt = _inproj_call(
        x2d, preg, wmqk, w1, wavt, wgt, conv_w, conv_b.reshape(1, 2 * M_QK), rc, rs1, rs2, B, S, tm)
    rows, cols = _gates_call(gt, gb, B, S, tg)
    hf = _mlstm_call(False, mq, mk, mv, rows, cols, B, S, tr)
    hb = _mlstm_call(True, mq, mk, mv, rows, cols, B, S, tr)
    o = _attn_call(aq, ak, avt, lambda_qk.astype(F32), a_norm_g.reshape(1, A_V_DIM), B, S, tq, tk, tm,
                   lam_init)
    out = _epilogue_call(
        x2d, hf.reshape(T, M_WIDTH), hb.reshape(T, M_WIDTH), o, preg, w4,
        m_norm_g.reshape(1, M_WIDTH), b_merge.reshape(1, 2 * D_MODEL),
        w_branch_m.astype(BF16), w_branch_a.astype(BF16), w_out.astype(BF16),
        post_g.reshape(1, D_MODEL), te)
    return out.reshape(B, S, D_MODEL)


def kernel(x, pre_norm_g, w_in, conv_w, conv_b, gate_b, m_norm_g, lambda_qk, a_norm_g, w_branch_m,
           w_branch_a, b_merge, w_out, post_norm_g):
    for l in range(pre_norm_g.shape[0]):
        x = _layer(x, l, pre_norm_g[l], w_in[l], conv_w[l], conv_b[l], gate_b[l], m_norm_g[l],
                   lambda_qk[l], a_norm_g[l], w_branch_m[l], w_branch_a[l], b_merge[l], w_out[l],
                   post_norm_g[l])
    return x
```

```python
import functools
import math

import jax
import jax.numpy as jnp
from jax import lax
from jax.experimental import pallas as pl
from jax.experimental.pallas import tpu as pltpu

D_MODEL = 1024
M_HEADS = 4
M_QK_DIM = 128
M_V_DIM = 256
M_QK = M_HEADS * M_QK_DIM
M_WIDTH = M_HEADS * M_V_DIM
M_CHUNK = 64
CONV_WIDTH = 4
A_HEADS = 8
A_HEAD_DIM = 64
A_V_DIM = 2 * A_HEAD_DIM
A_QK = A_HEADS * 2 * A_HEAD_DIM
A_WIDTH = A_HEADS * A_V_DIM
ROPE_THETA = 500000.0
ROPE_DIMS = A_HEAD_DIM // 4
EPS = 1e-6
SPLIT_SIZES = (2 * M_QK, M_WIDTH, 4 * M_HEADS, M_WIDTH, M_WIDTH, A_QK, A_QK, A_WIDTH, A_WIDTH, 2 * D_MODEL)

LANES = 128
MXU_N = 256
BF16_ROWS = 16
A_V_ROWS = A_V_DIM + BF16_ROWS
HALO = 16
N_GATE_ROWS = 2 * 8 * M_HEADS
VMEM_LIMIT = 56 * 1024 * 1024

F32 = jnp.float32
BF16 = jnp.bfloat16
_NT = (((1,), (1,)), ((), ()))
_TN = (((0,), (0,)), ((), ()))


def _rms(xf):
    return xf * lax.rsqrt(jnp.mean(xf * xf, axis=-1, keepdims=True) + EPS)


def _silu(y):
    return y * jax.nn.sigmoid(y)


def _inproj_body(tps, tm, xprev_ref, x_ref, xnext_ref, preg_ref, wmqk_ref, w1_ref, wavt_ref, wgt_ref,
                 convw_ref, convb_ref, rc_ref, rs1_ref, rs2_ref,
                 mq_ref, mk_ref, mv_ref, gt_ref, aq_ref, ak_ref, avt_ref, xext_ref):
    t = lax.rem(pl.program_id(0), tps)
    g = preg_ref[...]
    xn = (_rms(x_ref[...]) * g).astype(BF16)
    xp = jnp.where(t != 0, _rms(xprev_ref[...]) * g, 0.0).astype(BF16)
    xq = jnp.where(t != tps - 1, _rms(xnext_ref[...]) * g, 0.0).astype(BF16)
    xext_ref[0:HALO, :] = xp
    xext_ref[HALO:HALO + tm, :] = xn
    xext_ref[HALO + tm:, :] = xq
    n_ext = tm + 2 * HALO
    ext = jnp.dot(xext_ref[...], wmqk_ref[...], preferred_element_type=F32)
    left = CONV_WIDTH // 2
    y = None
    for j in range(CONV_WIDTH):
        sh = (left - j) % n_ext
        r = ext if sh == 0 else pltpu.roll(ext, sh, axis=0)
        term = convw_ref[j:j + 1, :] * r[HALO:HALO + tm, :]
        y = term if y is None else y + term
    y = _silu(y + convb_ref[...])
    qscale = M_QK_DIM ** -0.5
    for h in range(M_HEADS):
        mq_ref[0, h] = (y[:, h * M_QK_DIM:(h + 1) * M_QK_DIM] * qscale).astype(BF16)
        mk_ref[0, h] = y[:, M_QK + h * M_QK_DIM:M_QK + (h + 1) * M_QK_DIM].astype(BF16)

    r1 = jnp.dot(xn, w1_ref[...], preferred_element_type=F32)
    for h in range(M_HEADS):
        mv_ref[0, h] = r1[:, h * M_V_DIM:(h + 1) * M_V_DIM].astype(BF16)
    rc, rs1, rs2 = rc_ref[...], rs1_ref[...], rs2_ref[...]
    half = ROPE_DIMS // 2

    def rope(a):
        return a * rc + pltpu.roll(a, LANES - half, axis=1) * rs1 + pltpu.roll(a, half, axis=1) * rs2

    ascale = A_HEAD_DIM ** -0.5 * math.log2(math.e)
    for h in range(A_HEADS):
        a = r1[:, M_WIDTH + h * LANES:M_WIDTH + (h + 1) * LANES]
        aq_ref[0, h] = (rope(a) * ascale).astype(BF16)
        a = r1[:, M_WIDTH + A_QK + h * LANES:M_WIDTH + A_QK + (h + 1) * LANES]
        ak_ref[0, h] = rope(a).astype(BF16)

    avt = lax.dot_general(wavt_ref[...], xn, _NT, preferred_element_type=F32)
    ones_tile = (lax.broadcasted_iota(jnp.int32, (BF16_ROWS, tm), 0) == 0).astype(BF16)
    for h in range(A_HEADS):
        avt_ref[0, 0, h, 0:A_V_DIM, :] = avt[h * A_V_DIM:(h + 1) * A_V_DIM, :].astype(BF16)
        avt_ref[0, 0, h, A_V_DIM:A_V_ROWS, :] = ones_tile
    gt_ref[...] = lax.dot_general(wgt_ref[...], xn, _NT, preferred_element_type=F32)


def _inproj_call(x2d, preg, wmqk, w1, wavt, wgt, convw, convb, rc, rs1, rs2, B, S, tm):
    T = B * S
    tps = S // tm
    nh = tm // HALO
    const = lambda i: (0, 0)
    in_specs = [
        pl.BlockSpec((HALO, D_MODEL), lambda i: (jnp.maximum(i * nh - 1, 0), 0)),
        pl.BlockSpec((tm, D_MODEL), lambda i: (i, 0)),
        pl.BlockSpec((HALO, D_MODEL), lambda i: (jnp.minimum((i + 1) * nh, T // HALO - 1), 0)),
        pl.BlockSpec((1, D_MODEL), const),
        pl.BlockSpec(wmqk.shape, const),
        pl.BlockSpec(w1.shape, const),
        pl.BlockSpec(wavt.shape, const),
        pl.BlockSpec(wgt.shape, const),
        pl.BlockSpec(convw.shape, const),
        pl.BlockSpec(convb.shape, const),
        pl.BlockSpec((tm, LANES), lambda i: (lax.rem(i, tps), 0)),
        pl.BlockSpec((tm, LANES), lambda i: (lax.rem(i, tps), 0)),
        pl.BlockSpec((tm, LANES), lambda i: (lax.rem(i, tps), 0)),
    ]
    bt = lambda i: (i // tps, 0, lax.rem(i, tps), 0)
    out_shape = [
        jax.ShapeDtypeStruct((B, M_HEADS, S, M_QK_DIM), BF16),
        jax.ShapeDtypeStruct((B, M_HEADS, S, M_QK_DIM), BF16),
        jax.ShapeDtypeStruct((B, M_HEADS, S, M_V_DIM), BF16),
        jax.ShapeDtypeStruct((N_GATE_ROWS, T), F32),
        jax.ShapeDtypeStruct((B, A_HEADS, S, LANES), BF16),
        jax.ShapeDtypeStruct((B, A_HEADS, S, LANES), BF16),
        jax.ShapeDtypeStruct((B, tps, A_HEADS, A_V_ROWS, tm), BF16),
    ]
    out_specs = [
        pl.BlockSpec((1, M_HEADS, tm, M_QK_DIM), bt),
        pl.BlockSpec((1, M_HEADS, tm, M_QK_DIM), bt),
        pl.BlockSpec((1, M_HEADS, tm, M_V_DIM), bt),
        pl.BlockSpec((N_GATE_ROWS, tm), lambda i: (0, i)),
        pl.BlockSpec((1, A_HEADS, tm, LANES), bt),
        pl.BlockSpec((1, A_HEADS, tm, LANES), bt),
        pl.BlockSpec((1, 1, A_HEADS, A_V_ROWS, tm), lambda i: (i // tps, lax.rem(i, tps), 0, 0, 0)),
    ]
    return pl.pallas_call(
        functools.partial(_inproj_body, tps, tm),
        grid=(T // tm,),
        in_specs=in_specs,
        out_specs=out_specs,
        out_shape=out_shape,
        scratch_shapes=[pltpu.VMEM((tm + 2 * HALO, D_MODEL), BF16)],
        compiler_params=pltpu.CompilerParams(dimension_semantics=("arbitrary",),
                                             vmem_limit_bytes=VMEM_LIMIT),
        name="inproj",
    )(x2d, x2d, x2d, preg, wmqk, w1, wavt, wgt, convw, convb, rc, rs1, rs2)


N_GATE_COLS = 5


def _gates_body(S, ct, f_ref, i_ref, fb_ref, ib_ref, rows_ref, cols_ref):
    L = M_CHUNK
    f = f_ref[...] + fb_ref[...]
    ig = i_ref[...] + ib_ref[...]
    lf = jnp.minimum(f, 0.0) - jnp.log1p(jnp.exp(-jnp.abs(f)))
    lane = lax.broadcasted_iota(jnp.int32, f.shape, 1)
    pos = lane & (L - 1)
    fwd = lax.broadcasted_iota(jnp.int32, f.shape, 0) == 0

    def from_left(x, s):
        return pltpu.roll(x, s, axis=1)

    def from_right(x, s):
        return pltpu.roll(x, S - s, axis=1)

    def chunk_scans(x, op, ident):
        pre, suf = x, x
        s = 1
        while s < L:
            pre = op(pre, jnp.where(pos >= s, from_left(pre, s), ident))
            suf = op(suf, jnp.where(pos < L - s, from_right(suf, s), ident))
            s *= 2
        return pre, suf

    pre, suf = chunk_scans(lf, jnp.add, 0.0)
    a = jnp.where(fwd, pre, suf)
    a_tot = pre + suf - lf
    w = ig - a
    wpre, wsuf = chunk_scans(w, jnp.maximum, -jnp.inf)
    rowmax = a + jnp.where(fwd, wpre, wsuf)
    g_max = a_tot + jnp.maximum(wpre, wsuf)

    af, gf, ab, gb = a_tot, g_max, a_tot, g_max
    s = L
    while s < S:
        ok = lane >= s
        a_sh, g_sh = from_left(af, s), from_left(gf, s)
        gf = jnp.where(ok, jnp.maximum(g_sh + af, gf), gf)
        af = jnp.where(ok, af + a_sh, af)
        ok = lane < S - s
        a_sh, g_sh = from_right(ab, s), from_right(gb, s)
        gb = jnp.where(ok, jnp.maximum(g_sh + ab, gb), gb)
        ab = jnp.where(ok, ab + a_sh, ab)
        s *= 2
    m_f = jnp.maximum(af, gf)
    m_b = jnp.maximum(ab, gb)
    m_prev = jnp.where(fwd, jnp.where(lane >= L, from_left(m_f, L), 0.0),
                       jnp.where(lane < S - L, from_right(m_b, L), 0.0))

    inter = a + m_prev
    mj = jnp.maximum(inter, rowmax)
    m_new = jnp.maximum(a_tot + m_prev, g_max)
    cols = (a - mj, jnp.exp(inter - mj), jnp.exp(-mj), jnp.exp(a_tot + w - m_new),
            jnp.exp(a_tot + m_prev - m_new))
    rows_ref[0, 0] = w
    pad = jnp.zeros((LANES - 8 * N_GATE_COLS, ct), F32)
    for t in range(S // ct):
        sl = slice(t * ct, (t + 1) * ct)
        cols_ref[0, 0, sl, :] = jnp.concatenate([c[:, sl] for c in cols] + [pad], axis=0).T


def _gates_call(gt, gb, B, S):
    ct = min(2048, S)
    return pl.pallas_call(
        functools.partial(_gates_body, S, ct),
        grid=(B, M_HEADS),
        in_specs=[pl.BlockSpec((8, S), lambda b, h: (h, b)),
                  pl.BlockSpec((8, S), lambda b, h: (M_HEADS + h, b)),
                  pl.BlockSpec((8, 1), lambda b, h: (h, 0)),
                  pl.BlockSpec((8, 1), lambda b, h: (M_HEADS + h, 0))],
        out_specs=[pl.BlockSpec((1, 1, 8, S), lambda b, h: (b, h, 0, 0)),
                   pl.BlockSpec((1, 1, S, LANES), lambda b, h: (b, h, 0, 0))],
        out_shape=[jax.ShapeDtypeStruct((B, M_HEADS, 8, S), F32),
                   jax.ShapeDtypeStruct((B, M_HEADS, S, LANES), F32)],
        compiler_params=pltpu.CompilerParams(dimension_semantics=("arbitrary", "arbitrary"),
                                             vmem_limit_bytes=VMEM_LIMIT),
        name="gates",
    )(gt, gt, gb, gb)


def _mlstm_body(rev, nchunk, q_ref, k_ref, v_ref, rows_ref, cols_ref, h_ref, c_ref):
    @pl.when(pl.program_id(2) == 0)
    def _():
        c_ref[...] = jnp.zeros_like(c_ref)

    L = M_CHUNK
    d = 1 if rev else 0
    row_i = lax.broadcasted_iota(jnp.int32, (L, L), 0)
    col_i = lax.broadcasted_iota(jnp.int32, (L, L), 1)
    visible = (col_i >= row_i) if rev else (col_i <= row_i)
    ones_col = (lax.broadcasted_iota(jnp.int32, (L, LANES), 1) == 0).astype(BF16)
    C = c_ref[...]
    order = range(nchunk - 1, -1, -1) if rev else range(nchunk)
    for c in order:
        r0 = c * L
        qc = q_ref[0, 0, r0:r0 + L, :]
        kc = k_ref[0, 0, r0:r0 + L, :]
        va = jnp.concatenate([v_ref[0, 0, r0:r0 + L, :], ones_col], axis=1)
        u, w_inter, e, wk, decay = [cols_ref[0, 0, r0:r0 + L, 8 * k + d:8 * k + d + 1]
                                    for k in range(N_GATE_COLS)]
        w_row = rows_ref[0, 0, d:d + 1, r0:r0 + L]
        p = jnp.exp(jnp.where(visible, u + w_row, -jnp.inf))
        s = lax.dot_general(qc, kc, _NT, preferred_element_type=F32) * p
        num = (w_inter * jnp.dot(qc, C.astype(BF16), preferred_element_type=F32)
               + jnp.dot(s.astype(BF16), va, preferred_element_type=F32))
        den = num[:, M_V_DIM:M_V_DIM + 1]
        h_ref[0, r0:r0 + L, :] = num[:, 0:M_V_DIM] / jnp.maximum(jnp.abs(den), e)
        wv = (wk * va.astype(F32)).astype(BF16)
        C = decay[0:1, :] * C + lax.dot_general(kc, wv, _TN, preferred_element_type=F32)
    c_ref[...] = C


def _mlstm_call(rev, mq, mk, mv, rows, cols, B, S, tr):
    ns = S // tr
    pos = (lambda s: ns - 1 - s) if rev else (lambda s: s)
    return pl.pallas_call(
        functools.partial(_mlstm_body, rev, tr // M_CHUNK),
        grid=(B, M_HEADS, ns),
        in_specs=[pl.BlockSpec((1, 1, tr, M_QK_DIM), lambda b, h, s: (b, h, pos(s), 0)),
                  pl.BlockSpec((1, 1, tr, M_QK_DIM), lambda b, h, s: (b, h, pos(s), 0)),
                  pl.BlockSpec((1, 1, tr, M_V_DIM), lambda b, h, s: (b, h, pos(s), 0)),
                  pl.BlockSpec((1, 1, 8, tr), lambda b, h, s: (b, h, 0, pos(s))),
                  pl.BlockSpec((1, 1, tr, LANES), lambda b, h, s: (b, h, pos(s), 0))],
        out_specs=pl.BlockSpec((1, tr, M_V_DIM), lambda b, h, s: (b, pos(s), h)),
        out_shape=jax.ShapeDtypeStruct((B, S, M_WIDTH), F32),
        scratch_shapes=[pltpu.VMEM((M_QK_DIM, M_V_DIM + LANES), F32)],
        compiler_params=pltpu.CompilerParams(dimension_semantics=("arbitrary", "arbitrary", "arbitrary"),
                                             vmem_limit_bytes=VMEM_LIMIT),
        name="mlstm_bwd" if rev else "mlstm_fwd",
    )(mq, mk, mv, rows, cols)


def _attn_body(nkc, tq, tk, tv, lam_init, q_ref, k_ref, vt_ref, lq_ref, g_ref, o_ref,
               qz_ref, st0_ref, st1_ref, m_ref, acc_ref):
    q = q_ref[0, 0]
    lane = lax.broadcasted_iota(jnp.int32, q.shape, 1)
    zero = jnp.zeros_like(q)
    qz_ref[0:tq, :] = jnp.where(lane < A_HEAD_DIM, q, zero)
    qz_ref[tq:2 * tq, :] = jnp.where(lane >= A_HEAD_DIM, q, zero)
    m_ref[...] = jnp.full(m_ref.shape, -jnp.inf, F32)
    acc_ref[...] = jnp.zeros_like(acc_ref)

    strips = [slice(c * MXU_N, (c + 1) * MXU_N) for c in range(2 * tq // MXU_N)]
    nv = tk // tv

    def score_strip(j, st_ref, sl):
        kj = k_ref[0, 0, pl.ds(pl.multiple_of(j * tk, tk), tk), :]
        st = lax.dot_general(kj, qz_ref[sl, :], _NT, preferred_element_type=F32)
        st_ref[:, sl] = st
        return jnp.max(st, axis=0, keepdims=True)

    def accumulate_strip(j, st_ref, sl, cm):
        m_old = m_ref[:, sl]
        m_new = jnp.maximum(m_old, cm)
        alpha = jnp.exp2(m_old - m_new)
        p = jnp.exp2(st_ref[:, sl] - m_new).astype(BF16)
        pv = jnp.dot(vt_ref[0, j * nv, 0], p[0:tv], preferred_element_type=F32)
        for u in range(1, nv):
            pv = pv + jnp.dot(vt_ref[0, j * nv + u, 0], p[u * tv:(u + 1) * tv],
                              preferred_element_type=F32)
        acc_ref[:, sl] = alpha * acc_ref[:, sl] + pv
        m_ref[:, sl] = m_new

    def step(j_next, st_next_ref, j_cur, st_cur_ref, cmax_cur):
        cmax_next = []
        for sl, cm in zip(strips, cmax_cur):
            cmax_next.append(score_strip(j_next, st_next_ref, sl))
            accumulate_strip(j_cur, st_cur_ref, sl, cm)
        return tuple(cmax_next)

    def pair(i, cmax0):
        j = 2 * i
        cmax1 = step(j + 1, st1_ref, j, st0_ref, cmax0)
        return step(j + 2, st0_ref, j + 1, st1_ref, cmax1)

    cmax0 = tuple(score_strip(0, st0_ref, sl) for sl in strips)
    cmax0 = lax.fori_loop(0, nkc // 2 - 1, pair, cmax0)
    cmax1 = step(nkc - 1, st1_ref, nkc - 2, st0_ref, cmax0)
    for sl, cm in zip(strips, cmax1):
        accumulate_strip(nkc - 1, st1_ref, sl, cm)

    lq = lq_ref[...]
    lam = (jnp.exp(jnp.sum(lq[0:1] * lq[1:2], axis=1, keepdims=True))
           - jnp.exp(jnp.sum(lq[2:3] * lq[3:4], axis=1, keepdims=True)) + lam_init)
    accn = acc_ref[0:A_V_DIM, :] / acc_ref[A_V_DIM:A_V_DIM + 1, :]
    ot = accn[:, 0:tq] - lam * accn[:, tq:2 * tq]
    otn = ot * lax.rsqrt(jnp.mean(ot * ot, axis=0, keepdims=True) + EPS)
    o_ref[...] = (otn.T * g_ref[...]) * (1.0 - lam_init)


def _attn_call(aq, ak, avt, lq, ag, B, S, tq, tk, tv, lam_init):
    nkc = S // tk
    nq = S // tq
    assert nkc >= 2 and nkc % 2 == 0, "key chunks are processed in pairs"
    assert tk % tv == 0
    return pl.pallas_call(
        functools.partial(_attn_body, nkc, tq, tk, tv, lam_init),
        grid=(B, A_HEADS, nq),
        in_specs=[pl.BlockSpec((1, 1, tq, LANES), lambda b, h, i: (b, h, i, 0)),
                  pl.BlockSpec((1, 1, S, LANES), lambda b, h, i: (b, h, 0, 0)),
                  pl.BlockSpec((1, S // tv, 1, A_V_ROWS, tv), lambda b, h, i: (b, 0, h, 0, 0)),
                  pl.BlockSpec(lq.shape, lambda b, h, i: (0, 0)),
                  pl.BlockSpec(ag.shape, lambda b, h, i: (0, 0))],
        out_specs=pl.BlockSpec((tq, A_V_DIM), lambda b, h, i: (b * nq + i, h)),
        out_shape=jax.ShapeDtypeStruct((B * S, A_WIDTH), F32),
        scratch_shapes=[pltpu.VMEM((2 * tq, LANES), BF16),
                        pltpu.VMEM((tk, 2 * tq), F32),
                        pltpu.VMEM((tk, 2 * tq), F32),
                        pltpu.VMEM((1, 2 * tq), F32),
                        pltpu.VMEM((A_V_ROWS, 2 * tq), F32)],
        compiler_params=pltpu.CompilerParams(dimension_semantics=("arbitrary", "arbitrary", "arbitrary"),
                                             vmem_limit_bytes=VMEM_LIMIT),
        name="attn",
    )(aq, ak, avt, lq, ag)


def _epilogue_body(x_ref, hf_ref, hb_ref, o_ref, preg_ref, w4_ref, mng_ref, bm_ref, wbm_ref, wba_ref,
                   wout_ref, postg_ref, out_ref):
    x = x_ref[...]
    xn = (_rms(x) * preg_ref[...]).astype(BF16)

    def proj(lo, hi):
        return jnp.dot(xn, w4_ref[:, lo:hi], preferred_element_type=F32)

    h = hf_ref[...] + hb_ref[...]
    mng = mng_ref[...]
    hn = jnp.concatenate(
        [_rms(h[:, k * M_V_DIM:(k + 1) * M_V_DIM]) * mng[:, k * M_V_DIM:(k + 1) * M_V_DIM]
         for k in range(M_HEADS)], axis=1)
    h_a = jax.nn.sigmoid(proj(0, M_WIDTH)) * hn * _silu(proj(M_WIDTH, 2 * M_WIDTH))
    ya = jnp.dot(h_a.astype(BF16), wbm_ref[...], preferred_element_type=F32)
    h_b = o_ref[...] * _silu(proj(2 * M_WIDTH, 2 * M_WIDTH + A_WIDTH))
    yb = jnp.dot(h_b.astype(BF16), wba_ref[...], preferred_element_type=F32)
    g0 = 2 * M_WIDTH + A_WIDTH
    bm = bm_ref[...]
    g_a = jax.nn.sigmoid(proj(g0, g0 + D_MODEL) + bm[:, 0:D_MODEL])
    g_b = jax.nn.sigmoid(proj(g0 + D_MODEL, g0 + 2 * D_MODEL) + bm[:, D_MODEL:2 * D_MODEL])
    y = g_a * ya + g_b * yb
    out = jnp.dot(y.astype(BF16), wout_ref[...], preferred_element_type=F32)
    out_ref[...] = x + _rms(out) * postg_ref[...]


def _epilogue_call(x2d, hf, hb, o, preg, w4, mng, bm, wbm, wba, wout, postg, te):
    T = x2d.shape[0]
    row = lambda i: (i, 0)
    const = lambda i: (0, 0)
    single = pl.Buffered(1)
    return pl.pallas_call(
        _epilogue_body,
        grid=(T // te,),
        in_specs=[pl.BlockSpec((te, D_MODEL), row),
                  pl.BlockSpec((te, M_WIDTH), row),
                  pl.BlockSpec((te, M_WIDTH), row),
                  pl.BlockSpec((te, A_WIDTH), row),
                  pl.BlockSpec(preg.shape, const),
                  pl.BlockSpec(w4.shape, const, pipeline_mode=single),
                  pl.BlockSpec(mng.shape, const),
                  pl.BlockSpec(bm.shape, const),
                  pl.BlockSpec(wbm.shape, const, pipeline_mode=single),
                  pl.BlockSpec(wba.shape, const, pipeline_mode=single),
                  pl.BlockSpec(wout.shape, const, pipeline_mode=single),
                  pl.BlockSpec(postg.shape, const)],
        out_specs=pl.BlockSpec((te, D_MODEL), row),
        out_shape=jax.ShapeDtypeStruct((T, D_MODEL), F32),
        compiler_params=pltpu.CompilerParams(dimension_semantics=("arbitrary",),
                                             vmem_limit_bytes=VMEM_LIMIT),
        name="epilogue",
    )(x2d, hf, hb, o, preg, w4, mng, bm, wbm, wba, wout, postg)


def _rope_tables(S):
    half = ROPE_DIMS // 2
    inv = ROPE_THETA ** (-jnp.arange(0, ROPE_DIMS, 2, dtype=F32) / ROPE_DIMS)
    ang = jnp.arange(S, dtype=F32)[:, None] * inv[None, :]
    cos, sin = jnp.cos(ang), jnp.sin(ang)
    one = jnp.ones((S, A_HEAD_DIM - ROPE_DIMS), F32)
    zero = jnp.zeros((S, A_HEAD_DIM - ROPE_DIMS), F32)
    zh = jnp.zeros((S, half), F32)
    rc = jnp.concatenate([cos, cos, one], axis=1)
    rs1 = jnp.concatenate([-sin, zh, zero], axis=1)
    rs2 = jnp.concatenate([zh, sin, zero], axis=1)
    rep = LANES // A_HEAD_DIM
    return jnp.tile(rc, (1, rep)), jnp.tile(rs1, (1, rep)), jnp.tile(rs2, (1, rep))


def _gate_layout(m):
    def groups(fw, bw):
        g = jnp.stack([fw, bw], axis=-1)
        g = jnp.concatenate([g, jnp.zeros(g.shape[:-1] + (6,), g.dtype)], axis=-1)
        return g.reshape(g.shape[:-2] + (8 * M_HEADS,))
    return jnp.concatenate([groups(m[..., 1, :], m[..., 3, :]), groups(m[..., 0, :], m[..., 2, :])],
                           axis=-1)


def _layer(x, layer_idx, pre_g, w_in, conv_w, conv_b, gate_b, m_norm_g, lambda_qk, a_norm_g,
           w_branch_m, w_branch_a, b_merge, w_out, post_g):
    B, S, _ = x.shape
    T = B * S
    tm = min(512, S)
    tq = min(1024, S)
    tk = min(1024, S // 2)
    tr = min(512, S)
    te = min(256, T)
    lam_init = 0.8 - 0.6 * math.exp(-0.3 * layer_idx)

    cuts = [0]
    for sz in SPLIT_SIZES:
        cuts.append(cuts[-1] + sz)
    col = lambda k: w_in[:, cuts[k]:cuts[k + 1]]
    wmqk = col(0).astype(BF16)
    w1 = jnp.concatenate([col(1), col(5), col(6)], axis=1).astype(BF16)
    wavt = col(7).T.astype(BF16)
    wgt = _gate_layout(col(2).reshape(D_MODEL, 4, M_HEADS)).T.astype(BF16)
    gb = _gate_layout(gate_b).reshape(N_GATE_ROWS, 1).astype(F32)
    w4 = jnp.concatenate([col(3), col(4), col(8), col(9)], axis=1).astype(BF16)
    rc, rs1, rs2 = _rope_tables(S)

    x2d = x.reshape(T, D_MODEL)
    preg = pre_g.reshape(1, D_MODEL)
    mq, mk, mv, gt, aq, ak, avt = _inproj_call(
        x2d, preg, wmqk, w1, wavt, wgt, conv_w, conv_b.reshape(1, 2 * M_QK), rc, rs1, rs2, B, S, tm)
    rows, cols = _gates_call(gt, gb, B, S)
    hf = _mlstm_call(False, mq, mk, mv, rows, cols, B, S, tr)
    hb = _mlstm_call(True, mq, mk, mv, rows, cols, B, S, tr)
    o = _attn_call(aq, ak, avt, lambda_qk.astype(F32), a_norm_g.reshape(1, A_V_DIM), B, S, tq, tk, tm,
                   lam_init)
    out = _epilogue_call(
        x2d, hf.reshape(T, M_WIDTH), hb.reshape(T, M_WIDTH), o, preg, w4,
        m_norm_g.reshape(1, M_WIDTH), b_merge.reshape(1, 2 * D_MODEL),
        w_branch_m.astype(BF16), w_branch_a.astype(BF16), w_out.astype(BF16),
        post_g.reshape(1, D_MODEL), te)
    return out.reshape(B, S, D_MODEL)


def kernel(x, pre_norm_g, w_in, conv_w, conv_b, gate_b, m_norm_g, lambda_qk, a_norm_g, w_branch_m,
           w_branch_a, b_merge, w_out, post_norm_g):
    for l in range(pre_norm_g.shape[0]):
        x = _layer(x, l, pre_norm_g[l], w_in[l], conv_w[l], conv_b[l], gate_b[l], m_norm_g[l],
                   lambda_qk[l], a_norm_g[l], w_branch_m[l], w_branch_a[l], b_merge[l], w_out[l],
                   post_norm_g[l])
    return x
```

```python
import functools
import math

import jax
import jax.numpy as jnp
from jax import lax
from jax.experimental import pallas as pl
from jax.experimental.pallas import tpu as pltpu

D_MODEL = 1024
M_HEADS = 4
M_QK_DIM = 128
M_V_DIM = 256
M_QK = M_HEADS * M_QK_DIM
M_WIDTH = M_HEADS * M_V_DIM
M_CHUNK = 64
CONV_WIDTH = 4
A_HEADS = 8
A_HEAD_DIM = 64
A_V_DIM = 2 * A_HEAD_DIM
A_QK = A_HEADS * 2 * A_HEAD_DIM
A_WIDTH = A_HEADS * A_V_DIM
ROPE_THETA = 500000.0
ROPE_DIMS = A_HEAD_DIM // 4
EPS = 1e-6
SPLIT_SIZES = (2 * M_QK, M_WIDTH, 4 * M_HEADS, M_WIDTH, M_WIDTH, A_QK, A_QK, A_WIDTH, A_WIDTH, 2 * D_MODEL)

LANES = 128
MXU_N = 256
BF16_ROWS = 16
A_V_ROWS = A_V_DIM + BF16_ROWS
HALO = 16
N_GATE_ROWS = 2 * 8 * M_HEADS
VMEM_LIMIT = 56 * 1024 * 1024

F32 = jnp.float32
BF16 = jnp.bfloat16
_NT = (((1,), (1,)), ((), ()))
_TN = (((0,), (0,)), ((), ()))


def _rms(xf):
    return xf * lax.rsqrt(jnp.mean(xf * xf, axis=-1, keepdims=True) + EPS)


def _silu(y):
    return y * jax.nn.sigmoid(y)


def _inproj_body(tps, tm, xprev_ref, x_ref, xnext_ref, preg_ref, wmqk_ref, w1_ref, wavt_ref, wgt_ref,
                 convw_ref, convb_ref, rc_ref, rs1_ref, rs2_ref,
                 mq_ref, mk_ref, mv_ref, gt_ref, aq_ref, ak_ref, avt_ref, xext_ref):
    t = lax.rem(pl.program_id(0), tps)
    g = preg_ref[...]
    xn = (_rms(x_ref[...]) * g).astype(BF16)
    xp = jnp.where(t != 0, _rms(xprev_ref[...]) * g, 0.0).astype(BF16)
    xq = jnp.where(t != tps - 1, _rms(xnext_ref[...]) * g, 0.0).astype(BF16)
    xext_ref[0:HALO, :] = xp
    xext_ref[HALO:HALO + tm, :] = xn
    xext_ref[HALO + tm:, :] = xq
    n_ext = tm + 2 * HALO
    ext = jnp.dot(xext_ref[...], wmqk_ref[...], preferred_element_type=F32)
    mv = jnp.dot(xn, w1_ref[:, 0:M_WIDTH], preferred_element_type=F32)
    aq = jnp.dot(xn, w1_ref[:, M_WIDTH:M_WIDTH + A_QK], preferred_element_type=F32)
    ak = jnp.dot(xn, w1_ref[:, M_WIDTH + A_QK:M_WIDTH + 2 * A_QK], preferred_element_type=F32)
    avt = lax.dot_general(wavt_ref[...], xn, _NT, preferred_element_type=F32)
    gt_ref[...] = lax.dot_general(wgt_ref[...], xn, _NT, preferred_element_type=F32)

    left = CONV_WIDTH // 2
    y = None
    for j in range(CONV_WIDTH):
        sh = (left - j) % n_ext
        r = ext if sh == 0 else pltpu.roll(ext, sh, axis=0)
        term = convw_ref[j:j + 1, :] * r[HALO:HALO + tm, :]
        y = term if y is None else y + term
    y = _silu(y + convb_ref[...])
    qscale = M_QK_DIM ** -0.5
    for h in range(M_HEADS):
        mq_ref[0, h] = (y[:, h * M_QK_DIM:(h + 1) * M_QK_DIM] * qscale).astype(BF16)
        mk_ref[0, h] = y[:, M_QK + h * M_QK_DIM:M_QK + (h + 1) * M_QK_DIM].astype(BF16)
    for h in range(M_HEADS):
        mv_ref[0, h] = mv[:, h * M_V_DIM:(h + 1) * M_V_DIM].astype(BF16)
    rc, rs1, rs2 = rc_ref[...], rs1_ref[...], rs2_ref[...]
    half = ROPE_DIMS // 2

    def rope(a):
        return a * rc + pltpu.roll(a, LANES - half, axis=1) * rs1 + pltpu.roll(a, half, axis=1) * rs2

    ascale = A_HEAD_DIM ** -0.5 * math.log2(math.e)
    for h in range(A_HEADS):
        aq_ref[0, h] = (rope(aq[:, h * LANES:(h + 1) * LANES]) * ascale).astype(BF16)
        ak_ref[0, h] = rope(ak[:, h * LANES:(h + 1) * LANES]).astype(BF16)
    ones_tile = (lax.broadcasted_iota(jnp.int32, (BF16_ROWS, tm), 0) == 0).astype(BF16)
    for h in range(A_HEADS):
        avt_ref[0, 0, h, 0:A_V_DIM, :] = avt[h * A_V_DIM:(h + 1) * A_V_DIM, :].astype(BF16)
        avt_ref[0, 0, h, A_V_DIM:A_V_ROWS, :] = ones_tile


def _inproj_call(x2d, preg, wmqk, w1, wavt, wgt, convw, convb, rc, rs1, rs2, B, S, tm):
    T = B * S
    tps = S // tm
    nh = tm // HALO
    const = lambda i: (0, 0)
    in_specs = [
        pl.BlockSpec((HALO, D_MODEL), lambda i: (jnp.maximum(i * nh - 1, 0), 0)),
        pl.BlockSpec((tm, D_MODEL), lambda i: (i, 0)),
        pl.BlockSpec((HALO, D_MODEL), lambda i: (jnp.minimum((i + 1) * nh, T // HALO - 1), 0)),
        pl.BlockSpec((1, D_MODEL), const),
        pl.BlockSpec(wmqk.shape, const),
        pl.BlockSpec(w1.shape, const),
        pl.BlockSpec(wavt.shape, const),
        pl.BlockSpec(wgt.shape, const),
        pl.BlockSpec(convw.shape, const),
        pl.BlockSpec(convb.shape, const),
        pl.BlockSpec((tm, LANES), lambda i: (lax.rem(i, tps), 0)),
        pl.BlockSpec((tm, LANES), lambda i: (lax.rem(i, tps), 0)),
        pl.BlockSpec((tm, LANES), lambda i: (lax.rem(i, tps), 0)),
    ]
    bt = lambda i: (i // tps, 0, lax.rem(i, tps), 0)
    out_shape = [
        jax.ShapeDtypeStruct((B, M_HEADS, S, M_QK_DIM), BF16),
        jax.ShapeDtypeStruct((B, M_HEADS, S, M_QK_DIM), BF16),
        jax.ShapeDtypeStruct((B, M_HEADS, S, M_V_DIM), BF16),
        jax.ShapeDtypeStruct((N_GATE_ROWS, T), F32),
        jax.ShapeDtypeStruct((B, A_HEADS, S, LANES), BF16),
        jax.ShapeDtypeStruct((B, A_HEADS, S, LANES), BF16),
        jax.ShapeDtypeStruct((B, tps, A_HEADS, A_V_ROWS, tm), BF16),
    ]
    out_specs = [
        pl.BlockSpec((1, M_HEADS, tm, M_QK_DIM), bt),
        pl.BlockSpec((1, M_HEADS, tm, M_QK_DIM), bt),
        pl.BlockSpec((1, M_HEADS, tm, M_V_DIM), bt),
        pl.BlockSpec((N_GATE_ROWS, tm), lambda i: (0, i)),
        pl.BlockSpec((1, A_HEADS, tm, LANES), bt),
        pl.BlockSpec((1, A_HEADS, tm, LANES), bt),
        pl.BlockSpec((1, 1, A_HEADS, A_V_ROWS, tm), lambda i: (i // tps, lax.rem(i, tps), 0, 0, 0)),
    ]
    return pl.pallas_call(
        functools.partial(_inproj_body, tps, tm),
        grid=(T // tm,),
        in_specs=in_specs,
        out_specs=out_specs,
        out_shape=out_shape,
        scratch_shapes=[pltpu.VMEM((tm + 2 * HALO, D_MODEL), BF16)],
        compiler_params=pltpu.CompilerParams(dimension_semantics=("arbitrary",),
                                             vmem_limit_bytes=VMEM_LIMIT),
        name="inproj",
    )(x2d, x2d, x2d, preg, wmqk, w1, wavt, wgt, convw, convb, rc, rs1, rs2)


N_GATE_COLS = 5


def _gates_body(S, ct, f_ref, i_ref, fb_ref, ib_ref, rows_ref, cols_ref):
    L = M_CHUNK
    f = f_ref[...] + fb_ref[...]
    ig = i_ref[...] + ib_ref[...]
    lf = jnp.minimum(f, 0.0) - jnp.log1p(jnp.exp(-jnp.abs(f)))
    lane = lax.broadcasted_iota(jnp.int32, f.shape, 1)
    pos = lane & (L - 1)
    fwd = lax.broadcasted_iota(jnp.int32, f.shape, 0) == 0

    def from_left(x, s):
        return pltpu.roll(x, s, axis=1)

    def from_right(x, s):
        return pltpu.roll(x, S - s, axis=1)

    def chunk_scans(x, op, ident):
        pre, suf = x, x
        s = 1
        while s < L:
            pre = op(pre, jnp.where(pos >= s, from_left(pre, s), ident))
            suf = op(suf, jnp.where(pos < L - s, from_right(suf, s), ident))
            s *= 2
        return pre, suf

    pre, suf = chunk_scans(lf, jnp.add, 0.0)
    a = jnp.where(fwd, pre, suf)
    a_tot = pre + suf - lf
    w = ig - a
    wpre, wsuf = chunk_scans(w, jnp.maximum, -jnp.inf)
    rowmax = a + jnp.where(fwd, wpre, wsuf)
    g_max = a_tot + jnp.maximum(wpre, wsuf)

    af, gf, ab, gb = a_tot, g_max, a_tot, g_max
    s = L
    while s < S:
        ok = lane >= s
        a_sh, g_sh = from_left(af, s), from_left(gf, s)
        gf = jnp.where(ok, jnp.maximum(g_sh + af, gf), gf)
        af = jnp.where(ok, af + a_sh, af)
        ok = lane < S - s
        a_sh, g_sh = from_right(ab, s), from_right(gb, s)
        gb = jnp.where(ok, jnp.maximum(g_sh + ab, gb), gb)
        ab = jnp.where(ok, ab + a_sh, ab)
        s *= 2
    m_f = jnp.maximum(af, gf)
    m_b = jnp.maximum(ab, gb)
    m_prev = jnp.where(fwd, jnp.where(lane >= L, from_left(m_f, L), 0.0),
                       jnp.where(lane < S - L, from_right(m_b, L), 0.0))

    inter = a + m_prev
    mj = jnp.maximum(inter, rowmax)
    m_new = jnp.maximum(a_tot + m_prev, g_max)
    cols = (a - mj, jnp.exp(inter - mj), jnp.exp(-mj), jnp.exp(a_tot + w - m_new),
            jnp.exp(a_tot + m_prev - m_new))
    rows_ref[0, 0] = w
    pad = jnp.zeros((LANES - 8 * N_GATE_COLS, ct), F32)
    for t in range(S // ct):
        sl = slice(t * ct, (t + 1) * ct)
        cols_ref[0, 0, sl, :] = jnp.concatenate([c[:, sl] for c in cols] + [pad], axis=0).T


def _gates_call(gt, gb, B, S):
    ct = min(2048, S)
    return pl.pallas_call(
        functools.partial(_gates_body, S, ct),
        grid=(B, M_HEADS),
        in_specs=[pl.BlockSpec((8, S), lambda b, h: (h, b)),
                  pl.BlockSpec((8, S), lambda b, h: (M_HEADS + h, b)),
                  pl.BlockSpec((8, 1), lambda b, h: (h, 0)),
                  pl.BlockSpec((8, 1), lambda b, h: (M_HEADS + h, 0))],
        out_specs=[pl.BlockSpec((1, 1, 8, S), lambda b, h: (b, h, 0, 0)),
                   pl.BlockSpec((1, 1, S, LANES), lambda b, h: (b, h, 0, 0))],
        out_shape=[jax.ShapeDtypeStruct((B, M_HEADS, 8, S), F32),
                   jax.ShapeDtypeStruct((B, M_HEADS, S, LANES), F32)],
        compiler_params=pltpu.CompilerParams(dimension_semantics=("arbitrary", "arbitrary"),
                                             vmem_limit_bytes=VMEM_LIMIT),
        name="gates",
    )(gt, gt, gb, gb)


def _mlstm_body(rev, nchunk, q_ref, k_ref, v_ref, rows_ref, cols_ref, h_ref, c_ref):
    @pl.when(pl.program_id(2) == 0)
    def _():
        c_ref[...] = jnp.zeros_like(c_ref)

    L = M_CHUNK
    d = 1 if rev else 0
    row_i = lax.broadcasted_iota(jnp.int32, (L, L), 0)
    col_i = lax.broadcasted_iota(jnp.int32, (L, L), 1)
    visible = (col_i >= row_i) if rev else (col_i <= row_i)
    ones_col = (lax.broadcasted_iota(jnp.int32, (L, LANES), 1) == 0).astype(BF16)
    order = list(range(nchunk - 1, -1, -1) if rev else range(nchunk))

    def rows_of(c):
        return slice(c * L, (c + 1) * L)

    def gate_col(c, k):
        return cols_ref[0, 0, rows_of(c), 8 * k + d:8 * k + d + 1]

    def v_ext(c):
        return jnp.concatenate([v_ref[0, 0, rows_of(c), :], ones_col], axis=1)

    qk, kv = {}, {}
    for c in order:
        qk[c] = lax.dot_general(q_ref[0, 0, rows_of(c), :], k_ref[0, 0, rows_of(c), :], _NT,
                                preferred_element_type=F32)
        wv = (gate_col(c, 3) * v_ext(c).astype(F32)).astype(BF16)
        kv[c] = lax.dot_general(k_ref[0, 0, rows_of(c), :], wv, _TN, preferred_element_type=F32)
    C = c_ref[...]
    c_in = {}
    for c in order:
        c_in[c] = C.astype(BF16)
        C = gate_col(c, 4)[0:1, :] * C + kv[c]
    c_ref[...] = C
    for c in order:
        u, w_inter, e = gate_col(c, 0), gate_col(c, 1), gate_col(c, 2)
        w_row = rows_ref[0, 0, d:d + 1, rows_of(c)]
        p = jnp.exp(jnp.where(visible, u + w_row, -jnp.inf))
        s = qk[c] * p
        num = (w_inter * jnp.dot(q_ref[0, 0, rows_of(c), :], c_in[c], preferred_element_type=F32)
               + jnp.dot(s.astype(BF16), v_ext(c), preferred_element_type=F32))
        den = num[:, M_V_DIM:M_V_DIM + 1]
        h_ref[0, rows_of(c), :] = num[:, 0:M_V_DIM] / jnp.maximum(jnp.abs(den), e)


def _mlstm_call(rev, mq, mk, mv, rows, cols, B, S, tr):
    ns = S // tr
    pos = (lambda s: ns - 1 - s) if rev else (lambda s: s)
    return pl.pallas_call(
        functools.partial(_mlstm_body, rev, tr // M_CHUNK),
        grid=(B, M_HEADS, ns),
        in_specs=[pl.BlockSpec((1, 1, tr, M_QK_DIM), lambda b, h, s: (b, h, pos(s), 0)),
                  pl.BlockSpec((1, 1, tr, M_QK_DIM), lambda b, h, s: (b, h, pos(s), 0)),
                  pl.BlockSpec((1, 1, tr, M_V_DIM), lambda b, h, s: (b, h, pos(s), 0)),
                  pl.BlockSpec((1, 1, 8, tr), lambda b, h, s: (b, h, 0, pos(s))),
                  pl.BlockSpec((1, 1, tr, LANES), lambda b, h, s: (b, h, pos(s), 0))],
        out_specs=pl.BlockSpec((1, tr, M_V_DIM), lambda b, h, s: (b, pos(s), h)),
        out_shape=jax.ShapeDtypeStruct((B, S, M_WIDTH), F32),
        scratch_shapes=[pltpu.VMEM((M_QK_DIM, M_V_DIM + LANES), F32)],
        compiler_params=pltpu.CompilerParams(dimension_semantics=("arbitrary", "arbitrary", "arbitrary"),
                                             vmem_limit_bytes=VMEM_LIMIT),
        name="mlstm_bwd" if rev else "mlstm_fwd",
    )(mq, mk, mv, rows, cols)


def _attn_body(nkc, tq, tk, tv, lam_init, q_ref, k_ref, vt_ref, lq_ref, g_ref, o_ref,
               qz_ref, st0_ref, st1_ref, m_ref, acc_ref):
    q = q_ref[0, 0]
    lane = lax.broadcasted_iota(jnp.int32, q.shape, 1)
    zero = jnp.zeros_like(q)
    qz_ref[0:tq, :] = jnp.where(lane < A_HEAD_DIM, q, zero)
    qz_ref[tq:2 * tq, :] = jnp.where(lane >= A_HEAD_DIM, q, zero)
    m_ref[...] = jnp.full(m_ref.shape, -jnp.inf, F32)
    acc_ref[...] = jnp.zeros_like(acc_ref)

    strips = [slice(c * MXU_N, (c + 1) * MXU_N) for c in range(2 * tq // MXU_N)]
    nv = tk // tv

    def score_strip(j, st_ref, sl):
        kj = k_ref[0, 0, pl.ds(pl.multiple_of(j * tk, tk), tk), :]
        st = lax.dot_general(kj, qz_ref[sl, :], _NT, preferred_element_type=F32)
        st_ref[:, sl] = st
        return jnp.max(st, axis=0, keepdims=True)

    def accumulate_strip(j, st_ref, sl, cm):
        m_old = m_ref[:, sl]
        m_new = jnp.maximum(m_old, cm)
        alpha = jnp.exp2(m_old - m_new)
        p = jnp.exp2(st_ref[:, sl] - m_new).astype(BF16)
        pv = jnp.dot(vt_ref[0, j * nv, 0], p[0:tv], preferred_element_type=F32)
        for u in range(1, nv):
            pv = pv + jnp.dot(vt_ref[0, j * nv + u, 0], p[u * tv:(u + 1) * tv],
                              preferred_element_type=F32)
        acc_ref[:, sl] = alpha * acc_ref[:, sl] + pv
        m_ref[:, sl] = m_new

    def step(j_next, st_next_ref, j_cur, st_cur_ref, cmax_cur):
        cmax_next = []
        for sl, cm in zip(strips, cmax_cur):
            cmax_next.append(score_strip(j_next, st_next_ref, sl))
            accumulate_strip(j_cur, st_cur_ref, sl, cm)
        return tuple(cmax_next)

    def pair(i, cmax0):
        j = 2 * i
        cmax1 = step(j + 1, st1_ref, j, st0_ref, cmax0)
        return step(j + 2, st0_ref, j + 1, st1_ref, cmax1)

    cmax0 = tuple(score_strip(0, st0_ref, sl) for sl in strips)
    cmax0 = lax.fori_loop(0, nkc // 2 - 1, pair, cmax0)
    cmax1 = step(nkc - 1, st1_ref, nkc - 2, st0_ref, cmax0)
    for sl, cm in zip(strips, cmax1):
        accumulate_strip(nkc - 1, st1_ref, sl, cm)

    lq = lq_ref[...]
    lam = (jnp.exp(jnp.sum(lq[0:1] * lq[1:2], axis=1, keepdims=True))
           - jnp.exp(jnp.sum(lq[2:3] * lq[3:4], axis=1, keepdims=True)) + lam_init)
    accn = acc_ref[0:A_V_DIM, :] / acc_ref[A_V_DIM:A_V_DIM + 1, :]
    ot = accn[:, 0:tq] - lam * accn[:, tq:2 * tq]
    otn = ot * lax.rsqrt(jnp.mean(ot * ot, axis=0, keepdims=True) + EPS)
    o_ref[...] = (otn.T * g_ref[...]) * (1.0 - lam_init)


def _attn_call(aq, ak, avt, lq, ag, B, S, tq, tk, tv, lam_init):
    nkc = S // tk
    nq = S // tq
    assert nkc >= 2 and nkc % 2 == 0, "key chunks are processed in pairs"
    assert tk % tv == 0
    return pl.pallas_call(
        functools.partial(_attn_body, nkc, tq, tk, tv, lam_init),
        grid=(B, A_HEADS, nq),
        in_specs=[pl.BlockSpec((1, 1, tq, LANES), lambda b, h, i: (b, h, i, 0)),
                  pl.BlockSpec((1, 1, S, LANES), lambda b, h, i: (b, h, 0, 0)),
                  pl.BlockSpec((1, S // tv, 1, A_V_ROWS, tv), lambda b, h, i: (b, 0, h, 0, 0)),
                  pl.BlockSpec(lq.shape, lambda b, h, i: (0, 0)),
                  pl.BlockSpec(ag.shape, lambda b, h, i: (0, 0))],
        out_specs=pl.BlockSpec((tq, A_V_DIM), lambda b, h, i: (b * nq + i, h)),
        out_shape=jax.ShapeDtypeStruct((B * S, A_WIDTH), F32),
        scratch_shapes=[pltpu.VMEM((2 * tq, LANES), BF16),
                        pltpu.VMEM((tk, 2 * tq), F32),
                        pltpu.VMEM((tk, 2 * tq), F32),
                        pltpu.VMEM((1, 2 * tq), F32),
                        pltpu.VMEM((A_V_ROWS, 2 * tq), F32)],
        compiler_params=pltpu.CompilerParams(dimension_semantics=("arbitrary", "arbitrary", "arbitrary"),
                                             vmem_limit_bytes=VMEM_LIMIT),
        name="attn",
    )(aq, ak, avt, lq, ag)


def _epilogue_body(x_ref, hf_ref, hb_ref, o_ref, preg_ref, w4_ref, mng_ref, bm_ref, wbm_ref, wba_ref,
                   wout_ref, postg_ref, out_ref):
    x = x_ref[...]
    xn = (_rms(x) * preg_ref[...]).astype(BF16)

    def proj(lo, hi):
        return jnp.dot(xn, w4_ref[:, lo:hi], preferred_element_type=F32)

    h = hf_ref[...] + hb_ref[...]
    mng = mng_ref[...]
    hn = jnp.concatenate(
        [_rms(h[:, k * M_V_DIM:(k + 1) * M_V_DIM]) * mng[:, k * M_V_DIM:(k + 1) * M_V_DIM]
         for k in range(M_HEADS)], axis=1)
    h_a = jax.nn.sigmoid(proj(0, M_WIDTH)) * hn * _silu(proj(M_WIDTH, 2 * M_WIDTH))
    ya = jnp.dot(h_a.astype(BF16), wbm_ref[...], preferred_element_type=F32)
    h_b = o_ref[...] * _silu(proj(2 * M_WIDTH, 2 * M_WIDTH + A_WIDTH))
    yb = jnp.dot(h_b.astype(BF16), wba_ref[...], preferred_element_type=F32)
    g0 = 2 * M_WIDTH + A_WIDTH
    bm = bm_ref[...]
    g_a = jax.nn.sigmoid(proj(g0, g0 + D_MODEL) + bm[:, 0:D_MODEL])
    g_b = jax.nn.sigmoid(proj(g0 + D_MODEL, g0 + 2 * D_MODEL) + bm[:, D_MODEL:2 * D_MODEL])
    y = g_a * ya + g_b * yb
    out = jnp.dot(y.astype(BF16), wout_ref[...], preferred_element_type=F32)
    out_ref[...] = x + _rms(out) * postg_ref[...]


def _epilogue_call(x2d, hf, hb, o, preg, w4, mng, bm, wbm, wba, wout, postg, te):
    T = x2d.shape[0]
    row = lambda i: (i, 0)
    const = lambda i: (0, 0)
    single = pl.Buffered(1)
    return pl.pallas_call(
        _epilogue_body,
        grid=(T // te,),
        in_specs=[pl.BlockSpec((te, D_MODEL), row),
                  pl.BlockSpec((te, M_WIDTH), row),
                  pl.BlockSpec((te, M_WIDTH), row),
                  pl.BlockSpec((te, A_WIDTH), row),
                  pl.BlockSpec(preg.shape, const),
                  pl.BlockSpec(w4.shape, const, pipeline_mode=single),
                  pl.BlockSpec(mng.shape, const),
                  pl.BlockSpec(bm.shape, const),
                  pl.BlockSpec(wbm.shape, const, pipeline_mode=single),
                  pl.BlockSpec(wba.shape, const, pipeline_mode=single),
                  pl.BlockSpec(wout.shape, const, pipeline_mode=single),
                  pl.BlockSpec(postg.shape, const)],
        out_specs=pl.BlockSpec((te, D_MODEL), row),
        out_shape=jax.ShapeDtypeStruct((T, D_MODEL), F32),
        compiler_params=pltpu.CompilerParams(dimension_semantics=("arbitrary",),
                                             vmem_limit_bytes=VMEM_LIMIT),
        name="epilogue",
    )(x2d, hf, hb, o, preg, w4, mng, bm, wbm, wba, wout, postg)


def _rope_tables(S):
    half = ROPE_DIMS // 2
    inv = ROPE_THETA ** (-jnp.arange(0, ROPE_DIMS, 2, dtype=F32) / ROPE_DIMS)
    ang = jnp.arange(S, dtype=F32)[:, None] * inv[None, :]
    cos, sin = jnp.cos(ang), jnp.sin(ang)
    one = jnp.ones((S, A_HEAD_DIM - ROPE_DIMS), F32)
    zero = jnp.zeros((S, A_HEAD_DIM - ROPE_DIMS), F32)
    zh = jnp.zeros((S, half), F32)
    rc = jnp.concatenate([cos, cos, one], axis=1)
    rs1 = jnp.concatenate([-sin, zh, zero], axis=1)
    rs2 = jnp.concatenate([zh, sin, zero], axis=1)
    rep = LANES // A_HEAD_DIM
    return jnp.tile(rc, (1, rep)), jnp.tile(rs1, (1, rep)), jnp.tile(rs2, (1, rep))


def _gate_layout(m):
    def groups(fw, bw):
        g = jnp.stack([fw, bw], axis=-1)
        g = jnp.concatenate([g, jnp.zeros(g.shape[:-1] + (6,), g.dtype)], axis=-1)
        return g.reshape(g.shape[:-2] + (8 * M_HEADS,))
    return jnp.concatenate([groups(m[..., 1, :], m[..., 3, :]), groups(m[..., 0, :], m[..., 2, :])],
                           axis=-1)


def _layer(x, layer_idx, pre_g, w_in, conv_w, conv_b, gate_b, m_norm_g, lambda_qk, a_norm_g,
           w_branch_m, w_branch_a, b_merge, w_out, post_g):
    B, S, _ = x.shape
    T = B * S
    tm = min(512, S)
    tq = min(1024, S)
    tk = min(1024, S // 2)
    tr = min(1024, S)
    te = min(256, T)
    lam_init = 0.8 - 0.6 * math.exp(-0.3 * layer_idx)

    cuts = [0]
    for sz in SPLIT_SIZES:
        cuts.append(cuts[-1] + sz)
    col = lambda k: w_in[:, cuts[k]:cuts[k + 1]]
    wmqk = col(0).astype(BF16)
    w1 = jnp.concatenate([col(1), col(5), col(6)], axis=1).astype(BF16)
    wavt = col(7).T.astype(BF16)
    wgt = _gate_layout(col(2).reshape(D_MODEL, 4, M_HEADS)).T.astype(BF16)
    gb = _gate_layout(gate_b).reshape(N_GATE_ROWS, 1).astype(F32)
    w4 = jnp.concatenate([col(3), col(4), col(8), col(9)], axis=1).astype(BF16)
    rc, rs1, rs2 = _rope_tables(S)

    x2d = x.reshape(T, D_MODEL)
    preg = pre_g.reshape(1, D_MODEL)
    mq, mk, mv, gt, aq, ak, avt = _inproj_call(
        x2d, preg, wmqk, w1, wavt, wgt, conv_w, conv_b.reshape(1, 2 * M_QK), rc, rs1, rs2, B, S, tm)
    rows, cols = _gates_call(gt, gb, B, S)
    hf = _mlstm_call(False, mq, mk, mv, rows, cols, B, S, tr)
    hb = _mlstm_call(True, mq, mk, mv, rows, cols, B, S, tr)
    o = _attn_call(aq, ak, avt, lambda_qk.astype(F32), a_norm_g.reshape(1, A_V_DIM), B, S, tq, tk, tm,
                   lam_init)
    out = _epilogue_call(
        x2d, hf.reshape(T, M_WIDTH), hb.reshape(T, M_WIDTH), o, preg, w4,
        m_norm_g.reshape(1, M_WIDTH), b_merge.reshape(1, 2 * D_MODEL),
        w_branch_m.astype(BF16), w_branch_a.astype(BF16), w_out.astype(BF16),
        post_g.reshape(1, D_MODEL), te)
    return out.reshape(B, S, D_MODEL)


def kernel(x, pre_norm_g, w_in, conv_w, conv_b, gate_b, m_norm_g, lambda_qk, a_norm_g, w_branch_m,
           w_branch_a, b_merge, w_out, post_norm_g):
    for l in range(pre_norm_g.shape[0]):
        x = _layer(x, l, pre_norm_g[l], w_in[l], conv_w[l], conv_b[l], gate_b[l], m_norm_g[l],
                   lambda_qk[l], a_norm_g[l], w_branch_m[l], w_branch_a[l], b_merge[l], w_out[l],
                   post_norm_g[l])
    return x
```

```python
import functools
import math

import jax
import jax.numpy as jnp
from jax import lax
from jax.experimental import pallas as pl
from jax.experimental.pallas import tpu as pltpu

D_MODEL = 1024
M_HEADS = 4
M_QK_DIM = 128
M_V_DIM = 256
M_QK = M_HEADS * M_QK_DIM
M_WIDTH = M_HEADS * M_V_DIM
M_CHUNK = 64
CONV_WIDTH = 4
A_HEADS = 8
A_HEAD_DIM = 64
A_V_DIM = 2 * A_HEAD_DIM
A_QK = A_HEADS * 2 * A_HEAD_DIM
A_WIDTH = A_HEADS * A_V_DIM
ROPE_THETA = 500000.0
ROPE_DIMS = A_HEAD_DIM // 4
EPS = 1e-6
SPLIT_SIZES = (2 * M_QK, M_WIDTH, 4 * M_HEADS, M_WIDTH, M_WIDTH, A_QK, A_QK, A_WIDTH, A_WIDTH, 2 * D_MODEL)

LANES = 128
MXU_N = 256
BF16_ROWS = 16
A_V_ROWS = A_V_DIM + BF16_ROWS
HALO = 16
N_GATE_ROWS = 2 * 8 * M_HEADS
VMEM_LIMIT = 56 * 1024 * 1024

F32 = jnp.float32
BF16 = jnp.bfloat16
_NT = (((1,), (1,)), ((), ()))
_TN = (((0,), (0,)), ((), ()))


def _rms(xf):
    return xf * lax.rsqrt(jnp.mean(xf * xf, axis=-1, keepdims=True) + EPS)


def _silu(y):
    return y * jax.nn.sigmoid(y)


def _inproj_body(tps, tm, xprev_ref, x_ref, xnext_ref, preg_ref, wmqk_ref, w1_ref, wavt_ref, wgt_ref,
                 convw_ref, convb_ref, rc_ref, rs1_ref, rs2_ref,
                 mq_ref, mk_ref, mv_ref, gt_ref, aq_ref, ak_ref, avt_ref, xext_ref):
    t = lax.rem(pl.program_id(0), tps)
    g = preg_ref[...]
    xn = (_rms(x_ref[...]) * g).astype(BF16)
    xp = jnp.where(t != 0, _rms(xprev_ref[...]) * g, 0.0).astype(BF16)
    xq = jnp.where(t != tps - 1, _rms(xnext_ref[...]) * g, 0.0).astype(BF16)
    xext_ref[0:HALO, :] = xp
    xext_ref[HALO:HALO + tm, :] = xn
    xext_ref[HALO + tm:, :] = xq
    n_ext = tm + 2 * HALO
    ext = jnp.dot(xext_ref[...], wmqk_ref[...], preferred_element_type=F32)
    mv = jnp.dot(xn, w1_ref[:, 0:M_WIDTH], preferred_element_type=F32)
    aq = jnp.dot(xn, w1_ref[:, M_WIDTH:M_WIDTH + A_QK], preferred_element_type=F32)
    ak = jnp.dot(xn, w1_ref[:, M_WIDTH + A_QK:M_WIDTH + 2 * A_QK], preferred_element_type=F32)
    avt = lax.dot_general(wavt_ref[...], xn, _NT, preferred_element_type=F32)
    gt_ref[...] = lax.dot_general(wgt_ref[...], xn, _NT, preferred_element_type=F32)

    left = CONV_WIDTH // 2
    y = None
    for j in range(CONV_WIDTH):
        sh = (left - j) % n_ext
        r = ext if sh == 0 else pltpu.roll(ext, sh, axis=0)
        term = convw_ref[j:j + 1, :] * r[HALO:HALO + tm, :]
        y = term if y is None else y + term
    y = _silu(y + convb_ref[...])
    qscale = M_QK_DIM ** -0.5
    for h in range(M_HEADS):
        mq_ref[0, h] = (y[:, h * M_QK_DIM:(h + 1) * M_QK_DIM] * qscale).astype(BF16)
        mk_ref[0, h] = y[:, M_QK + h * M_QK_DIM:M_QK + (h + 1) * M_QK_DIM].astype(BF16)
    for h in range(M_HEADS):
        mv_ref[0, h] = mv[:, h * M_V_DIM:(h + 1) * M_V_DIM].astype(BF16)
    rc, rs1, rs2 = rc_ref[...], rs1_ref[...], rs2_ref[...]
    half = ROPE_DIMS // 2

    def rope(a):
        return a * rc + pltpu.roll(a, LANES - half, axis=1) * rs1 + pltpu.roll(a, half, axis=1) * rs2

    ascale = A_HEAD_DIM ** -0.5 * math.log2(math.e)
    for h in range(A_HEADS):
        aq_ref[0, h] = (rope(aq[:, h * LANES:(h + 1) * LANES]) * ascale).astype(BF16)
        ak_ref[0, h] = rope(ak[:, h * LANES:(h + 1) * LANES]).astype(BF16)
    ones_tile = (lax.broadcasted_iota(jnp.int32, (BF16_ROWS, tm), 0) == 0).astype(BF16)
    for h in range(A_HEADS):
        avt_ref[0, 0, h, 0:A_V_DIM, :] = avt[h * A_V_DIM:(h + 1) * A_V_DIM, :].astype(BF16)
        avt_ref[0, 0, h, A_V_DIM:A_V_ROWS, :] = ones_tile


def _inproj_call(x2d, preg, wmqk, w1, wavt, wgt, convw, convb, rc, rs1, rs2, B, S, tm):
    T = B * S
    tps = S // tm
    nh = tm // HALO
    const = lambda i: (0, 0)
    in_specs = [
        pl.BlockSpec((HALO, D_MODEL), lambda i: (jnp.maximum(i * nh - 1, 0), 0)),
        pl.BlockSpec((tm, D_MODEL), lambda i: (i, 0)),
        pl.BlockSpec((HALO, D_MODEL), lambda i: (jnp.minimum((i + 1) * nh, T // HALO - 1), 0)),
        pl.BlockSpec((1, D_MODEL), const),
        pl.BlockSpec(wmqk.shape, const),
        pl.BlockSpec(w1.shape, const),
        pl.BlockSpec(wavt.shape, const),
        pl.BlockSpec(wgt.shape, const),
        pl.BlockSpec(convw.shape, const),
        pl.BlockSpec(convb.shape, const),
        pl.BlockSpec((tm, LANES), lambda i: (lax.rem(i, tps), 0)),
        pl.BlockSpec((tm, LANES), lambda i: (lax.rem(i, tps), 0)),
        pl.BlockSpec((tm, LANES), lambda i: (lax.rem(i, tps), 0)),
    ]
    bt = lambda i: (i // tps, 0, lax.rem(i, tps), 0)
    out_shape = [
        jax.ShapeDtypeStruct((B, M_HEADS, S, M_QK_DIM), BF16),
        jax.ShapeDtypeStruct((B, M_HEADS, S, M_QK_DIM), BF16),
        jax.ShapeDtypeStruct((B, M_HEADS, S, M_V_DIM), BF16),
        jax.ShapeDtypeStruct((N_GATE_ROWS, T), F32),
        jax.ShapeDtypeStruct((B, A_HEADS, S, LANES), BF16),
        jax.ShapeDtypeStruct((B, A_HEADS, S, LANES), BF16),
        jax.ShapeDtypeStruct((B, tps, A_HEADS, A_V_ROWS, tm), BF16),
    ]
    out_specs = [
        pl.BlockSpec((1, M_HEADS, tm, M_QK_DIM), bt),
        pl.BlockSpec((1, M_HEADS, tm, M_QK_DIM), bt),
        pl.BlockSpec((1, M_HEADS, tm, M_V_DIM), bt),
        pl.BlockSpec((N_GATE_ROWS, tm), lambda i: (0, i)),
        pl.BlockSpec((1, A_HEADS, tm, LANES), bt),
        pl.BlockSpec((1, A_HEADS, tm, LANES), bt),
        pl.BlockSpec((1, 1, A_HEADS, A_V_ROWS, tm), lambda i: (i // tps, lax.rem(i, tps), 0, 0, 0)),
    ]
    return pl.pallas_call(
        functools.partial(_inproj_body, tps, tm),
        grid=(T // tm,),
        in_specs=in_specs,
        out_specs=out_specs,
        out_shape=out_shape,
        scratch_shapes=[pltpu.VMEM((tm + 2 * HALO, D_MODEL), BF16)],
        compiler_params=pltpu.CompilerParams(dimension_semantics=("arbitrary",),
                                             vmem_limit_bytes=VMEM_LIMIT),
        name="inproj",
    )(x2d, x2d, x2d, preg, wmqk, w1, wavt, wgt, convw, convb, rc, rs1, rs2)


N_GATE_COLS = 5


def _gates_body(S, ct, f_ref, i_ref, fb_ref, ib_ref, rows_ref, cols_ref):
    L = M_CHUNK
    f = f_ref[...] + fb_ref[...]
    ig = i_ref[...] + ib_ref[...]
    lf = jnp.minimum(f, 0.0) - jnp.log1p(jnp.exp(-jnp.abs(f)))
    lane = lax.broadcasted_iota(jnp.int32, f.shape, 1)
    pos = lane & (L - 1)
    fwd = lax.broadcasted_iota(jnp.int32, f.shape, 0) == 0

    def from_left(x, s):
        return pltpu.roll(x, s, axis=1)

    def from_right(x, s):
        return pltpu.roll(x, S - s, axis=1)

    def chunk_scans(x, op, ident):
        pre, suf = x, x
        s = 1
        while s < L:
            pre = op(pre, jnp.where(pos >= s, from_left(pre, s), ident))
            suf = op(suf, jnp.where(pos < L - s, from_right(suf, s), ident))
            s *= 2
        return pre, suf

    pre, suf = chunk_scans(lf, jnp.add, 0.0)
    a = jnp.where(fwd, pre, suf)
    a_tot = pre + suf - lf
    w = ig - a
    wpre, wsuf = chunk_scans(w, jnp.maximum, -jnp.inf)
    rowmax = a + jnp.where(fwd, wpre, wsuf)
    g_max = a_tot + jnp.maximum(wpre, wsuf)

    af, gf, ab, gb = a_tot, g_max, a_tot, g_max
    s = L
    while s < S:
        ok = lane >= s
        a_sh, g_sh = from_left(af, s), from_left(gf, s)
        gf = jnp.where(ok, jnp.maximum(g_sh + af, gf), gf)
        af = jnp.where(ok, af + a_sh, af)
        ok = lane < S - s
        a_sh, g_sh = from_right(ab, s), from_right(gb, s)
        gb = jnp.where(ok, jnp.maximum(g_sh + ab, gb), gb)
        ab = jnp.where(ok, ab + a_sh, ab)
        s *= 2
    m_f = jnp.maximum(af, gf)
    m_b = jnp.maximum(ab, gb)
    m_prev = jnp.where(fwd, jnp.where(lane >= L, from_left(m_f, L), 0.0),
                       jnp.where(lane < S - L, from_right(m_b, L), 0.0))

    inter = a + m_prev
    mj = jnp.maximum(inter, rowmax)
    m_new = jnp.maximum(a_tot + m_prev, g_max)
    cols = (a - mj, jnp.exp(inter - mj), jnp.exp(-mj), jnp.exp(a_tot + w - m_new),
            jnp.exp(a_tot + m_prev - m_new))
    rows_ref[0, 0] = w
    pad = jnp.zeros((LANES - 8 * N_GATE_COLS, ct), F32)
    for t in range(S // ct):
        sl = slice(t * ct, (t + 1) * ct)
        cols_ref[0, 0, sl, :] = jnp.concatenate([c[:, sl] for c in cols] + [pad], axis=0).T


def _gates_call(gt, gb, B, S):
    ct = min(2048, S)
    return pl.pallas_call(
        functools.partial(_gates_body, S, ct),
        grid=(B, M_HEADS),
        in_specs=[pl.BlockSpec((8, S), lambda b, h: (h, b)),
                  pl.BlockSpec((8, S), lambda b, h: (M_HEADS + h, b)),
                  pl.BlockSpec((8, 1), lambda b, h: (h, 0)),
                  pl.BlockSpec((8, 1), lambda b, h: (M_HEADS + h, 0))],
        out_specs=[pl.BlockSpec((1, 1, 8, S), lambda b, h: (b, h, 0, 0)),
                   pl.BlockSpec((1, 1, S, LANES), lambda b, h: (b, h, 0, 0))],
        out_shape=[jax.ShapeDtypeStruct((B, M_HEADS, 8, S), F32),
                   jax.ShapeDtypeStruct((B, M_HEADS, S, LANES), F32)],
        compiler_params=pltpu.CompilerParams(dimension_semantics=("arbitrary", "arbitrary"),
                                             vmem_limit_bytes=VMEM_LIMIT),
        name="gates",
    )(gt, gt, gb, gb)


def _mlstm_body(rev, nchunk, q_ref, k_ref, v_ref, rows_ref, cols_ref, h_ref, c_ref):
    @pl.when(pl.program_id(2) == 0)
    def _():
        c_ref[...] = jnp.zeros_like(c_ref)

    L = M_CHUNK
    d = 1 if rev else 0
    row_i = lax.broadcasted_iota(jnp.int32, (L, L), 0)
    col_i = lax.broadcasted_iota(jnp.int32, (L, L), 1)
    visible = (col_i >= row_i) if rev else (col_i <= row_i)
    ones_col = (lax.broadcasted_iota(jnp.int32, (L, LANES), 1) == 0).astype(BF16)
    order = list(range(nchunk - 1, -1, -1) if rev else range(nchunk))

    def rows_of(c):
        return slice(c * L, (c + 1) * L)

    def gate_col(c, k):
        return cols_ref[0, 0, rows_of(c), 8 * k + d:8 * k + d + 1]

    def v_ext(c):
        return jnp.concatenate([v_ref[0, 0, rows_of(c), :], ones_col], axis=1)

    qk, kv = {}, {}
    for c in order:
        qk[c] = lax.dot_general(q_ref[0, 0, rows_of(c), :], k_ref[0, 0, rows_of(c), :], _NT,
                                preferred_element_type=F32)
        wv = (gate_col(c, 3) * v_ext(c).astype(F32)).astype(BF16)
        kv[c] = lax.dot_general(k_ref[0, 0, rows_of(c), :], wv, _TN, preferred_element_type=F32)
    C = c_ref[...]
    c_in = {}
    for c in order:
        c_in[c] = C.astype(BF16)
        C = gate_col(c, 4)[0:1, :] * C + kv[c]
    c_ref[...] = C
    for c in order:
        u, w_inter, e = gate_col(c, 0), gate_col(c, 1), gate_col(c, 2)
        w_row = rows_ref[0, 0, d:d + 1, rows_of(c)]
        p = jnp.exp(jnp.where(visible, u + w_row, -jnp.inf))
        s = qk[c] * p
        num = (w_inter * jnp.dot(q_ref[0, 0, rows_of(c), :], c_in[c], preferred_element_type=F32)
               + jnp.dot(s.astype(BF16), v_ext(c), preferred_element_type=F32))
        den = num[:, M_V_DIM:M_V_DIM + 1]
        h_ref[0, rows_of(c), :] = num[:, 0:M_V_DIM] / jnp.maximum(jnp.abs(den), e)


def _mlstm_call(rev, mq, mk, mv, rows, cols, B, S, tr):
    ns = S // tr
    pos = (lambda s: ns - 1 - s) if rev else (lambda s: s)
    return pl.pallas_call(
        functools.partial(_mlstm_body, rev, tr // M_CHUNK),
        grid=(B, M_HEADS, ns),
        in_specs=[pl.BlockSpec((1, 1, tr, M_QK_DIM), lambda b, h, s: (b, h, pos(s), 0)),
                  pl.BlockSpec((1, 1, tr, M_QK_DIM), lambda b, h, s: (b, h, pos(s), 0)),
                  pl.BlockSpec((1, 1, tr, M_V_DIM), lambda b, h, s: (b, h, pos(s), 0)),
                  pl.BlockSpec((1, 1, 8, tr), lambda b, h, s: (b, h, 0, pos(s))),
                  pl.BlockSpec((1, 1, tr, LANES), lambda b, h, s: (b, h, pos(s), 0))],
        out_specs=pl.BlockSpec((1, tr, M_V_DIM), lambda b, h, s: (b, pos(s), h)),
        out_shape=jax.ShapeDtypeStruct((B, S, M_WIDTH), F32),
        scratch_shapes=[pltpu.VMEM((M_QK_DIM, M_V_DIM + LANES), F32)],
        compiler_params=pltpu.CompilerParams(dimension_semantics=("arbitrary", "arbitrary", "arbitrary"),
                                             vmem_limit_bytes=VMEM_LIMIT),
        name="mlstm_bwd" if rev else "mlstm_fwd",
    )(mq, mk, mv, rows, cols)


def _attn_body(nkc, nt, tq, tk, tv, lam_init, q_ref, k_ref, vt_ref, lq_ref, g_ref, o_ref,
               qz_ref, st0_ref, st1_ref, m_ref, acc_ref):
    def tile_rows(t):
        return pl.ds(pl.multiple_of(t * tq, tq), tq)

    def load_queries(t):
        q = q_ref[0, 0, tile_rows(t), :]
        lane = lax.broadcasted_iota(jnp.int32, q.shape, 1)
        zero = jnp.zeros_like(q)
        qz_ref[0:tq, :] = jnp.where(lane < A_HEAD_DIM, q, zero)
        qz_ref[tq:2 * tq, :] = jnp.where(lane >= A_HEAD_DIM, q, zero)

    def reset_stats():
        m_ref[...] = jnp.full(m_ref.shape, -jnp.inf, F32)
        acc_ref[...] = jnp.zeros_like(acc_ref)

    strips = [slice(c * MXU_N, (c + 1) * MXU_N) for c in range(2 * tq // MXU_N)]
    nv = tk // tv

    def score_strip(j, st_ref, sl):
        kj = k_ref[0, 0, pl.ds(pl.multiple_of(j * tk, tk), tk), :]
        st = lax.dot_general(kj, qz_ref[sl, :], _NT, preferred_element_type=F32)
        st_ref[:, sl] = st
        return jnp.max(st, axis=0, keepdims=True)

    def accumulate_strip(j, st_ref, sl, cm):
        m_old = m_ref[:, sl]
        m_new = jnp.maximum(m_old, cm)
        alpha = jnp.exp2(m_old - m_new)
        p = jnp.exp2(st_ref[:, sl] - m_new).astype(BF16)
        pv = jnp.dot(vt_ref[0, j * nv, 0], p[0:tv], preferred_element_type=F32)
        for u in range(1, nv):
            pv = pv + jnp.dot(vt_ref[0, j * nv + u, 0], p[u * tv:(u + 1) * tv],
                              preferred_element_type=F32)
        acc_ref[:, sl] = alpha * acc_ref[:, sl] + pv
        m_ref[:, sl] = m_new

    def step(j_next, st_next_ref, j_cur, st_cur_ref, cmax_cur):
        cmax_next = []
        for sl, cm in zip(strips, cmax_cur):
            cmax_next.append(score_strip(j_next, st_next_ref, sl))
            accumulate_strip(j_cur, st_cur_ref, sl, cm)
        return tuple(cmax_next)

    def pair(i, cmax0):
        j = 2 * i
        cmax1 = step(j + 1, st1_ref, j, st0_ref, cmax0)
        return step(j + 2, st0_ref, j + 1, st1_ref, cmax1)

    def all_but_last_chunk(cmax0):
        cmax0 = lax.fori_loop(0, nkc // 2 - 1, pair, cmax0)
        return step(nkc - 1, st1_ref, nkc - 2, st0_ref, cmax0)

    lq = lq_ref[...]
    lam = (jnp.exp(jnp.sum(lq[0:1] * lq[1:2], axis=1, keepdims=True))
           - jnp.exp(jnp.sum(lq[2:3] * lq[3:4], axis=1, keepdims=True)) + lam_init)

    def finalize(t):
        accn = acc_ref[0:A_V_DIM, :] / acc_ref[A_V_DIM:A_V_DIM + 1, :]
        ot = accn[:, 0:tq] - lam * accn[:, tq:2 * tq]
        otn = ot * lax.rsqrt(jnp.mean(ot * ot, axis=0, keepdims=True) + EPS)
        o_ref[tile_rows(t), :] = (otn.T * g_ref[...]) * (1.0 - lam_init)

    def tile(t, cmax0):
        cmax1 = all_but_last_chunk(cmax0)
        load_queries(t + 1)
        cmax_next = []
        for sl, cm in zip(strips, cmax1):
            cmax_next.append(score_strip(0, st0_ref, sl))
            accumulate_strip(nkc - 1, st1_ref, sl, cm)
        finalize(t)
        reset_stats()
        return tuple(cmax_next)

    load_queries(0)
    reset_stats()
    cmax0 = tuple(score_strip(0, st0_ref, sl) for sl in strips)
    cmax0 = lax.fori_loop(0, nt - 1, tile, cmax0)
    cmax1 = all_but_last_chunk(cmax0)
    for sl, cm in zip(strips, cmax1):
        accumulate_strip(nkc - 1, st1_ref, sl, cm)
    finalize(nt - 1)


def _attn_call(aq, ak, avt, lq, ag, B, S, tq, nt, tk, tv, lam_init):
    nkc = S // tk
    nq = S // (nt * tq)
    assert nkc >= 2 and nkc % 2 == 0, "key chunks are processed in pairs"
    assert tk % tv == 0 and S % (nt * tq) == 0
    return pl.pallas_call(
        functools.partial(_attn_body, nkc, nt, tq, tk, tv, lam_init),
        grid=(B, A_HEADS, nq),
        in_specs=[pl.BlockSpec((1, 1, nt * tq, LANES), lambda b, h, i: (b, h, i, 0)),
                  pl.BlockSpec((1, 1, S, LANES), lambda b, h, i: (b, h, 0, 0)),
                  pl.BlockSpec((1, S // tv, 1, A_V_ROWS, tv), lambda b, h, i: (b, 0, h, 0, 0)),
                  pl.BlockSpec(lq.shape, lambda b, h, i: (0, 0)),
                  pl.BlockSpec(ag.shape, lambda b, h, i: (0, 0))],
        out_specs=pl.BlockSpec((nt * tq, A_V_DIM), lambda b, h, i: (b * nq + i, h)),
        out_shape=jax.ShapeDtypeStruct((B * S, A_WIDTH), F32),
        scratch_shapes=[pltpu.VMEM((2 * tq, LANES), BF16),
                        pltpu.VMEM((tk, 2 * tq), F32),
                        pltpu.VMEM((tk, 2 * tq), F32),
                        pltpu.VMEM((1, 2 * tq), F32),
                        pltpu.VMEM((A_V_ROWS, 2 * tq), F32)],
        compiler_params=pltpu.CompilerParams(dimension_semantics=("arbitrary", "arbitrary", "arbitrary"),
                                             vmem_limit_bytes=VMEM_LIMIT),
        name="attn",
    )(aq, ak, avt, lq, ag)


def _epilogue_body(x_ref, hf_ref, hb_ref, o_ref, preg_ref, w4_ref, mng_ref, bm_ref, wbm_ref, wba_ref,
                   wout_ref, postg_ref, out_ref):
    x = x_ref[...]
    xn = (_rms(x) * preg_ref[...]).astype(BF16)

    def proj(lo, hi):
        return jnp.dot(xn, w4_ref[:, lo:hi], preferred_element_type=F32)

    h = hf_ref[...] + hb_ref[...]
    mng = mng_ref[...]
    hn = jnp.concatenate(
        [_rms(h[:, k * M_V_DIM:(k + 1) * M_V_DIM]) * mng[:, k * M_V_DIM:(k + 1) * M_V_DIM]
         for k in range(M_HEADS)], axis=1)
    h_a = jax.nn.sigmoid(proj(0, M_WIDTH)) * hn * _silu(proj(M_WIDTH, 2 * M_WIDTH))
    ya = jnp.dot(h_a.astype(BF16), wbm_ref[...], preferred_element_type=F32)
    h_b = o_ref[...] * _silu(proj(2 * M_WIDTH, 2 * M_WIDTH + A_WIDTH))
    yb = jnp.dot(h_b.astype(BF16), wba_ref[...], preferred_element_type=F32)
    g0 = 2 * M_WIDTH + A_WIDTH
    bm = bm_ref[...]
    g_a = jax.nn.sigmoid(proj(g0, g0 + D_MODEL) + bm[:, 0:D_MODEL])
    g_b = jax.nn.sigmoid(proj(g0 + D_MODEL, g0 + 2 * D_MODEL) + bm[:, D_MODEL:2 * D_MODEL])
    y = g_a * ya + g_b * yb
    out = jnp.dot(y.astype(BF16), wout_ref[...], preferred_element_type=F32)
    out_ref[...] = x + _rms(out) * postg_ref[...]


def _epilogue_call(x2d, hf, hb, o, preg, w4, mng, bm, wbm, wba, wout, postg, te):
    T = x2d.shape[0]
    row = lambda i: (i, 0)
    const = lambda i: (0, 0)
    single = pl.Buffered(1)
    return pl.pallas_call(
        _epilogue_body,
        grid=(T // te,),
        in_specs=[pl.BlockSpec((te, D_MODEL), row),
                  pl.BlockSpec((te, M_WIDTH), row),
                  pl.BlockSpec((te, M_WIDTH), row),
                  pl.BlockSpec((te, A_WIDTH), row),
                  pl.BlockSpec(preg.shape, const),
                  pl.BlockSpec(w4.shape, const, pipeline_mode=single),
                  pl.BlockSpec(mng.shape, const),
                  pl.BlockSpec(bm.shape, const),
                  pl.BlockSpec(wbm.shape, const, pipeline_mode=single),
                  pl.BlockSpec(wba.shape, const, pipeline_mode=single),
                  pl.BlockSpec(wout.shape, const, pipeline_mode=single),
                  pl.BlockSpec(postg.shape, const)],
        out_specs=pl.BlockSpec((te, D_MODEL), row),
        out_shape=jax.ShapeDtypeStruct((T, D_MODEL), F32),
        compiler_params=pltpu.CompilerParams(dimension_semantics=("arbitrary",),
                                             vmem_limit_bytes=VMEM_LIMIT),
        name="epilogue",
    )(x2d, hf, hb, o, preg, w4, mng, bm, wbm, wba, wout, postg)


def _rope_tables(S):
    half = ROPE_DIMS // 2
    inv = ROPE_THETA ** (-jnp.arange(0, ROPE_DIMS, 2, dtype=F32) / ROPE_DIMS)
    ang = jnp.arange(S, dtype=F32)[:, None] * inv[None, :]
    cos, sin = jnp.cos(ang), jnp.sin(ang)
    one = jnp.ones((S, A_HEAD_DIM - ROPE_DIMS), F32)
    zero = jnp.zeros((S, A_HEAD_DIM - ROPE_DIMS), F32)
    zh = jnp.zeros((S, half), F32)
    rc = jnp.concatenate([cos, cos, one], axis=1)
    rs1 = jnp.concatenate([-sin, zh, zero], axis=1)
    rs2 = jnp.concatenate([zh, sin, zero], axis=1)
    rep = LANES // A_HEAD_DIM
    return jnp.tile(rc, (1, rep)), jnp.tile(rs1, (1, rep)), jnp.tile(rs2, (1, rep))


def _gate_layout(m):
    def groups(fw, bw):
        g = jnp.stack([fw, bw], axis=-1)
        g = jnp.concatenate([g, jnp.zeros(g.shape[:-1] + (6,), g.dtype)], axis=-1)
        return g.reshape(g.shape[:-2] + (8 * M_HEADS,))
    return jnp.concatenate([groups(m[..., 1, :], m[..., 3, :]), groups(m[..., 0, :], m[..., 2, :])],
                           axis=-1)


def _layer(x, layer_idx, pre_g, w_in, conv_w, conv_b, gate_b, m_norm_g, lambda_qk, a_norm_g,
           w_branch_m, w_branch_a, b_merge, w_out, post_g):
    B, S, _ = x.shape
    T = B * S
    tm = min(512, S)
    tq = min(1024, S // 2)
    nt = min(8, S // tq)
    tk = min(1024, S // 2)
    tr = min(1024, S)
    te = min(256, T)
    lam_init = 0.8 - 0.6 * math.exp(-0.3 * layer_idx)

    cuts = [0]
    for sz in SPLIT_SIZES:
        cuts.append(cuts[-1] + sz)
    col = lambda k: w_in[:, cuts[k]:cuts[k + 1]]
    wmqk = col(0).astype(BF16)
    w1 = jnp.concatenate([col(1), col(5), col(6)], axis=1).astype(BF16)
    wavt = col(7).T.astype(BF16)
    wgt = _gate_layout(col(2).reshape(D_MODEL, 4, M_HEADS)).T.astype(BF16)
    gb = _gate_layout(gate_b).reshape(N_GATE_ROWS, 1).astype(F32)
    w4 = jnp.concatenate([col(3), col(4), col(8), col(9)], axis=1).astype(BF16)
    rc, rs1, rs2 = _rope_tables(S)

    x2d = x.reshape(T, D_MODEL)
    preg = pre_g.reshape(1, D_MODEL)
    mq, mk, mv, gt, aq, ak, avt = _inproj_call(
        x2d, preg, wmqk, w1, wavt, wgt, conv_w, conv_b.reshape(1, 2 * M_QK), rc, rs1, rs2, B, S, tm)
    rows, cols = _gates_call(gt, gb, B, S)
    hf = _mlstm_call(False, mq, mk, mv, rows, cols, B, S, tr)
    hb = _mlstm_call(True, mq, mk, mv, rows, cols, B, S, tr)
    o = _attn_call(aq, ak, avt, lambda_qk.astype(F32), a_norm_g.reshape(1, A_V_DIM), B, S, tq, nt, tk, tm,
                   lam_init)
    out = _epilogue_call(
        x2d, hf.reshape(T, M_WIDTH), hb.reshape(T, M_WIDTH), o, preg, w4,
        m_norm_g.reshape(1, M_WIDTH), b_merge.reshape(1, 2 * D_MODEL),
        w_branch_m.astype(BF16), w_branch_a.astype(BF16), w_out.astype(BF16),
        post_g.reshape(1, D_MODEL), te)
    return out.reshape(B, S, D_MODEL)


def kernel(x, pre_norm_g, w_in, conv_w, conv_b, gate_b, m_norm_g, lambda_qk, a_norm_g, w_branch_m,
           w_branch_a, b_merge, w_out, post_norm_g):
    for l in range(pre_norm_g.shape[0]):
        x = _layer(x, l, pre_norm_g[l], w_in[l], conv_w[l], conv_b[l], gate_b[l], m_norm_g[l],
                   lambda_qk[l], a_norm_g[l], w_branch_m[l], w_branch_a[l], b_merge[l], w_out[l],
                   post_norm_g[l])
    return x
```

```python
import functools
import math

import jax
import jax.numpy as jnp
from jax import lax
from jax.experimental import pallas as pl
from jax.experimental.pallas import tpu as pltpu

D_MODEL = 1024
M_HEADS = 4
M_QK_DIM = 128
M_V_DIM = 256
M_QK = M_HEADS * M_QK_DIM
M_WIDTH = M_HEADS * M_V_DIM
M_CHUNK = 64
CONV_WIDTH = 4
A_HEADS = 8
A_HEAD_DIM = 64
A_V_DIM = 2 * A_HEAD_DIM
A_QK = A_HEADS * 2 * A_HEAD_DIM
A_WIDTH = A_HEADS * A_V_DIM
ROPE_THETA = 500000.0
ROPE_DIMS = A_HEAD_DIM // 4
EPS = 1e-6
SPLIT_SIZES = (2 * M_QK, M_WIDTH, 4 * M_HEADS, M_WIDTH, M_WIDTH, A_QK, A_QK, A_WIDTH, A_WIDTH, 2 * D_MODEL)

LANES = 128
MXU_N = 256
BF16_ROWS = 16
A_V_ROWS = A_V_DIM + BF16_ROWS
HALO = 16
N_GATE_ROWS = 2 * 8 * M_HEADS
VMEM_LIMIT = 56 * 1024 * 1024

F32 = jnp.float32
BF16 = jnp.bfloat16
_NT = (((1,), (1,)), ((), ()))
_TN = (((0,), (0,)), ((), ()))


def _rms(xf):
    return xf * lax.rsqrt(jnp.mean(xf * xf, axis=-1, keepdims=True) + EPS)


def _silu(y):
    return y * jax.nn.sigmoid(y)


def _inproj_body(tps, tm, xprev_ref, x_ref, xnext_ref, preg_ref, wmqk_ref, w1_ref, wavt_ref, wgt_ref,
                 convw_ref, convb_ref, rc_ref, rs_ref,
                 mq_ref, mk_ref, mv_ref, gt_ref, aq_ref, ak_ref, avt_ref, xext_ref):
    t = lax.rem(pl.program_id(0), tps)
    g = preg_ref[...]
    xn = (_rms(x_ref[...]) * g).astype(BF16)
    xp = jnp.where(t != 0, _rms(xprev_ref[...]) * g, 0.0).astype(BF16)
    xq = jnp.where(t != tps - 1, _rms(xnext_ref[...]) * g, 0.0).astype(BF16)
    xext_ref[0:HALO, :] = xp
    xext_ref[HALO:HALO + tm, :] = xn
    xext_ref[HALO + tm:, :] = xq
    n_ext = tm + 2 * HALO
    ext = jnp.dot(xext_ref[...], wmqk_ref[...], preferred_element_type=F32)
    mv = jnp.dot(xn, w1_ref[:, 0:M_WIDTH], preferred_element_type=F32)
    aq = jnp.dot(xn, w1_ref[:, M_WIDTH:M_WIDTH + A_QK], preferred_element_type=F32)
    ak = jnp.dot(xn, w1_ref[:, M_WIDTH + A_QK:M_WIDTH + 2 * A_QK], preferred_element_type=F32)
    avt = lax.dot_general(wavt_ref[...], xn, _NT, preferred_element_type=F32)
    gt_ref[...] = lax.dot_general(wgt_ref[...], xn, _NT, preferred_element_type=F32)

    left = CONV_WIDTH // 2
    y = None
    for j in range(CONV_WIDTH):
        sh = (left - j) % n_ext
        r = ext if sh == 0 else pltpu.roll(ext, sh, axis=0)
        term = convw_ref[j:j + 1, :] * r[HALO:HALO + tm, :]
        y = term if y is None else y + term
    y = _silu(y + convb_ref[...])
    qscale = M_QK_DIM ** -0.5
    for h in range(M_HEADS):
        mq_ref[0, h] = (y[:, h * M_QK_DIM:(h + 1) * M_QK_DIM] * qscale).astype(BF16)
        mk_ref[0, h] = y[:, M_QK + h * M_QK_DIM:M_QK + (h + 1) * M_QK_DIM].astype(BF16)
    for h in range(M_HEADS):
        mv_ref[0, h] = mv[:, h * M_V_DIM:(h + 1) * M_V_DIM].astype(BF16)
    rc, rs = rc_ref[...], rs_ref[...]
    half = ROPE_DIMS // 2
    first_half = (lax.broadcasted_iota(jnp.int32, rc.shape, 1) & (A_HEAD_DIM - 1)) < half

    def rope(a):
        partner = jnp.where(first_half, pltpu.roll(a, LANES - half, axis=1), pltpu.roll(a, half, axis=1))
        return a * rc + partner * rs

    ascale = A_HEAD_DIM ** -0.5 * math.log2(math.e)
    for h in range(A_HEADS):
        aq_ref[0, h] = (rope(aq[:, h * LANES:(h + 1) * LANES]) * ascale).astype(BF16)
        ak_ref[0, h] = rope(ak[:, h * LANES:(h + 1) * LANES]).astype(BF16)
    ones_tile = (lax.broadcasted_iota(jnp.int32, (BF16_ROWS, tm), 0) == 0).astype(BF16)
    for h in range(A_HEADS):
        avt_ref[0, 0, h, 0:A_V_DIM, :] = avt[h * A_V_DIM:(h + 1) * A_V_DIM, :].astype(BF16)
        avt_ref[0, 0, h, A_V_DIM:A_V_ROWS, :] = ones_tile


def _inproj_call(x2d, preg, wmqk, w1, wavt, wgt, convw, convb, rc, rs, B, S, tm):
    T = B * S
    tps = S // tm
    nh = tm // HALO
    const = lambda i: (0, 0)
    in_specs = [
        pl.BlockSpec((HALO, D_MODEL), lambda i: (jnp.maximum(i * nh - 1, 0), 0)),
        pl.BlockSpec((tm, D_MODEL), lambda i: (i, 0)),
        pl.BlockSpec((HALO, D_MODEL), lambda i: (jnp.minimum((i + 1) * nh, T // HALO - 1), 0)),
        pl.BlockSpec((1, D_MODEL), const),
        pl.BlockSpec(wmqk.shape, const),
        pl.BlockSpec(w1.shape, const),
        pl.BlockSpec(wavt.shape, const),
        pl.BlockSpec(wgt.shape, const),
        pl.BlockSpec(convw.shape, const),
        pl.BlockSpec(convb.shape, const),
        pl.BlockSpec((tm, LANES), lambda i: (lax.rem(i, tps), 0)),
        pl.BlockSpec((tm, LANES), lambda i: (lax.rem(i, tps), 0)),
    ]
    bt = lambda i: (i // tps, 0, lax.rem(i, tps), 0)
    out_shape = [
        jax.ShapeDtypeStruct((B, M_HEADS, S, M_QK_DIM), BF16),
        jax.ShapeDtypeStruct((B, M_HEADS, S, M_QK_DIM), BF16),
        jax.ShapeDtypeStruct((B, M_HEADS, S, M_V_DIM), BF16),
        jax.ShapeDtypeStruct((N_GATE_ROWS, T), F32),
        jax.ShapeDtypeStruct((B, A_HEADS, S, LANES), BF16),
        jax.ShapeDtypeStruct((B, A_HEADS, S, LANES), BF16),
        jax.ShapeDtypeStruct((B, tps, A_HEADS, A_V_ROWS, tm), BF16),
    ]
    out_specs = [
        pl.BlockSpec((1, M_HEADS, tm, M_QK_DIM), bt),
        pl.BlockSpec((1, M_HEADS, tm, M_QK_DIM), bt),
        pl.BlockSpec((1, M_HEADS, tm, M_V_DIM), bt),
        pl.BlockSpec((N_GATE_ROWS, tm), lambda i: (0, i)),
        pl.BlockSpec((1, A_HEADS, tm, LANES), bt),
        pl.BlockSpec((1, A_HEADS, tm, LANES), bt),
        pl.BlockSpec((1, 1, A_HEADS, A_V_ROWS, tm), lambda i: (i // tps, lax.rem(i, tps), 0, 0, 0)),
    ]
    return pl.pallas_call(
        functools.partial(_inproj_body, tps, tm),
        grid=(T // tm,),
        in_specs=in_specs,
        out_specs=out_specs,
        out_shape=out_shape,
        scratch_shapes=[pltpu.VMEM((tm + 2 * HALO, D_MODEL), BF16)],
        compiler_params=pltpu.CompilerParams(dimension_semantics=("arbitrary",),
                                             vmem_limit_bytes=VMEM_LIMIT),
        name="inproj",
    )(x2d, x2d, x2d, preg, wmqk, w1, wavt, wgt, convw, convb, rc, rs)


N_GATE_COLS = 5


def _gates_body(S, ct, f_ref, i_ref, fb_ref, ib_ref, rows_ref, cols_ref):
    L = M_CHUNK
    f = f_ref[...] + fb_ref[...]
    ig = i_ref[...] + ib_ref[...]
    lf = jnp.minimum(f, 0.0) - jnp.log1p(jnp.exp(-jnp.abs(f)))
    lane = lax.broadcasted_iota(jnp.int32, f.shape, 1)
    pos = lane & (L - 1)
    fwd = lax.broadcasted_iota(jnp.int32, f.shape, 0) == 0

    def from_left(x, s):
        return pltpu.roll(x, s, axis=1)

    def from_right(x, s):
        return pltpu.roll(x, S - s, axis=1)

    def chunk_scans(x, op, ident):
        pre, suf = x, x
        s = 1
        while s < L:
            pre = op(pre, jnp.where(pos >= s, from_left(pre, s), ident))
            suf = op(suf, jnp.where(pos < L - s, from_right(suf, s), ident))
            s *= 2
        return pre, suf

    pre, suf = chunk_scans(lf, jnp.add, 0.0)
    a = jnp.where(fwd, pre, suf)
    a_tot = pre + suf - lf
    w = ig - a
    wpre, wsuf = chunk_scans(w, jnp.maximum, -jnp.inf)
    rowmax = a + jnp.where(fwd, wpre, wsuf)
    g_max = a_tot + jnp.maximum(wpre, wsuf)

    af, gf, ab, gb = a_tot, g_max, a_tot, g_max
    s = L
    while s < S:
        ok = lane >= s
        a_sh, g_sh = from_left(af, s), from_left(gf, s)
        gf = jnp.where(ok, jnp.maximum(g_sh + af, gf), gf)
        af = jnp.where(ok, af + a_sh, af)
        ok = lane < S - s
        a_sh, g_sh = from_right(ab, s), from_right(gb, s)
        gb = jnp.where(ok, jnp.maximum(g_sh + ab, gb), gb)
        ab = jnp.where(ok, ab + a_sh, ab)
        s *= 2
    m_f = jnp.maximum(af, gf)
    m_b = jnp.maximum(ab, gb)
    m_prev = jnp.where(fwd, jnp.where(lane >= L, from_left(m_f, L), 0.0),
                       jnp.where(lane < S - L, from_right(m_b, L), 0.0))

    inter = a + m_prev
    mj = jnp.maximum(inter, rowmax)
    m_new = jnp.maximum(a_tot + m_prev, g_max)
    cols = (a - mj, jnp.exp(inter - mj), jnp.exp(-mj), jnp.exp(a_tot + w - m_new),
            jnp.exp(a_tot + m_prev - m_new))
    rows_ref[0, 0] = w
    pad = jnp.zeros((LANES - 8 * N_GATE_COLS, ct), F32)
    for t in range(S // ct):
        sl = slice(t * ct, (t + 1) * ct)
        cols_ref[0, 0, sl, :] = jnp.concatenate([c[:, sl] for c in cols] + [pad], axis=0).T


def _gates_call(gt, gb, B, S):
    ct = min(2048, S)
    return pl.pallas_call(
        functools.partial(_gates_body, S, ct),
        grid=(B, M_HEADS),
        in_specs=[pl.BlockSpec((8, S), lambda b, h: (h, b)),
                  pl.BlockSpec((8, S), lambda b, h: (M_HEADS + h, b)),
                  pl.BlockSpec((8, 1), lambda b, h: (h, 0)),
                  pl.BlockSpec((8, 1), lambda b, h: (M_HEADS + h, 0))],
        out_specs=[pl.BlockSpec((1, 1, 8, S), lambda b, h: (b, h, 0, 0)),
                   pl.BlockSpec((1, 1, S, LANES), lambda b, h: (b, h, 0, 0))],
        out_shape=[jax.ShapeDtypeStruct((B, M_HEADS, 8, S), F32),
                   jax.ShapeDtypeStruct((B, M_HEADS, S, LANES), F32)],
        compiler_params=pltpu.CompilerParams(dimension_semantics=("arbitrary", "arbitrary"),
                                             vmem_limit_bytes=VMEM_LIMIT),
        name="gates",
    )(gt, gt, gb, gb)


def _mlstm_body(rev, nchunk, q_ref, k_ref, v_ref, rows_ref, cols_ref, h_ref, c_ref):
    @pl.when(pl.program_id(2) == 0)
    def _():
        c_ref[...] = jnp.zeros_like(c_ref)

    L = M_CHUNK
    d = 1 if rev else 0
    row_i = lax.broadcasted_iota(jnp.int32, (L, L), 0)
    col_i = lax.broadcasted_iota(jnp.int32, (L, L), 1)
    visible = (col_i >= row_i) if rev else (col_i <= row_i)
    ones_col = (lax.broadcasted_iota(jnp.int32, (L, LANES), 1) == 0).astype(BF16)
    order = list(range(nchunk - 1, -1, -1) if rev else range(nchunk))

    def rows_of(c):
        return slice(c * L, (c + 1) * L)

    def gate_col(c, k):
        return cols_ref[0, 0, rows_of(c), 8 * k + d:8 * k + d + 1]

    def v_ext(c):
        return jnp.concatenate([v_ref[0, 0, rows_of(c), :], ones_col], axis=1)

    qk, kv = {}, {}
    for c in order:
        qk[c] = lax.dot_general(q_ref[0, 0, rows_of(c), :], k_ref[0, 0, rows_of(c), :], _NT,
                                preferred_element_type=F32)
        wv = (gate_col(c, 3) * v_ext(c).astype(F32)).astype(BF16)
        kv[c] = lax.dot_general(k_ref[0, 0, rows_of(c), :], wv, _TN, preferred_element_type=F32)
    C = c_ref[...]
    c_in = {}
    for c in order:
        c_in[c] = C.astype(BF16)
        C = gate_col(c, 4)[0:1, :] * C + kv[c]
    c_ref[...] = C
    for c in order:
        u, w_inter, e = gate_col(c, 0), gate_col(c, 1), gate_col(c, 2)
        w_row = rows_ref[0, 0, d:d + 1, rows_of(c)]
        p = jnp.exp(jnp.where(visible, u + w_row, -jnp.inf))
        s = qk[c] * p
        num = (w_inter * jnp.dot(q_ref[0, 0, rows_of(c), :], c_in[c], preferred_element_type=F32)
               + jnp.dot(s.astype(BF16), v_ext(c), preferred_element_type=F32))
        den = num[:, M_V_DIM:M_V_DIM + 1]
        h_ref[0, rows_of(c), :] = num[:, 0:M_V_DIM] / jnp.maximum(jnp.abs(den), e)


def _mlstm_call(rev, mq, mk, mv, rows, cols, B, S, tr):
    ns = S // tr
    pos = (lambda s: ns - 1 - s) if rev else (lambda s: s)
    return pl.pallas_call(
        functools.partial(_mlstm_body, rev, tr // M_CHUNK),
        grid=(B, M_HEADS, ns),
        in_specs=[pl.BlockSpec((1, 1, tr, M_QK_DIM), lambda b, h, s: (b, h, pos(s), 0)),
                  pl.BlockSpec((1, 1, tr, M_QK_DIM), lambda b, h, s: (b, h, pos(s), 0)),
                  pl.BlockSpec((1, 1, tr, M_V_DIM), lambda b, h, s: (b, h, pos(s), 0)),
                  pl.BlockSpec((1, 1, 8, tr), lambda b, h, s: (b, h, 0, pos(s))),
                  pl.BlockSpec((1, 1, tr, LANES), lambda b, h, s: (b, h, pos(s), 0))],
        out_specs=pl.BlockSpec((1, tr, M_V_DIM), lambda b, h, s: (b, pos(s), h)),
        out_shape=jax.ShapeDtypeStruct((B, S, M_WIDTH), F32),
        scratch_shapes=[pltpu.VMEM((M_QK_DIM, M_V_DIM + LANES), F32)],
        compiler_params=pltpu.CompilerParams(dimension_semantics=("arbitrary", "arbitrary", "arbitrary"),
                                             vmem_limit_bytes=VMEM_LIMIT),
        name="mlstm_bwd" if rev else "mlstm_fwd",
    )(mq, mk, mv, rows, cols)


def _attn_body(nkc, nt, tq, tk, tv, lam_init, q_ref, k_ref, vt_ref, lq_ref, g_ref, o_ref,
               qz_ref, st0_ref, st1_ref, m_ref, acc_ref):
    def tile_rows(t):
        return pl.ds(pl.multiple_of(t * tq, tq), tq)

    def load_queries(t):
        q = q_ref[0, 0, tile_rows(t), :]
        lane = lax.broadcasted_iota(jnp.int32, q.shape, 1)
        zero = jnp.zeros_like(q)
        qz_ref[0:tq, :] = jnp.where(lane < A_HEAD_DIM, q, zero)
        qz_ref[tq:2 * tq, :] = jnp.where(lane >= A_HEAD_DIM, q, zero)

    def reset_stats():
        m_ref[...] = jnp.full(m_ref.shape, -jnp.inf, F32)
        acc_ref[...] = jnp.zeros_like(acc_ref)

    strips = [slice(c * MXU_N, (c + 1) * MXU_N) for c in range(2 * tq // MXU_N)]
    nv = tk // tv

    def score_strip(j, st_ref, sl):
        kj = k_ref[0, 0, pl.ds(pl.multiple_of(j * tk, tk), tk), :]
        st = lax.dot_general(kj, qz_ref[sl, :], _NT, preferred_element_type=F32)
        st_ref[:, sl] = st
        return jnp.max(st, axis=0, keepdims=True)

    def accumulate_strip(j, st_ref, sl, cm):
        m_old = m_ref[:, sl]
        m_new = jnp.maximum(m_old, cm)
        alpha = jnp.exp2(m_old - m_new)
        p = jnp.exp2(st_ref[:, sl] - m_new).astype(BF16)
        pv = jnp.dot(vt_ref[0, j * nv, 0], p[0:tv], preferred_element_type=F32)
        for u in range(1, nv):
            pv = pv + jnp.dot(vt_ref[0, j * nv + u, 0], p[u * tv:(u + 1) * tv],
                              preferred_element_type=F32)
        acc_ref[:, sl] = alpha * acc_ref[:, sl] + pv
        m_ref[:, sl] = m_new

    def step(j_next, st_next_ref, j_cur, st_cur_ref, cmax_cur):
        cmax_next = []
        for sl, cm in zip(strips, cmax_cur):
            cmax_next.append(score_strip(j_next, st_next_ref, sl))
            accumulate_strip(j_cur, st_cur_ref, sl, cm)
        return tuple(cmax_next)

    def pair(i, cmax0):
        j = 2 * i
        cmax1 = step(j + 1, st1_ref, j, st0_ref, cmax0)
        return step(j + 2, st0_ref, j + 1, st1_ref, cmax1)

    def two_pairs(i, cmax0):
        return pair(2 * i + 1, pair(2 * i, cmax0))

    def all_but_last_chunk(cmax0):
        npairs = nkc // 2 - 1
        cmax0 = lax.fori_loop(0, npairs // 2, two_pairs, cmax0)
        if npairs % 2:
            cmax0 = pair(npairs - 1, cmax0)
        return step(nkc - 1, st1_ref, nkc - 2, st0_ref, cmax0)

    lq = lq_ref[...]
    lam = (jnp.exp(jnp.sum(lq[0:1] * lq[1:2], axis=1, keepdims=True))
           - jnp.exp(jnp.sum(lq[2:3] * lq[3:4], axis=1, keepdims=True)) + lam_init)

    def finalize(t):
        accn = acc_ref[0:A_V_DIM, :] / acc_ref[A_V_DIM:A_V_DIM + 1, :]
        ot = accn[:, 0:tq] - lam * accn[:, tq:2 * tq]
        otn = ot * lax.rsqrt(jnp.mean(ot * ot, axis=0, keepdims=True) + EPS)
        o_ref[tile_rows(t), :] = (otn.T * g_ref[...]) * (1.0 - lam_init)

    def tile(t, cmax0):
        cmax1 = all_but_last_chunk(cmax0)
        load_queries(t + 1)
        cmax_next = []
        for sl, cm in zip(strips, cmax1):
            cmax_next.append(score_strip(0, st0_ref, sl))
            accumulate_strip(nkc - 1, st1_ref, sl, cm)
        finalize(t)
        reset_stats()
        return tuple(cmax_next)

    load_queries(0)
    reset_stats()
    cmax0 = tuple(score_strip(0, st0_ref, sl) for sl in strips)
    cmax0 = lax.fori_loop(0, nt - 1, tile, cmax0)
    cmax1 = all_but_last_chunk(cmax0)
    for sl, cm in zip(strips, cmax1):
        accumulate_strip(nkc - 1, st1_ref, sl, cm)
    finalize(nt - 1)


def _attn_call(aq, ak, avt, lq, ag, B, S, tq, nt, tk, tv, lam_init):
    nkc = S // tk
    nq = S // (nt * tq)
    assert nkc >= 2 and nkc % 2 == 0, "key chunks are processed in pairs"
    assert tk % tv == 0 and S % (nt * tq) == 0
    return pl.pallas_call(
        functools.partial(_attn_body, nkc, nt, tq, tk, tv, lam_init),
        grid=(B, A_HEADS, nq),
        in_specs=[pl.BlockSpec((1, 1, nt * tq, LANES), lambda b, h, i: (b, h, i, 0)),
                  pl.BlockSpec((1, 1, S, LANES), lambda b, h, i: (b, h, 0, 0)),
                  pl.BlockSpec((1, S // tv, 1, A_V_ROWS, tv), lambda b, h, i: (b, 0, h, 0, 0)),
                  pl.BlockSpec(lq.shape, lambda b, h, i: (0, 0)),
                  pl.BlockSpec(ag.shape, lambda b, h, i: (0, 0))],
        out_specs=pl.BlockSpec((nt * tq, A_V_DIM), lambda b, h, i: (b * nq + i, h)),
        out_shape=jax.ShapeDtypeStruct((B * S, A_WIDTH), F32),
        scratch_shapes=[pltpu.VMEM((2 * tq, LANES), BF16),
                        pltpu.VMEM((tk, 2 * tq), F32),
                        pltpu.VMEM((tk, 2 * tq), F32),
                        pltpu.VMEM((1, 2 * tq), F32),
                        pltpu.VMEM((A_V_ROWS, 2 * tq), F32)],
        compiler_params=pltpu.CompilerParams(dimension_semantics=("arbitrary", "arbitrary", "arbitrary"),
                                             vmem_limit_bytes=VMEM_LIMIT),
        name="attn",
    )(aq, ak, avt, lq, ag)


def _epilogue_body(x_ref, hf_ref, hb_ref, o_ref, preg_ref, w4_ref, mng_ref, bm_ref, wbm_ref, wba_ref,
                   wout_ref, postg_ref, out_ref):
    x = x_ref[...]
    xn = (_rms(x) * preg_ref[...]).astype(BF16)

    def proj(lo, hi):
        return jnp.dot(xn, w4_ref[:, lo:hi], preferred_element_type=F32)

    h = hf_ref[...] + hb_ref[...]
    mng = mng_ref[...]
    hn = jnp.concatenate(
        [_rms(h[:, k * M_V_DIM:(k + 1) * M_V_DIM]) * mng[:, k * M_V_DIM:(k + 1) * M_V_DIM]
         for k in range(M_HEADS)], axis=1)
    h_a = jax.nn.sigmoid(proj(0, M_WIDTH)) * hn * _silu(proj(M_WIDTH, 2 * M_WIDTH))
    ya = jnp.dot(h_a.astype(BF16), wbm_ref[...], preferred_element_type=F32)
    h_b = o_ref[...] * _silu(proj(2 * M_WIDTH, 2 * M_WIDTH + A_WIDTH))
    yb = jnp.dot(h_b.astype(BF16), wba_ref[...], preferred_element_type=F32)
    g0 = 2 * M_WIDTH + A_WIDTH
    bm = bm_ref[...]
    g_a = jax.nn.sigmoid(proj(g0, g0 + D_MODEL) + bm[:, 0:D_MODEL])
    g_b = jax.nn.sigmoid(proj(g0 + D_MODEL, g0 + 2 * D_MODEL) + bm[:, D_MODEL:2 * D_MODEL])
    y = g_a * ya + g_b * yb
    out = jnp.dot(y.astype(BF16), wout_ref[...], preferred_element_type=F32)
    out_ref[...] = x + _rms(out) * postg_ref[...]


def _epilogue_call(x2d, hf, hb, o, preg, w4, mng, bm, wbm, wba, wout, postg, te):
    T = x2d.shape[0]
    row = lambda i: (i, 0)
    const = lambda i: (0, 0)
    single = pl.Buffered(1)
    return pl.pallas_call(
        _epilogue_body,
        grid=(T // te,),
        in_specs=[pl.BlockSpec((te, D_MODEL), row),
                  pl.BlockSpec((te, M_WIDTH), row),
                  pl.BlockSpec((te, M_WIDTH), row),
                  pl.BlockSpec((te, A_WIDTH), row),
                  pl.BlockSpec(preg.shape, const),
                  pl.BlockSpec(w4.shape, const, pipeline_mode=single),
                  pl.BlockSpec(mng.shape, const),
                  pl.BlockSpec(bm.shape, const),
                  pl.BlockSpec(wbm.shape, const, pipeline_mode=single),
                  pl.BlockSpec(wba.shape, const, pipeline_mode=single),
                  pl.BlockSpec(wout.shape, const, pipeline_mode=single),
                  pl.BlockSpec(postg.shape, const)],
        out_specs=pl.BlockSpec((te, D_MODEL), row),
        out_shape=jax.ShapeDtypeStruct((T, D_MODEL), F32),
        compiler_params=pltpu.CompilerParams(dimension_semantics=("arbitrary",),
                                             vmem_limit_bytes=VMEM_LIMIT),
        name="epilogue",
    )(x2d, hf, hb, o, preg, w4, mng, bm, wbm, wba, wout, postg)


def _rope_tables(S):
    inv = ROPE_THETA ** (-jnp.arange(0, ROPE_DIMS, 2, dtype=F32) / ROPE_DIMS)
    ang = jnp.arange(S, dtype=F32)[:, None] * inv[None, :]
    cos, sin = jnp.cos(ang), jnp.sin(ang)
    one = jnp.ones((S, A_HEAD_DIM - ROPE_DIMS), F32)
    zero = jnp.zeros((S, A_HEAD_DIM - ROPE_DIMS), F32)
    rc = jnp.concatenate([cos, cos, one], axis=1)
    rs = jnp.concatenate([-sin, sin, zero], axis=1)
    rep = LANES // A_HEAD_DIM
    return jnp.tile(rc, (1, rep)), jnp.tile(rs, (1, rep))


def _gate_layout(m):
    def groups(fw, bw):
        g = jnp.stack([fw, bw], axis=-1)
        g = jnp.concatenate([g, jnp.zeros(g.shape[:-1] + (6,), g.dtype)], axis=-1)
        return g.reshape(g.shape[:-2] + (8 * M_HEADS,))
    return jnp.concatenate([groups(m[..., 1, :], m[..., 3, :]), groups(m[..., 0, :], m[..., 2, :])],
                           axis=-1)


def _layer(x, layer_idx, pre_g, w_in, conv_w, conv_b, gate_b, m_norm_g, lambda_qk, a_norm_g,
           w_branch_m, w_branch_a, b_merge, w_out, post_g):
    B, S, _ = x.shape
    T = B * S
    tm = min(512, S)
    tq = min(1024, S // 2)
    nt = min(8, S // tq)
    tk = min(1024, S // 2)
    tr = min(1024, S)
    te = min(256, T)
    lam_init = 0.8 - 0.6 * math.exp(-0.3 * layer_idx)

    cuts = [0]
    for sz in SPLIT_SIZES:
        cuts.append(cuts[-1] + sz)
    col = lambda k: w_in[:, cuts[k]:cuts[k + 1]]
    wmqk = col(0).astype(BF16)
    w1 = jnp.concatenate([col(1), col(5), col(6)], axis=1).astype(BF16)
    wavt = col(7).T.astype(BF16)
    wgt = _gate_layout(col(2).reshape(D_MODEL, 4, M_HEADS)).T.astype(BF16)
    gb = _gate_layout(gate_b).reshape(N_GATE_ROWS, 1).astype(F32)
    w4 = jnp.concatenate([col(3), col(4), col(8), col(9)], axis=1).astype(BF16)
    rc, rs = _rope_tables(S)

    x2d = x.reshape(T, D_MODEL)
    preg = pre_g.reshape(1, D_MODEL)
    mq, mk, mv, gt, aq, ak, avt = _inproj_call(
        x2d, preg, wmqk, w1, wavt, wgt, conv_w, conv_b.reshape(1, 2 * M_QK), rc, rs, B, S, tm)
    rows, cols = _gates_call(gt, gb, B, S)
    hf = _mlstm_call(False, mq, mk, mv, rows, cols, B, S, tr)
    hb = _mlstm_call(True, mq, mk, mv, rows, cols, B, S, tr)
    o = _attn_call(aq, ak, avt, lambda_qk.astype(F32), a_norm_g.reshape(1, A_V_DIM), B, S, tq, nt, tk, tm,
                   lam_init)
    out = _epilogue_call(
        x2d, hf.reshape(T, M_WIDTH), hb.reshape(T, M_WIDTH), o, preg, w4,
        m_norm_g.reshape(1, M_WIDTH), b_merge.reshape(1, 2 * D_MODEL),
        w_branch_m.astype(BF16), w_branch_a.astype(BF16), w_out.astype(BF16),
        post_g.reshape(1, D_MODEL), te)
    return out.reshape(B, S, D_MODEL)


def kernel(x, pre_norm_g, w_in, conv_w, conv_b, gate_b, m_norm_g, lambda_qk, a_norm_g, w_branch_m,
           w_branch_a, b_merge, w_out, post_norm_g):
    for l in range(pre_norm_g.shape[0]):
        x = _layer(x, l, pre_norm_g[l], w_in[l], conv_w[l], conv_b[l], gate_b[l], m_norm_g[l],
                   lambda_qk[l], a_norm_g[l], w_branch_m[l], w_branch_a[l], b_merge[l], w_out[l],
                   post_norm_g[l])
    return x
```

```python
import functools
import math

import jax
import jax.numpy as jnp
import numpy as np
from jax import lax
from jax.experimental import pallas as pl
from jax.experimental.pallas import tpu as pltpu

D_MODEL = 1024
M_HEADS = 4
M_QK_DIM = 128
M_V_DIM = 256
M_QK = M_HEADS * M_QK_DIM
M_WIDTH = M_HEADS * M_V_DIM
M_CHUNK = 64
CONV_WIDTH = 4
A_HEADS = 8
A_HEAD_DIM = 64
A_V_DIM = 2 * A_HEAD_DIM
A_QK = A_HEADS * 2 * A_HEAD_DIM
A_WIDTH = A_HEADS * A_V_DIM
ROPE_THETA = 500000.0
ROPE_DIMS = A_HEAD_DIM // 4
EPS = 1e-6
SPLIT_SIZES = (2 * M_QK, M_WIDTH, 4 * M_HEADS, M_WIDTH, M_WIDTH, A_QK, A_QK, A_WIDTH, A_WIDTH, 2 * D_MODEL)

LANES = 128
MXU_N = 256
BF16_ROWS = 16
A_V_ROWS = A_V_DIM + BF16_ROWS
HALO = 16
N_GATE_ROWS = 2 * 8 * M_HEADS
VMEM_LIMIT = 56 * 1024 * 1024

F32 = jnp.float32
BF16 = jnp.bfloat16
_NT = (((1,), (1,)), ((), ()))
_TN = (((0,), (0,)), ((), ()))


def _rms(xf):
    return xf * lax.rsqrt(jnp.mean(xf * xf, axis=-1, keepdims=True) + EPS)


def _silu(y):
    return y * jax.nn.sigmoid(y)


def _inproj_body(tps, tm, xprev_ref, x_ref, xnext_ref, preg_ref, wmqk_ref, w1_ref, wavt_ref, wgt_ref,
                 convw_ref, convb_ref, rc_ref, rs_ref,
                 mq_ref, mk_ref, mv_ref, gt_ref, aq_ref, ak_ref, avt_ref, xext_ref):
    t = lax.rem(pl.program_id(0), tps)
    g = preg_ref[...]
    xn = (_rms(x_ref[...]) * g).astype(BF16)
    xp = jnp.where(t != 0, _rms(xprev_ref[...]) * g, 0.0).astype(BF16)
    xq = jnp.where(t != tps - 1, _rms(xnext_ref[...]) * g, 0.0).astype(BF16)
    xext_ref[0:HALO, :] = xp
    xext_ref[HALO:HALO + tm, :] = xn
    xext_ref[HALO + tm:, :] = xq
    n_ext = tm + 2 * HALO
    ext = jnp.dot(xext_ref[...], wmqk_ref[...], preferred_element_type=F32)
    mv = jnp.dot(xn, w1_ref[:, 0:M_WIDTH], preferred_element_type=F32)
    aq = jnp.dot(xn, w1_ref[:, M_WIDTH:M_WIDTH + A_QK], preferred_element_type=F32)
    ak = jnp.dot(xn, w1_ref[:, M_WIDTH + A_QK:M_WIDTH + 2 * A_QK], preferred_element_type=F32)
    avt = lax.dot_general(wavt_ref[...], xn, _NT, preferred_element_type=F32)
    gt_ref[...] = lax.dot_general(wgt_ref[...], xn, _NT, preferred_element_type=F32)

    left = CONV_WIDTH // 2
    y = None
    for j in range(CONV_WIDTH):
        sh = (left - j) % n_ext
        r = ext if sh == 0 else pltpu.roll(ext, sh, axis=0)
        term = convw_ref[j:j + 1, :] * r[HALO:HALO + tm, :]
        y = term if y is None else y + term
    y = _silu(y + convb_ref[...])
    qscale = M_QK_DIM ** -0.5
    for h in range(M_HEADS):
        mq_ref[0, h] = (y[:, h * M_QK_DIM:(h + 1) * M_QK_DIM] * qscale).astype(BF16)
        mk_ref[0, h] = y[:, M_QK + h * M_QK_DIM:M_QK + (h + 1) * M_QK_DIM].astype(BF16)
    for h in range(M_HEADS):
        mv_ref[0, h] = mv[:, h * M_V_DIM:(h + 1) * M_V_DIM].astype(BF16)
    rc, rs = rc_ref[...], rs_ref[...]
    half = ROPE_DIMS // 2
    first_half = (lax.broadcasted_iota(jnp.int32, rc.shape, 1) & (A_HEAD_DIM - 1)) < half

    def rope(a):
        partner = jnp.where(first_half, pltpu.roll(a, LANES - half, axis=1), pltpu.roll(a, half, axis=1))
        return a * rc + partner * rs

    ascale = A_HEAD_DIM ** -0.5 * math.log2(math.e)
    for h in range(A_HEADS):
        aq_ref[0, h] = (rope(aq[:, h * LANES:(h + 1) * LANES]) * ascale).astype(BF16)
        ak_ref[0, h] = rope(ak[:, h * LANES:(h + 1) * LANES]).astype(BF16)
    ones_tile = (lax.broadcasted_iota(jnp.int32, (BF16_ROWS, tm), 0) == 0).astype(BF16)
    for h in range(A_HEADS):
        avt_ref[0, 0, h, 0:A_V_DIM, :] = avt[h * A_V_DIM:(h + 1) * A_V_DIM, :].astype(BF16)
        avt_ref[0, 0, h, A_V_DIM:A_V_ROWS, :] = ones_tile


def _inproj_call(x2d, preg, wmqk, w1, wavt, wgt, convw, convb, rc, rs, B, S, tm):
    T = B * S
    tps = S // tm
    nh = tm // HALO
    const = lambda i: (0, 0)
    in_specs = [
        pl.BlockSpec((HALO, D_MODEL), lambda i: (jnp.maximum(i * nh - 1, 0), 0)),
        pl.BlockSpec((tm, D_MODEL), lambda i: (i, 0)),
        pl.BlockSpec((HALO, D_MODEL), lambda i: (jnp.minimum((i + 1) * nh, T // HALO - 1), 0)),
        pl.BlockSpec((1, D_MODEL), const),
        pl.BlockSpec(wmqk.shape, const),
        pl.BlockSpec(w1.shape, const),
        pl.BlockSpec(wavt.shape, const),
        pl.BlockSpec(wgt.shape, const),
        pl.BlockSpec(convw.shape, const),
        pl.BlockSpec(convb.shape, const),
        pl.BlockSpec((tm, LANES), lambda i: (lax.rem(i, tps), 0)),
        pl.BlockSpec((tm, LANES), lambda i: (lax.rem(i, tps), 0)),
    ]
    bt = lambda i: (i // tps, 0, lax.rem(i, tps), 0)
    out_shape = [
        jax.ShapeDtypeStruct((B, M_HEADS, S, M_QK_DIM), BF16),
        jax.ShapeDtypeStruct((B, M_HEADS, S, M_QK_DIM), BF16),
        jax.ShapeDtypeStruct((B, M_HEADS, S, M_V_DIM), BF16),
        jax.ShapeDtypeStruct((N_GATE_ROWS, T), F32),
        jax.ShapeDtypeStruct((B, A_HEADS, S, LANES), BF16),
        jax.ShapeDtypeStruct((B, A_HEADS, S, LANES), BF16),
        jax.ShapeDtypeStruct((B, tps, A_HEADS, A_V_ROWS, tm), BF16),
    ]
    out_specs = [
        pl.BlockSpec((1, M_HEADS, tm, M_QK_DIM), bt),
        pl.BlockSpec((1, M_HEADS, tm, M_QK_DIM), bt),
        pl.BlockSpec((1, M_HEADS, tm, M_V_DIM), bt),
        pl.BlockSpec((N_GATE_ROWS, tm), lambda i: (0, i)),
        pl.BlockSpec((1, A_HEADS, tm, LANES), bt),
        pl.BlockSpec((1, A_HEADS, tm, LANES), bt),
        pl.BlockSpec((1, 1, A_HEADS, A_V_ROWS, tm), lambda i: (i // tps, lax.rem(i, tps), 0, 0, 0)),
    ]
    return pl.pallas_call(
        functools.partial(_inproj_body, tps, tm),
        grid=(T // tm,),
        in_specs=in_specs,
        out_specs=out_specs,
        out_shape=out_shape,
        scratch_shapes=[pltpu.VMEM((tm + 2 * HALO, D_MODEL), BF16)],
        compiler_params=pltpu.CompilerParams(dimension_semantics=("arbitrary",),
                                             vmem_limit_bytes=VMEM_LIMIT),
        name="inproj",
    )(x2d, x2d, x2d, preg, wmqk, w1, wavt, wgt, convw, convb, rc, rs)


N_GATE_COLS = 5


def _gates_body(S, ct, f_ref, i_ref, fb_ref, ib_ref, rows_ref, cols_ref):
    L = M_CHUNK
    f = f_ref[...] + fb_ref[...]
    ig = i_ref[...] + ib_ref[...]
    lf = jnp.minimum(f, 0.0) - jnp.log1p(jnp.exp(-jnp.abs(f)))
    lane = lax.broadcasted_iota(jnp.int32, f.shape, 1)
    pos = lane & (L - 1)
    fwd = lax.broadcasted_iota(jnp.int32, f.shape, 0) == 0

    def from_left(x, s):
        return pltpu.roll(x, s, axis=1)

    def from_right(x, s):
        return pltpu.roll(x, S - s, axis=1)

    def chunk_scans(x, op, ident):
        pre, suf = x, x
        s = 1
        while s < L:
            pre = op(pre, jnp.where(pos >= s, from_left(pre, s), ident))
            suf = op(suf, jnp.where(pos < L - s, from_right(suf, s), ident))
            s *= 2
        return pre, suf

    pre, suf = chunk_scans(lf, jnp.add, 0.0)
    a = jnp.where(fwd, pre, suf)
    a_tot = pre + suf - lf
    w = ig - a
    wpre, wsuf = chunk_scans(w, jnp.maximum, -jnp.inf)
    rowmax = a + jnp.where(fwd, wpre, wsuf)
    g_max = a_tot + jnp.maximum(wpre, wsuf)

    af, gf, ab, gb = a_tot, g_max, a_tot, g_max
    s = L
    while s < S:
        ok = lane >= s
        a_sh, g_sh = from_left(af, s), from_left(gf, s)
        gf = jnp.where(ok, jnp.maximum(g_sh + af, gf), gf)
        af = jnp.where(ok, af + a_sh, af)
        ok = lane < S - s
        a_sh, g_sh = from_right(ab, s), from_right(gb, s)
        gb = jnp.where(ok, jnp.maximum(g_sh + ab, gb), gb)
        ab = jnp.where(ok, ab + a_sh, ab)
        s *= 2
    m_f = jnp.maximum(af, gf)
    m_b = jnp.maximum(ab, gb)
    m_prev = jnp.where(fwd, jnp.where(lane >= L, from_left(m_f, L), 0.0),
                       jnp.where(lane < S - L, from_right(m_b, L), 0.0))

    inter = a + m_prev
    mj = jnp.maximum(inter, rowmax)
    m_new = jnp.maximum(a_tot + m_prev, g_max)
    cols = (a - mj, jnp.exp(inter - mj), jnp.exp(-mj), jnp.exp(a_tot + w - m_new),
            jnp.exp(a_tot + m_prev - m_new))
    rows_ref[0, 0] = w
    pad = jnp.zeros((LANES - 8 * N_GATE_COLS, ct), F32)
    for t in range(S // ct):
        sl = slice(t * ct, (t + 1) * ct)
        cols_ref[0, 0, sl, :] = jnp.concatenate([c[:, sl] for c in cols] + [pad], axis=0).T


def _gates_call(gt, gb, B, S):
    ct = min(2048, S)
    return pl.pallas_call(
        functools.partial(_gates_body, S, ct),
        grid=(B, M_HEADS),
        in_specs=[pl.BlockSpec((8, S), lambda b, h: (h, b)),
                  pl.BlockSpec((8, S), lambda b, h: (M_HEADS + h, b)),
                  pl.BlockSpec((8, 1), lambda b, h: (h, 0)),
                  pl.BlockSpec((8, 1), lambda b, h: (M_HEADS + h, 0))],
        out_specs=[pl.BlockSpec((1, 1, 8, S), lambda b, h: (b, h, 0, 0)),
                   pl.BlockSpec((1, 1, S, LANES), lambda b, h: (b, h, 0, 0))],
        out_shape=[jax.ShapeDtypeStruct((B, M_HEADS, 8, S), F32),
                   jax.ShapeDtypeStruct((B, M_HEADS, S, LANES), F32)],
        compiler_params=pltpu.CompilerParams(dimension_semantics=("arbitrary", "arbitrary"),
                                             vmem_limit_bytes=VMEM_LIMIT),
        name="gates",
    )(gt, gt, gb, gb)


class _MlstmScan:
    def __init__(self, rev, nchunk, q_ref, k_ref, v_ref, rows_ref, cols_ref, h_ref, c_ref):
        self.rev, self.d = rev, (1 if rev else 0)
        self.q_ref, self.k_ref, self.v_ref = q_ref, k_ref, v_ref
        self.rows_ref, self.cols_ref, self.h_ref, self.c_ref = rows_ref, cols_ref, h_ref, c_ref
        self.order = list(range(nchunk - 1, -1, -1) if rev else range(nchunk))
        L = M_CHUNK
        row_i = lax.broadcasted_iota(jnp.int32, (L, L), 0)
        col_i = lax.broadcasted_iota(jnp.int32, (L, L), 1)
        self.visible = (col_i >= row_i) if rev else (col_i <= row_i)
        self.ones_col = (lax.broadcasted_iota(jnp.int32, (L, LANES), 1) == 0).astype(BF16)

    @staticmethod
    def rows_of(c):
        return slice(c * M_CHUNK, (c + 1) * M_CHUNK)

    def gate_col(self, c, k):
        return self.cols_ref[0, 0, self.rows_of(c), 8 * k + self.d:8 * k + self.d + 1]

    def v_ext(self, c):
        return jnp.concatenate([self.v_ref[0, 0, self.rows_of(c), :], self.ones_col], axis=1)

    def input_products(self):
        self.qk, self.kv = {}, {}
        for c in self.order:
            q, k = self.q_ref[0, 0, self.rows_of(c), :], self.k_ref[0, 0, self.rows_of(c), :]
            self.qk[c] = lax.dot_general(q, k, _NT, preferred_element_type=F32)
            wv = (self.gate_col(c, 3) * self.v_ext(c).astype(F32)).astype(BF16)
            self.kv[c] = lax.dot_general(k, wv, _TN, preferred_element_type=F32)

    def states(self):
        C = self.c_ref[...]
        self.c_in = {}
        for c in self.order:
            self.c_in[c] = C.astype(BF16)
            C = self.gate_col(c, 4)[0:1, :] * C + self.kv[c]
        self.c_ref[...] = C

    def read_out(self):
        for c in self.order:
            u, w_inter, e = self.gate_col(c, 0), self.gate_col(c, 1), self.gate_col(c, 2)
            w_row = self.rows_ref[0, 0, self.d:self.d + 1, self.rows_of(c)]
            p = jnp.exp(jnp.where(self.visible, u + w_row, -jnp.inf))
            s = self.qk[c] * p
            q = self.q_ref[0, 0, self.rows_of(c), :]
            num = (w_inter * jnp.dot(q, self.c_in[c], preferred_element_type=F32)
                   + jnp.dot(s.astype(BF16), self.v_ext(c), preferred_element_type=F32))
            den = num[:, M_V_DIM:M_V_DIM + 1]
            self.h_ref[0, self.rows_of(c), :] = num[:, 0:M_V_DIM] / jnp.maximum(jnp.abs(den), e)


def _mlstm_body(nchunk, qf_ref, kf_ref, vf_ref, rowsf_ref, colsf_ref, qb_ref, kb_ref, vb_ref, rowsb_ref,
                colsb_ref, hf_ref, hb_ref, cf_ref, cb_ref):
    @pl.when(pl.program_id(2) == 0)
    def _():
        cf_ref[...] = jnp.zeros_like(cf_ref)
        cb_ref[...] = jnp.zeros_like(cb_ref)

    fwd = _MlstmScan(False, nchunk, qf_ref, kf_ref, vf_ref, rowsf_ref, colsf_ref, hf_ref, cf_ref)
    bwd = _MlstmScan(True, nchunk, qb_ref, kb_ref, vb_ref, rowsb_ref, colsb_ref, hb_ref, cb_ref)
    fwd.input_products()
    bwd.input_products()
    fwd.states()
    bwd.states()
    fwd.read_out()
    bwd.read_out()


def _mlstm_call(mq, mk, mv, rows, cols, B, S, tr):
    ns = S // tr

    def specs(pos):
        return [pl.BlockSpec((1, 1, tr, M_QK_DIM), lambda b, h, s: (b, h, pos(s), 0)),
                pl.BlockSpec((1, 1, tr, M_QK_DIM), lambda b, h, s: (b, h, pos(s), 0)),
                pl.BlockSpec((1, 1, tr, M_V_DIM), lambda b, h, s: (b, h, pos(s), 0)),
                pl.BlockSpec((1, 1, 8, tr), lambda b, h, s: (b, h, 0, pos(s))),
                pl.BlockSpec((1, 1, tr, LANES), lambda b, h, s: (b, h, pos(s), 0))]

    fpos = lambda s: s
    bpos = lambda s: ns - 1 - s
    state = pltpu.VMEM((M_QK_DIM, M_V_DIM + LANES), F32)
    return pl.pallas_call(
        functools.partial(_mlstm_body, tr // M_CHUNK),
        grid=(B, M_HEADS, ns),
        in_specs=specs(fpos) + specs(bpos),
        out_specs=[pl.BlockSpec((1, tr, M_V_DIM), lambda b, h, s: (b, fpos(s), h)),
                   pl.BlockSpec((1, tr, M_V_DIM), lambda b, h, s: (b, bpos(s), h))],
        out_shape=[jax.ShapeDtypeStruct((B, S, M_WIDTH), F32),
                   jax.ShapeDtypeStruct((B, S, M_WIDTH), F32)],
        scratch_shapes=[state, state],
        compiler_params=pltpu.CompilerParams(dimension_semantics=("arbitrary", "arbitrary", "arbitrary"),
                                             vmem_limit_bytes=VMEM_LIMIT),
        name="mlstm",
    )(mq, mk, mv, rows, cols, mq, mk, mv, rows, cols)


def _attn_body(nkc, nt, tq, tk, tv, lam_init, q_ref, k_ref, vt_ref, lq_ref, g_ref, o_ref,
               qz_ref, st0_ref, st1_ref, m_ref, acc_ref):
    def tile_rows(t):
        return pl.ds(pl.multiple_of(t * tq, tq), tq)

    def load_queries(t):
        q = q_ref[0, 0, tile_rows(t), :]
        lane = lax.broadcasted_iota(jnp.int32, q.shape, 1)
        zero = jnp.zeros_like(q)
        qz_ref[0:tq, :] = jnp.where(lane < A_HEAD_DIM, q, zero)
        qz_ref[tq:2 * tq, :] = jnp.where(lane >= A_HEAD_DIM, q, zero)

    def reset_stats():
        m_ref[...] = jnp.full(m_ref.shape, -jnp.inf, F32)
        acc_ref[...] = jnp.zeros_like(acc_ref)

    strips = [slice(c * MXU_N, (c + 1) * MXU_N) for c in range(2 * tq // MXU_N)]
    nv = tk // tv

    def score_strip(j, st_ref, sl):
        kj = k_ref[0, 0, pl.ds(pl.multiple_of(j * tk, tk), tk), :]
        st = lax.dot_general(kj, qz_ref[sl, :], _NT, preferred_element_type=F32)
        st_ref[:, sl] = st
        return jnp.max(st, axis=0, keepdims=True)

    def accumulate_strip(j, st_ref, sl, cm):
        m_old = m_ref[:, sl]
        m_new = jnp.maximum(m_old, cm)
        alpha = jnp.exp2(m_old - m_new)
        p = jnp.exp2(st_ref[:, sl] - m_new).astype(BF16)
        pv = jnp.dot(vt_ref[0, j * nv, 0], p[0:tv], preferred_element_type=F32)
        for u in range(1, nv):
            pv = pv + jnp.dot(vt_ref[0, j * nv + u, 0], p[u * tv:(u + 1) * tv],
                              preferred_element_type=F32)
        acc_ref[:, sl] = alpha * acc_ref[:, sl] + pv
        m_ref[:, sl] = m_new

    def step(j_next, st_next_ref, j_cur, st_cur_ref, cmax_cur):
        cmax_next = []
        for sl, cm in zip(strips, cmax_cur):
            cmax_next.append(score_strip(j_next, st_next_ref, sl))
            accumulate_strip(j_cur, st_cur_ref, sl, cm)
        return tuple(cmax_next)

    def pair(i, cmax0):
        j = 2 * i
        cmax1 = step(j + 1, st1_ref, j, st0_ref, cmax0)
        return step(j + 2, st0_ref, j + 1, st1_ref, cmax1)

    def two_pairs(i, cmax0):
        return pair(2 * i + 1, pair(2 * i, cmax0))

    def all_but_last_chunk(cmax0):
        npairs = nkc // 2 - 1
        cmax0 = lax.fori_loop(0, npairs // 2, two_pairs, cmax0)
        if npairs % 2:
            cmax0 = pair(npairs - 1, cmax0)
        return step(nkc - 1, st1_ref, nkc - 2, st0_ref, cmax0)

    lq = lq_ref[...]
    lam = (jnp.exp(jnp.sum(lq[0:1] * lq[1:2], axis=1, keepdims=True))
           - jnp.exp(jnp.sum(lq[2:3] * lq[3:4], axis=1, keepdims=True)) + lam_init)

    def finalize(t):
        accn = acc_ref[0:A_V_DIM, :] / acc_ref[A_V_DIM:A_V_DIM + 1, :]
        ot = accn[:, 0:tq] - lam * accn[:, tq:2 * tq]
        otn = ot * lax.rsqrt(jnp.mean(ot * ot, axis=0, keepdims=True) + EPS)
        o_ref[tile_rows(t), :] = (otn.T * g_ref[...]) * (1.0 - lam_init)

    def tile(t, cmax0):
        cmax1 = all_but_last_chunk(cmax0)
        load_queries(t + 1)
        cmax_next = []
        for sl, cm in zip(strips, cmax1):
            cmax_next.append(score_strip(0, st0_ref, sl))
            accumulate_strip(nkc - 1, st1_ref, sl, cm)
        finalize(t)
        reset_stats()
        return tuple(cmax_next)

    load_queries(0)
    reset_stats()
    cmax0 = tuple(score_strip(0, st0_ref, sl) for sl in strips)
    cmax0 = lax.fori_loop(0, nt - 1, tile, cmax0)
    cmax1 = all_but_last_chunk(cmax0)
    for sl, cm in zip(strips, cmax1):
        accumulate_strip(nkc - 1, st1_ref, sl, cm)
    finalize(nt - 1)


def _attn_call(aq, ak, avt, lq, ag, B, S, tq, nt, tk, tv, lam_init):
    nkc = S // tk
    nq = S // (nt * tq)
    assert nkc >= 2 and nkc % 2 == 0, "key chunks are processed in pairs"
    assert tk % tv == 0 and S % (nt * tq) == 0
    return pl.pallas_call(
        functools.partial(_attn_body, nkc, nt, tq, tk, tv, lam_init),
        grid=(B, A_HEADS, nq),
        in_specs=[pl.BlockSpec((1, 1, nt * tq, LANES), lambda b, h, i: (b, h, i, 0)),
                  pl.BlockSpec((1, 1, S, LANES), lambda b, h, i: (b, h, 0, 0)),
                  pl.BlockSpec((1, S // tv, 1, A_V_ROWS, tv), lambda b, h, i: (b, 0, h, 0, 0)),
                  pl.BlockSpec(lq.shape, lambda b, h, i: (0, 0)),
                  pl.BlockSpec(ag.shape, lambda b, h, i: (0, 0))],
        out_specs=pl.BlockSpec((nt * tq, A_V_DIM), lambda b, h, i: (b * nq + i, h)),
        out_shape=jax.ShapeDtypeStruct((B * S, A_WIDTH), F32),
        scratch_shapes=[pltpu.VMEM((2 * tq, LANES), BF16),
                        pltpu.VMEM((tk, 2 * tq), F32),
                        pltpu.VMEM((tk, 2 * tq), F32),
                        pltpu.VMEM((1, 2 * tq), F32),
                        pltpu.VMEM((A_V_ROWS, 2 * tq), F32)],
        compiler_params=pltpu.CompilerParams(dimension_semantics=("arbitrary", "arbitrary", "arbitrary"),
                                             vmem_limit_bytes=VMEM_LIMIT),
        name="attn",
    )(aq, ak, avt, lq, ag)


def _epilogue_body(x_ref, hf_ref, hb_ref, o_ref, preg_ref, w4_ref, mng_ref, bm_ref, wbm_ref, wba_ref,
                   wout_ref, postg_ref, out_ref):
    x = x_ref[...]
    xn = (_rms(x) * preg_ref[...]).astype(BF16)

    def proj(lo, hi):
        return jnp.dot(xn, w4_ref[:, lo:hi], preferred_element_type=F32)

    h = hf_ref[...] + hb_ref[...]
    mng = mng_ref[...]
    hn = jnp.concatenate(
        [_rms(h[:, k * M_V_DIM:(k + 1) * M_V_DIM]) * mng[:, k * M_V_DIM:(k + 1) * M_V_DIM]
         for k in range(M_HEADS)], axis=1)
    h_a = jax.nn.sigmoid(proj(0, M_WIDTH)) * hn * _silu(proj(M_WIDTH, 2 * M_WIDTH))
    ya = jnp.dot(h_a.astype(BF16), wbm_ref[...], preferred_element_type=F32)
    h_b = o_ref[...] * _silu(proj(2 * M_WIDTH, 2 * M_WIDTH + A_WIDTH))
    yb = jnp.dot(h_b.astype(BF16), wba_ref[...], preferred_element_type=F32)
    g0 = 2 * M_WIDTH + A_WIDTH
    bm = bm_ref[...]
    g_a = jax.nn.sigmoid(proj(g0, g0 + D_MODEL) + bm[:, 0:D_MODEL])
    g_b = jax.nn.sigmoid(proj(g0 + D_MODEL, g0 + 2 * D_MODEL) + bm[:, D_MODEL:2 * D_MODEL])
    y = g_a * ya + g_b * yb
    out = jnp.dot(y.astype(BF16), wout_ref[...], preferred_element_type=F32)
    out_ref[...] = x + _rms(out) * postg_ref[...]


def _epilogue_call(x2d, hf, hb, o, preg, w4, mng, bm, wbm, wba, wout, postg, te):
    T = x2d.shape[0]
    row = lambda i: (i, 0)
    const = lambda i: (0, 0)
    single = pl.Buffered(1)
    return pl.pallas_call(
        _epilogue_body,
        grid=(T // te,),
        in_specs=[pl.BlockSpec((te, D_MODEL), row),
                  pl.BlockSpec((te, M_WIDTH), row),
                  pl.BlockSpec((te, M_WIDTH), row),
                  pl.BlockSpec((te, A_WIDTH), row),
                  pl.BlockSpec(preg.shape, const),
                  pl.BlockSpec(w4.shape, const, pipeline_mode=single),
                  pl.BlockSpec(mng.shape, const),
                  pl.BlockSpec(bm.shape, const),
                  pl.BlockSpec(wbm.shape, const, pipeline_mode=single),
                  pl.BlockSpec(wba.shape, const, pipeline_mode=single),
                  pl.BlockSpec(wout.shape, const, pipeline_mode=single),
                  pl.BlockSpec(postg.shape, const)],
        out_specs=pl.BlockSpec((te, D_MODEL), row),
        out_shape=jax.ShapeDtypeStruct((T, D_MODEL), F32),
        compiler_params=pltpu.CompilerParams(dimension_semantics=("arbitrary",),
                                             vmem_limit_bytes=VMEM_LIMIT),
        name="epilogue",
    )(x2d, hf, hb, o, preg, w4, mng, bm, wbm, wba, wout, postg)


def _rope_tables(S):
    f32 = np.float32
    inv = ROPE_THETA ** (-np.arange(0, ROPE_DIMS, 2, dtype=np.float64) / ROPE_DIMS)
    ang = np.arange(S, dtype=np.float64)[:, None] * inv[None, :]
    cos, sin = np.cos(ang).astype(f32), np.sin(ang).astype(f32)
    one = np.ones((S, A_HEAD_DIM - ROPE_DIMS), f32)
    zero = np.zeros((S, A_HEAD_DIM - ROPE_DIMS), f32)
    rc = np.concatenate([cos, cos, one], axis=1)
    rs = np.concatenate([-sin, sin, zero], axis=1)
    rep = LANES // A_HEAD_DIM
    return jnp.asarray(np.tile(rc, (1, rep))), jnp.asarray(np.tile(rs, (1, rep)))


def _gate_layout(m):
    def groups(fw, bw):
        g = jnp.stack([fw, bw], axis=-1)
        g = jnp.concatenate([g, jnp.zeros(g.shape[:-1] + (6,), g.dtype)], axis=-1)
        return g.reshape(g.shape[:-2] + (8 * M_HEADS,))
    return jnp.concatenate([groups(m[..., 1, :], m[..., 3, :]), groups(m[..., 0, :], m[..., 2, :])],
                           axis=-1)


def _layer(x, layer_idx, pre_g, w_in, conv_w, conv_b, gate_b, m_norm_g, lambda_qk, a_norm_g,
           w_branch_m, w_branch_a, b_merge, w_out, post_g):
    B, S, _ = x.shape
    T = B * S
    tm = min(512, S)
    tq = min(1024, S // 2)
    nt = min(8, S // tq)
    tk = min(1024, S // 2)
    tr = min(512, S)
    te = min(512, T)
    lam_init = 0.8 - 0.6 * math.exp(-0.3 * layer_idx)

    cuts = [0]
    for sz in SPLIT_SIZES:
        cuts.append(cuts[-1] + sz)
    col = lambda k: w_in[:, cuts[k]:cuts[k + 1]]
    wmqk = col(0).astype(BF16)
    w1 = jnp.concatenate([col(1), col(5), col(6)], axis=1).astype(BF16)
    wavt = col(7).T.astype(BF16)
    wgt = _gate_layout(col(2).reshape(D_MODEL, 4, M_HEADS)).T.astype(BF16)
    gb = _gate_layout(gate_b).reshape(N_GATE_ROWS, 1).astype(F32)
    w4 = jnp.concatenate([col(3), col(4), col(8), col(9)], axis=1).astype(BF16)
    rc, rs = _rope_tables(S)

    x2d = x.reshape(T, D_MODEL)
    preg = pre_g.reshape(1, D_MODEL)
    mq, mk, mv, gt, aq, ak, avt = _inproj_call(
        x2d, preg, wmqk, w1, wavt, wgt, conv_w, conv_b.reshape(1, 2 * M_QK), rc, rs, B, S, tm)
    rows, cols = _gates_call(gt, gb, B, S)
    hf, hb = _mlstm_call(mq, mk, mv, rows, cols, B, S, tr)
    o = _attn_call(aq, ak, avt, lambda_qk.astype(F32), a_norm_g.reshape(1, A_V_DIM), B, S, tq, nt, tk, tm,
                   lam_init)
    out = _epilogue_call(
        x2d, hf.reshape(T, M_WIDTH), hb.reshape(T, M_WIDTH), o, preg, w4,
        m_norm_g.reshape(1, M_WIDTH), b_merge.reshape(1, 2 * D_MODEL),
        w_branch_m.astype(BF16), w_branch_a.astype(BF16), w_out.astype(BF16),
        post_g.reshape(1, D_MODEL), te)
    return out.reshape(B, S, D_MODEL)


def kernel(x, pre_norm_g, w_in, conv_w, conv_b, gate_b, m_norm_g, lambda_qk, a_norm_g, w_branch_m,
           w_branch_a, b_merge, w_out, post_norm_g):
    for l in range(pre_norm_g.shape[0]):
        x = _layer(x, l, pre_norm_g[l], w_in[l], conv_w[l], conv_b[l], gate_b[l], m_norm_g[l],
                   lambda_qk[l], a_norm_g[l], w_branch_m[l], w_branch_a[l], b_merge[l], w_out[l],
                   post_norm_g[l])
    return x
```

```python
import functools
import math

import jax
import jax.numpy as jnp
import numpy as np
from jax import lax
from jax.experimental import pallas as pl
from jax.experimental.pallas import tpu as pltpu

D_MODEL = 1024
M_HEADS = 4
M_QK_DIM = 128
M_V_DIM = 256
M_QK = M_HEADS * M_QK_DIM
M_WIDTH = M_HEADS * M_V_DIM
M_CHUNK = 64
CONV_WIDTH = 4
A_HEADS = 8
A_HEAD_DIM = 64
A_V_DIM = 2 * A_HEAD_DIM
A_QK = A_HEADS * 2 * A_HEAD_DIM
A_WIDTH = A_HEADS * A_V_DIM
ROPE_THETA = 500000.0
ROPE_DIMS = A_HEAD_DIM // 4
EPS = 1e-6
SPLIT_SIZES = (2 * M_QK, M_WIDTH, 4 * M_HEADS, M_WIDTH, M_WIDTH, A_QK, A_QK, A_WIDTH, A_WIDTH, 2 * D_MODEL)

LANES = 128
MXU_N = 256
BF16_ROWS = 16
A_V_ROWS = A_V_DIM + BF16_ROWS
HALO = 16
N_GATE_ROWS = 2 * 8 * M_HEADS
VMEM_LIMIT = 56 * 1024 * 1024

F32 = jnp.float32
BF16 = jnp.bfloat16
_NT = (((1,), (1,)), ((), ()))
_TN = (((0,), (0,)), ((), ()))


def _rms(xf):
    return xf * lax.rsqrt(jnp.mean(xf * xf, axis=-1, keepdims=True) + EPS)


def _silu(y):
    return y * jax.nn.sigmoid(y)


def _inproj_body(tps, tm, xprev_ref, x_ref, xnext_ref, preg_ref, wmqk_ref, w1_ref, wavt_ref, wgt_ref,
                 convw_ref, convb_ref, rc_ref, rs_ref,
                 mq_ref, mk_ref, mv_ref, gt_ref, aq_ref, ak_ref, avt_ref, xext_ref):
    t = lax.rem(pl.program_id(0), tps)
    g = preg_ref[...]
    xn = (_rms(x_ref[...]) * g).astype(BF16)
    xp = jnp.where(t != 0, _rms(xprev_ref[...]) * g, 0.0).astype(BF16)
    xq = jnp.where(t != tps - 1, _rms(xnext_ref[...]) * g, 0.0).astype(BF16)
    xext_ref[0:HALO, :] = xp
    xext_ref[HALO:HALO + tm, :] = xn
    xext_ref[HALO + tm:, :] = xq
    n_ext = tm + 2 * HALO
    ext = jnp.dot(xext_ref[...], wmqk_ref[...], preferred_element_type=F32)
    mv = jnp.dot(xn, w1_ref[:, 0:M_WIDTH], preferred_element_type=F32)
    aq = jnp.dot(xn, w1_ref[:, M_WIDTH:M_WIDTH + A_QK], preferred_element_type=F32)
    ak = jnp.dot(xn, w1_ref[:, M_WIDTH + A_QK:M_WIDTH + 2 * A_QK], preferred_element_type=F32)
    avt = lax.dot_general(wavt_ref[...], xn, _NT, preferred_element_type=F32)
    gt_ref[...] = lax.dot_general(wgt_ref[...], xn, _NT, preferred_element_type=F32)

    left = CONV_WIDTH // 2
    y = None
    for j in range(CONV_WIDTH):
        sh = (left - j) % n_ext
        r = ext if sh == 0 else pltpu.roll(ext, sh, axis=0)
        term = convw_ref[j:j + 1, :] * r[HALO:HALO + tm, :]
        y = term if y is None else y + term
    y = _silu(y + convb_ref[...])
    qscale = M_QK_DIM ** -0.5
    for h in range(M_HEADS):
        mq_ref[0, h] = (y[:, h * M_QK_DIM:(h + 1) * M_QK_DIM] * qscale).astype(BF16)
        mk_ref[0, h] = y[:, M_QK + h * M_QK_DIM:M_QK + (h + 1) * M_QK_DIM].astype(BF16)
    for h in range(M_HEADS):
        mv_ref[0, h] = mv[:, h * M_V_DIM:(h + 1) * M_V_DIM].astype(BF16)
    rc, rs = rc_ref[...], rs_ref[...]
    half = ROPE_DIMS // 2
    first_half = (lax.broadcasted_iota(jnp.int32, rc.shape, 1) & (A_HEAD_DIM - 1)) < half

    def rope(a):
        partner = jnp.where(first_half, pltpu.roll(a, LANES - half, axis=1), pltpu.roll(a, half, axis=1))
        return a * rc + partner * rs

    ascale = A_HEAD_DIM ** -0.5 * math.log2(math.e)
    for h in range(A_HEADS):
        aq_ref[0, h] = (rope(aq[:, h * LANES:(h + 1) * LANES]) * ascale).astype(BF16)
        ak_ref[0, h] = rope(ak[:, h * LANES:(h + 1) * LANES]).astype(BF16)
    ones_tile = (lax.broadcasted_iota(jnp.int32, (BF16_ROWS, tm), 0) == 0).astype(BF16)
    for h in range(A_HEADS):
        avt_ref[0, 0, h, 0:A_V_DIM, :] = avt[h * A_V_DIM:(h + 1) * A_V_DIM, :].astype(BF16)
        avt_ref[0, 0, h, A_V_DIM:A_V_ROWS, :] = ones_tile


def _inproj_call(x2d, preg, wmqk, w1, wavt, wgt, convw, convb, rc, rs, B, S, tm):
    T = B * S
    tps = S // tm
    nh = tm // HALO
    const = lambda i: (0, 0)
    in_specs = [
        pl.BlockSpec((HALO, D_MODEL), lambda i: (jnp.maximum(i * nh - 1, 0), 0)),
        pl.BlockSpec((tm, D_MODEL), lambda i: (i, 0)),
        pl.BlockSpec((HALO, D_MODEL), lambda i: (jnp.minimum((i + 1) * nh, T // HALO - 1), 0)),
        pl.BlockSpec((1, D_MODEL), const),
        pl.BlockSpec(wmqk.shape, const),
        pl.BlockSpec(w1.shape, const),
        pl.BlockSpec(wavt.shape, const),
        pl.BlockSpec(wgt.shape, const),
        pl.BlockSpec(convw.shape, const),
        pl.BlockSpec(convb.shape, const),
        pl.BlockSpec((tm, LANES), lambda i: (lax.rem(i, tps), 0)),
        pl.BlockSpec((tm, LANES), lambda i: (lax.rem(i, tps), 0)),
    ]
    bt = lambda i: (i // tps, 0, lax.rem(i, tps), 0)
    out_shape = [
        jax.ShapeDtypeStruct((B, M_HEADS, S, M_QK_DIM), BF16),
        jax.ShapeDtypeStruct((B, M_HEADS, S, M_QK_DIM), BF16),
        jax.ShapeDtypeStruct((B, M_HEADS, S, M_V_DIM), BF16),
        jax.ShapeDtypeStruct((N_GATE_ROWS, T), F32),
        jax.ShapeDtypeStruct((B, A_HEADS, S, LANES), BF16),
        jax.ShapeDtypeStruct((B, A_HEADS, S, LANES), BF16),
        jax.ShapeDtypeStruct((B, tps, A_HEADS, A_V_ROWS, tm), BF16),
    ]
    out_specs = [
        pl.BlockSpec((1, M_HEADS, tm, M_QK_DIM), bt),
        pl.BlockSpec((1, M_HEADS, tm, M_QK_DIM), bt),
        pl.BlockSpec((1, M_HEADS, tm, M_V_DIM), bt),
        pl.BlockSpec((N_GATE_ROWS, tm), lambda i: (0, i)),
        pl.BlockSpec((1, A_HEADS, tm, LANES), bt),
        pl.BlockSpec((1, A_HEADS, tm, LANES), bt),
        pl.BlockSpec((1, 1, A_HEADS, A_V_ROWS, tm), lambda i: (i // tps, lax.rem(i, tps), 0, 0, 0)),
    ]
    return pl.pallas_call(
        functools.partial(_inproj_body, tps, tm),
        grid=(T // tm,),
        in_specs=in_specs,
        out_specs=out_specs,
        out_shape=out_shape,
        scratch_shapes=[pltpu.VMEM((tm + 2 * HALO, D_MODEL), BF16)],
        compiler_params=pltpu.CompilerParams(dimension_semantics=("arbitrary",),
                                             vmem_limit_bytes=VMEM_LIMIT),
        name="inproj",
    )(x2d, x2d, x2d, preg, wmqk, w1, wavt, wgt, convw, convb, rc, rs)


N_GATE_COLS = 5


def _gates_body(S, ct, f_ref, i_ref, fb_ref, ib_ref, rows_ref, cols_ref):
    L = M_CHUNK
    f = f_ref[...] + fb_ref[...]
    ig = i_ref[...] + ib_ref[...]
    lf = jnp.minimum(f, 0.0) - jnp.log1p(jnp.exp(-jnp.abs(f)))
    lane = lax.broadcasted_iota(jnp.int32, f.shape, 1)
    pos = lane & (L - 1)
    fwd = lax.broadcasted_iota(jnp.int32, f.shape, 0) == 0

    def from_left(x, s):
        return pltpu.roll(x, s, axis=1)

    def from_right(x, s):
        return pltpu.roll(x, S - s, axis=1)

    def chunk_scans(x, op, ident):
        pre, suf = x, x
        s = 1
        while s < L:
            pre = op(pre, jnp.where(pos >= s, from_left(pre, s), ident))
            suf = op(suf, jnp.where(pos < L - s, from_right(suf, s), ident))
            s *= 2
        return pre, suf

    pre, suf = chunk_scans(lf, jnp.add, 0.0)
    a = jnp.where(fwd, pre, suf)
    a_tot = pre + suf - lf
    w = ig - a
    wpre, wsuf = chunk_scans(w, jnp.maximum, -jnp.inf)
    rowmax = a + jnp.where(fwd, wpre, wsuf)
    g_max = a_tot + jnp.maximum(wpre, wsuf)

    af, gf, ab, gb = a_tot, g_max, a_tot, g_max
    s = L
    while s < S:
        ok = lane >= s
        a_sh, g_sh = from_left(af, s), from_left(gf, s)
        gf = jnp.where(ok, jnp.maximum(g_sh + af, gf), gf)
        af = jnp.where(ok, af + a_sh, af)
        ok = lane < S - s
        a_sh, g_sh = from_right(ab, s), from_right(gb, s)
        gb = jnp.where(ok, jnp.maximum(g_sh + ab, gb), gb)
        ab = jnp.where(ok, ab + a_sh, ab)
        s *= 2
    m_f = jnp.maximum(af, gf)
    m_b = jnp.maximum(ab, gb)
    m_prev = jnp.where(fwd, jnp.where(lane >= L, from_left(m_f, L), 0.0),
                       jnp.where(lane < S - L, from_right(m_b, L), 0.0))

    inter = a + m_prev
    mj = jnp.maximum(inter, rowmax)
    m_new = jnp.maximum(a_tot + m_prev, g_max)
    cols = (a - mj, jnp.exp(inter - mj), jnp.exp(-mj), jnp.exp(a_tot + w - m_new),
            jnp.exp(a_tot + m_prev - m_new))
    rows_ref[0, 0] = w
    pad = jnp.zeros((LANES - 8 * N_GATE_COLS, ct), F32)
    for t in range(S // ct):
        sl = slice(t * ct, (t + 1) * ct)
        cols_ref[0, 0, sl, :] = jnp.concatenate([c[:, sl] for c in cols] + [pad], axis=0).T


def _gates_call(gt, gb, B, S):
    ct = min(2048, S)
    return pl.pallas_call(
        functools.partial(_gates_body, S, ct),
        grid=(B, M_HEADS),
        in_specs=[pl.BlockSpec((8, S), lambda b, h: (h, b)),
                  pl.BlockSpec((8, S), lambda b, h: (M_HEADS + h, b)),
                  pl.BlockSpec((8, 1), lambda b, h: (h, 0)),
                  pl.BlockSpec((8, 1), lambda b, h: (M_HEADS + h, 0))],
        out_specs=[pl.BlockSpec((1, 1, 8, S), lambda b, h: (b, h, 0, 0)),
                   pl.BlockSpec((1, 1, S, LANES), lambda b, h: (b, h, 0, 0))],
        out_shape=[jax.ShapeDtypeStruct((B, M_HEADS, 8, S), F32),
                   jax.ShapeDtypeStruct((B, M_HEADS, S, LANES), F32)],
        compiler_params=pltpu.CompilerParams(dimension_semantics=("arbitrary", "arbitrary"),
                                             vmem_limit_bytes=VMEM_LIMIT),
        name="gates",
    )(gt, gt, gb, gb)


class _MlstmScan:
    def __init__(self, rev, nchunk, q_ref, k_ref, v_ref, rows_ref, cols_ref, h_ref, c_ref):
        self.rev, self.d = rev, (1 if rev else 0)
        self.q_ref, self.k_ref, self.v_ref = q_ref, k_ref, v_ref
        self.rows_ref, self.cols_ref, self.h_ref, self.c_ref = rows_ref, cols_ref, h_ref, c_ref
        self.order = list(range(nchunk - 1, -1, -1) if rev else range(nchunk))
        L = M_CHUNK
        row_i = lax.broadcasted_iota(jnp.int32, (L, L), 0)
        col_i = lax.broadcasted_iota(jnp.int32, (L, L), 1)
        self.visible = (col_i >= row_i) if rev else (col_i <= row_i)
        self.ones_col = (lax.broadcasted_iota(jnp.int32, (L, LANES), 1) == 0).astype(BF16)

    @staticmethod
    def rows_of(c):
        return slice(c * M_CHUNK, (c + 1) * M_CHUNK)

    def gate_col(self, c, k):
        return self.cols_ref[0, 0, self.rows_of(c), 8 * k + self.d:8 * k + self.d + 1]

    def v_ext(self, c):
        return jnp.concatenate([self.v_ref[0, 0, self.rows_of(c), :], self.ones_col], axis=1)

    def input_products(self):
        self.qk, self.kv = {}, {}
        for c in self.order:
            q, k = self.q_ref[0, 0, self.rows_of(c), :], self.k_ref[0, 0, self.rows_of(c), :]
            self.qk[c] = lax.dot_general(q, k, _NT, preferred_element_type=F32)
            wv = (self.gate_col(c, 3) * self.v_ext(c).astype(F32)).astype(BF16)
            self.kv[c] = lax.dot_general(k, wv, _TN, preferred_element_type=F32)

    def states(self):
        C = self.c_ref[...]
        self.c_in = {}
        for c in self.order:
            self.c_in[c] = C.astype(BF16)
            C = self.gate_col(c, 4)[0:1, :] * C + self.kv[c]
        self.c_ref[...] = C

    def read_out(self):
        for c in self.order:
            u, w_inter, e = self.gate_col(c, 0), self.gate_col(c, 1), self.gate_col(c, 2)
            w_row = self.rows_ref[0, 0, self.d:self.d + 1, self.rows_of(c)]
            p = jnp.exp(jnp.where(self.visible, u + w_row, -jnp.inf))
            s = self.qk[c] * p
            q = self.q_ref[0, 0, self.rows_of(c), :]
            num = (w_inter * jnp.dot(q, self.c_in[c], preferred_element_type=F32)
                   + jnp.dot(s.astype(BF16), self.v_ext(c), preferred_element_type=F32))
            den = num[:, M_V_DIM:M_V_DIM + 1]
            self.h_ref[0, self.rows_of(c), :] = num[:, 0:M_V_DIM] / jnp.maximum(jnp.abs(den), e)


def _mlstm_body(nchunk, qf_ref, kf_ref, vf_ref, rowsf_ref, colsf_ref, qb_ref, kb_ref, vb_ref, rowsb_ref,
                colsb_ref, hf_ref, hb_ref, cf_ref, cb_ref):
    @pl.when(pl.program_id(2) == 0)
    def _():
        cf_ref[...] = jnp.zeros_like(cf_ref)
        cb_ref[...] = jnp.zeros_like(cb_ref)

    fwd = _MlstmScan(False, nchunk, qf_ref, kf_ref, vf_ref, rowsf_ref, colsf_ref, hf_ref, cf_ref)
    bwd = _MlstmScan(True, nchunk, qb_ref, kb_ref, vb_ref, rowsb_ref, colsb_ref, hb_ref, cb_ref)
    fwd.input_products()
    bwd.input_products()
    fwd.states()
    bwd.states()
    fwd.read_out()
    bwd.read_out()


def _mlstm_call(mq, mk, mv, rows, cols, B, S, tr):
    ns = S // tr

    def specs(pos):
        return [pl.BlockSpec((1, 1, tr, M_QK_DIM), lambda b, h, s: (b, h, pos(s), 0)),
                pl.BlockSpec((1, 1, tr, M_QK_DIM), lambda b, h, s: (b, h, pos(s), 0)),
                pl.BlockSpec((1, 1, tr, M_V_DIM), lambda b, h, s: (b, h, pos(s), 0)),
                pl.BlockSpec((1, 1, 8, tr), lambda b, h, s: (b, h, 0, pos(s))),
                pl.BlockSpec((1, 1, tr, LANES), lambda b, h, s: (b, h, pos(s), 0))]

    fpos = lambda s: s
    bpos = lambda s: ns - 1 - s
    state = pltpu.VMEM((M_QK_DIM, M_V_DIM + LANES), F32)
    return pl.pallas_call(
        functools.partial(_mlstm_body, tr // M_CHUNK),
        grid=(B, M_HEADS, ns),
        in_specs=specs(fpos) + specs(bpos),
        out_specs=[pl.BlockSpec((1, tr, M_V_DIM), lambda b, h, s: (b, fpos(s), h)),
                   pl.BlockSpec((1, tr, M_V_DIM), lambda b, h, s: (b, bpos(s), h))],
        out_shape=[jax.ShapeDtypeStruct((B, S, M_WIDTH), F32),
                   jax.ShapeDtypeStruct((B, S, M_WIDTH), F32)],
        scratch_shapes=[state, state],
        compiler_params=pltpu.CompilerParams(dimension_semantics=("arbitrary", "arbitrary", "arbitrary"),
                                             vmem_limit_bytes=VMEM_LIMIT),
        name="mlstm",
    )(mq, mk, mv, rows, cols, mq, mk, mv, rows, cols)


def _attn_body(nkc, nt, tq, tk, tv, lam_init, q_ref, k_ref, vt_ref, lq_ref, g_ref, o_ref,
               qz_ref, st0_ref, st1_ref, m_ref, acc_ref):
    def tile_rows(t):
        return pl.ds(pl.multiple_of(t * tq, tq), tq)

    def load_queries(t):
        q = q_ref[0, 0, tile_rows(t), :]
        lane = lax.broadcasted_iota(jnp.int32, q.shape, 1)
        zero = jnp.zeros_like(q)
        qz_ref[0:tq, :] = jnp.where(lane < A_HEAD_DIM, q, zero)
        qz_ref[tq:2 * tq, :] = jnp.where(lane >= A_HEAD_DIM, q, zero)

    def reset_stats():
        m_ref[...] = jnp.full(m_ref.shape, -jnp.inf, F32)
        acc_ref[...] = jnp.zeros_like(acc_ref)

    strips = [slice(c * MXU_N, (c + 1) * MXU_N) for c in range(2 * tq // MXU_N)]
    nv = tk // tv

    def score_strip(j, st_ref, sl):
        kj = k_ref[0, 0, pl.ds(pl.multiple_of(j * tk, tk), tk), :]
        st = lax.dot_general(kj, qz_ref[sl, :], _NT, preferred_element_type=F32)
        st_ref[:, sl] = st
        return jnp.max(st, axis=0, keepdims=True)

    def accumulate_strip(j, st_ref, sl, cm):
        m_old = m_ref[:, sl]
        m_new = jnp.maximum(m_old, cm)
        alpha = jnp.exp2(m_old - m_new)
        p = jnp.exp2(st_ref[:, sl] - m_new).astype(BF16)
        pv = jnp.dot(vt_ref[0, j * nv, 0], p[0:tv], preferred_element_type=F32)
        for u in range(1, nv):
            pv = pv + jnp.dot(vt_ref[0, j * nv + u, 0], p[u * tv:(u + 1) * tv],
                              preferred_element_type=F32)
        acc_ref[:, sl] = alpha * acc_ref[:, sl] + pv
        m_ref[:, sl] = m_new

    def step(j_next, st_next_ref, j_cur, st_cur_ref, cmax_cur):
        cmax_next = []
        for sl, cm in zip(strips, cmax_cur):
            cmax_next.append(score_strip(j_next, st_next_ref, sl))
            accumulate_strip(j_cur, st_cur_ref, sl, cm)
        return tuple(cmax_next)

    def pair(i, cmax0):
        j = 2 * i
        cmax1 = step(j + 1, st1_ref, j, st0_ref, cmax0)
        return step(j + 2, st0_ref, j + 1, st1_ref, cmax1)

    def two_pairs(i, cmax0):
        return pair(2 * i + 1, pair(2 * i, cmax0))

    def all_but_last_chunk(cmax0):
        npairs = nkc // 2 - 1
        cmax0 = lax.fori_loop(0, npairs // 2, two_pairs, cmax0)
        if npairs % 2:
            cmax0 = pair(npairs - 1, cmax0)
        return step(nkc - 1, st1_ref, nkc - 2, st0_ref, cmax0)

    lq = lq_ref[...]
    lam = (jnp.exp(jnp.sum(lq[0:1] * lq[1:2], axis=1, keepdims=True))
           - jnp.exp(jnp.sum(lq[2:3] * lq[3:4], axis=1, keepdims=True)) + lam_init)

    def finalize(t):
        accn = acc_ref[0:A_V_DIM, :] / acc_ref[A_V_DIM:A_V_DIM + 1, :]
        ot = accn[:, 0:tq] - lam * accn[:, tq:2 * tq]
        otn = ot * lax.rsqrt(jnp.mean(ot * ot, axis=0, keepdims=True) + EPS)
        o_ref[tile_rows(t), :] = (otn.T * g_ref[...]) * (1.0 - lam_init)

    def tile(t, cmax0):
        cmax1 = all_but_last_chunk(cmax0)
        load_queries(t + 1)
        cmax_next = []
        for sl, cm in zip(strips, cmax1):
            cmax_next.append(score_strip(0, st0_ref, sl))
            accumulate_strip(nkc - 1, st1_ref, sl, cm)
        finalize(t)
        reset_stats()
        return tuple(cmax_next)

    load_queries(0)
    reset_stats()
    cmax0 = tuple(score_strip(0, st0_ref, sl) for sl in strips)
    cmax0 = lax.fori_loop(0, nt - 1, tile, cmax0)
    cmax1 = all_but_last_chunk(cmax0)
    for sl, cm in zip(strips, cmax1):
        accumulate_strip(nkc - 1, st1_ref, sl, cm)
    finalize(nt - 1)


def _attn_call(aq, ak, avt, lq, ag, B, S, tq, nt, tk, tv, lam_init):
    nkc = S // tk
    nq = S // (nt * tq)
    assert nkc >= 2 and nkc % 2 == 0, "key chunks are processed in pairs"
    assert tk % tv == 0 and S % (nt * tq) == 0
    return pl.pallas_call(
        functools.partial(_attn_body, nkc, nt, tq, tk, tv, lam_init),
        grid=(B, A_HEADS, nq),
        in_specs=[pl.BlockSpec((1, 1, nt * tq, LANES), lambda b, h, i: (b, h, i, 0)),
                  pl.BlockSpec((1, 1, S, LANES), lambda b, h, i: (b, h, 0, 0)),
                  pl.BlockSpec((1, S // tv, 1, A_V_ROWS, tv), lambda b, h, i: (b, 0, h, 0, 0)),
                  pl.BlockSpec(lq.shape, lambda b, h, i: (0, 0)),
                  pl.BlockSpec(ag.shape, lambda b, h, i: (0, 0))],
        out_specs=pl.BlockSpec((nt * tq, A_V_DIM), lambda b, h, i: (b * nq + i, h)),
        out_shape=jax.ShapeDtypeStruct((B * S, A_WIDTH), F32),
        scratch_shapes=[pltpu.VMEM((2 * tq, LANES), BF16),
                        pltpu.VMEM((tk, 2 * tq), F32),
                        pltpu.VMEM((tk, 2 * tq), F32),
                        pltpu.VMEM((1, 2 * tq), F32),
                        pltpu.VMEM((A_V_ROWS, 2 * tq), F32)],
        compiler_params=pltpu.CompilerParams(dimension_semantics=("arbitrary", "arbitrary", "arbitrary"),
                                             vmem_limit_bytes=VMEM_LIMIT),
        name="attn",
    )(aq, ak, avt, lq, ag)


def _epilogue_body(x_ref, hf_ref, hb_ref, o_ref, preg_ref, w4_ref, mng_ref, bm_ref, wbm_ref, wba_ref,
                   wout_ref, postg_ref, out_ref):
    x = x_ref[...]
    xn = (_rms(x) * preg_ref[...]).astype(BF16)

    def proj(lo, hi):
        return jnp.dot(xn, w4_ref[:, lo:hi], preferred_element_type=F32)

    h = hf_ref[...] + hb_ref[...]
    mng = mng_ref[...]
    hn = jnp.concatenate(
        [_rms(h[:, k * M_V_DIM:(k + 1) * M_V_DIM]) * mng[:, k * M_V_DIM:(k + 1) * M_V_DIM]
         for k in range(M_HEADS)], axis=1)
    h_a = jax.nn.sigmoid(proj(0, M_WIDTH)) * hn * _silu(proj(M_WIDTH, 2 * M_WIDTH))
    ya = jnp.dot(h_a.astype(BF16), wbm_ref[...], preferred_element_type=F32)
    h_b = o_ref[...] * _silu(proj(2 * M_WIDTH, 2 * M_WIDTH + A_WIDTH))
    yb = jnp.dot(h_b.astype(BF16), wba_ref[...], preferred_element_type=F32)
    g0 = 2 * M_WIDTH + A_WIDTH
    bm = bm_ref[...]
    g_a = jax.nn.sigmoid(proj(g0, g0 + D_MODEL) + bm[:, 0:D_MODEL])
    g_b = jax.nn.sigmoid(proj(g0 + D_MODEL, g0 + 2 * D_MODEL) + bm[:, D_MODEL:2 * D_MODEL])
    y = g_a * ya + g_b * yb
    out = jnp.dot(y.astype(BF16), wout_ref[...], preferred_element_type=F32)
    out_ref[...] = x + _rms(out) * postg_ref[...]


def _epilogue_call(x2d, hf, hb, o, preg, w4, mng, bm, wbm, wba, wout, postg, te):
    T = x2d.shape[0]
    row = lambda i: (i, 0)
    const = lambda i: (0, 0)
    single = pl.Buffered(1)
    return pl.pallas_call(
        _epilogue_body,
        grid=(T // te,),
        in_specs=[pl.BlockSpec((te, D_MODEL), row),
                  pl.BlockSpec((te, M_WIDTH), row),
                  pl.BlockSpec((te, M_WIDTH), row),
                  pl.BlockSpec((te, A_WIDTH), row),
                  pl.BlockSpec(preg.shape, const),
                  pl.BlockSpec(w4.shape, const, pipeline_mode=single),
                  pl.BlockSpec(mng.shape, const),
                  pl.BlockSpec(bm.shape, const),
                  pl.BlockSpec(wbm.shape, const, pipeline_mode=single),
                  pl.BlockSpec(wba.shape, const, pipeline_mode=single),
                  pl.BlockSpec(wout.shape, const, pipeline_mode=single),
                  pl.BlockSpec(postg.shape, const)],
        out_specs=pl.BlockSpec((te, D_MODEL), row),
        out_shape=jax.ShapeDtypeStruct((T, D_MODEL), F32),
        compiler_params=pltpu.CompilerParams(dimension_semantics=("arbitrary",),
                                             vmem_limit_bytes=VMEM_LIMIT),
        name="epilogue",
    )(x2d, hf, hb, o, preg, w4, mng, bm, wbm, wba, wout, postg)


def _rope_tables(S):
    f32 = np.float32
    inv = ROPE_THETA ** (-np.arange(0, ROPE_DIMS, 2, dtype=np.float64) / ROPE_DIMS)
    ang = np.arange(S, dtype=np.float64)[:, None] * inv[None, :]
    cos, sin = np.cos(ang).astype(f32), np.sin(ang).astype(f32)
    one = np.ones((S, A_HEAD_DIM - ROPE_DIMS), f32)
    zero = np.zeros((S, A_HEAD_DIM - ROPE_DIMS), f32)
    rc = np.concatenate([cos, cos, one], axis=1)
    rs = np.concatenate([-sin, sin, zero], axis=1)
    rep = LANES // A_HEAD_DIM
    return jnp.asarray(np.tile(rc, (1, rep))), jnp.asarray(np.tile(rs, (1, rep)))


def _gate_layout(m):
    def groups(fw, bw):
        g = jnp.stack([fw, bw], axis=-1)
        g = jnp.concatenate([g, jnp.zeros(g.shape[:-1] + (6,), g.dtype)], axis=-1)
        return g.reshape(g.shape[:-2] + (8 * M_HEADS,))
    return jnp.concatenate([groups(m[..., 1, :], m[..., 3, :]), groups(m[..., 0, :], m[..., 2, :])],
                           axis=-1)


def _layer(x, layer_idx, pre_g, w_in, conv_w, conv_b, gate_b, m_norm_g, lambda_qk, a_norm_g,
           w_branch_m, w_branch_a, b_merge, w_out, post_g):
    B, S, _ = x.shape
    T = B * S
    tm = min(512, S)
    tq = min(2048, S // 2)
    nt = min(4, S // tq)
    tk = min(512, S // 2)
    tr = min(512, S)
    te = min(256, T)
    lam_init = 0.8 - 0.6 * math.exp(-0.3 * layer_idx)

    cuts = [0]
    for sz in SPLIT_SIZES:
        cuts.append(cuts[-1] + sz)
    col = lambda k: w_in[:, cuts[k]:cuts[k + 1]]
    wmqk = col(0).astype(BF16)
    w1 = jnp.concatenate([col(1), col(5), col(6)], axis=1).astype(BF16)
    wavt = col(7).T.astype(BF16)
    wgt = _gate_layout(col(2).reshape(D_MODEL, 4, M_HEADS)).T.astype(BF16)
    gb = _gate_layout(gate_b).reshape(N_GATE_ROWS, 1).astype(F32)
    w4 = jnp.concatenate([col(3), col(4), col(8), col(9)], axis=1).astype(BF16)
    rc, rs = _rope_tables(S)

    x2d = x.reshape(T, D_MODEL)
    preg = pre_g.reshape(1, D_MODEL)
    mq, mk, mv, gt, aq, ak, avt = _inproj_call(
        x2d, preg, wmqk, w1, wavt, wgt, conv_w, conv_b.reshape(1, 2 * M_QK), rc, rs, B, S, tm)
    rows, cols = _gates_call(gt, gb, B, S)
    hf, hb = _mlstm_call(mq, mk, mv, rows, cols, B, S, tr)
    o = _attn_call(aq, ak, avt, lambda_qk.astype(F32), a_norm_g.reshape(1, A_V_DIM), B, S, tq, nt, tk, tm,
                   lam_init)
    out = _epilogue_call(
        x2d, hf.reshape(T, M_WIDTH), hb.reshape(T, M_WIDTH), o, preg, w4,
        m_norm_g.reshape(1, M_WIDTH), b_merge.reshape(1, 2 * D_MODEL),
        w_branch_m.astype(BF16), w_branch_a.astype(BF16), w_out.astype(BF16),
        post_g.reshape(1, D_MODEL), te)
    return out.reshape(B, S, D_MODEL)


def kernel(x, pre_norm_g, w_in, conv_w, conv_b, gate_b, m_norm_g, lambda_qk, a_norm_g, w_branch_m,
           w_branch_a, b_merge, w_out, post_norm_g):
    for l in range(pre_norm_g.shape[0]):
        x = _layer(x, l, pre_norm_g[l], w_in[l], conv_w[l], conv_b[l], gate_b[l], m_norm_g[l],
                   lambda_qk[l], a_norm_g[l], w_branch_m[l], w_branch_a[l], b_merge[l], w_out[l],
                   post_norm_g[l])
    return x
```

```python
import functools
import math

import jax
import jax.numpy as jnp
import numpy as np
from jax import lax
from jax.experimental import pallas as pl
from jax.experimental.pallas import tpu as pltpu

D_MODEL = 1024
M_HEADS = 4
M_QK_DIM = 128
M_V_DIM = 256
M_QK = M_HEADS * M_QK_DIM
M_WIDTH = M_HEADS * M_V_DIM
M_CHUNK = 64
CONV_WIDTH = 4
A_HEADS = 8
A_HEAD_DIM = 64
A_V_DIM = 2 * A_HEAD_DIM
A_QK = A_HEADS * 2 * A_HEAD_DIM
A_WIDTH = A_HEADS * A_V_DIM
ROPE_THETA = 500000.0
ROPE_DIMS = A_HEAD_DIM // 4
EPS = 1e-6
SPLIT_SIZES = (2 * M_QK, M_WIDTH, 4 * M_HEADS, M_WIDTH, M_WIDTH, A_QK, A_QK, A_WIDTH, A_WIDTH, 2 * D_MODEL)

LANES = 128
MXU_N = 256
BF16_ROWS = 16
A_V_ROWS = A_V_DIM + BF16_ROWS
HALO = 16
N_GATE_ROWS = 2 * 2 * M_HEADS
VMEM_LIMIT = 56 * 1024 * 1024

F32 = jnp.float32
BF16 = jnp.bfloat16
_NT = (((1,), (1,)), ((), ()))
_TN = (((0,), (0,)), ((), ()))


def _rms(xf):
    return xf * lax.rsqrt(jnp.mean(xf * xf, axis=-1, keepdims=True) + EPS)


def _silu(y):
    return y * jax.nn.sigmoid(y)


def _inproj_body(tps, tm, xprev_ref, x_ref, xnext_ref, preg_ref, wmqk_ref, w1_ref, wavt_ref, wgt_ref,
                 convw_ref, convb_ref, rc_ref, rs_ref,
                 mq_ref, mk_ref, mv_ref, gt_ref, aq_ref, ak_ref, avt_ref, xext_ref):
    t = lax.rem(pl.program_id(0), tps)
    g = preg_ref[...]
    xn = (_rms(x_ref[...]) * g).astype(BF16)
    xp = jnp.where(t != 0, _rms(xprev_ref[...]) * g, 0.0).astype(BF16)
    xq = jnp.where(t != tps - 1, _rms(xnext_ref[...]) * g, 0.0).astype(BF16)
    xext_ref[0:HALO, :] = xp
    xext_ref[HALO:HALO + tm, :] = xn
    xext_ref[HALO + tm:, :] = xq
    n_ext = tm + 2 * HALO
    ext = jnp.dot(xext_ref[...], wmqk_ref[...], preferred_element_type=F32)
    mv = jnp.dot(xn, w1_ref[:, 0:M_WIDTH], preferred_element_type=F32)
    aq = jnp.dot(xn, w1_ref[:, M_WIDTH:M_WIDTH + A_QK], preferred_element_type=F32)
    ak = jnp.dot(xn, w1_ref[:, M_WIDTH + A_QK:M_WIDTH + 2 * A_QK], preferred_element_type=F32)
    avt = lax.dot_general(wavt_ref[...], xn, _NT, preferred_element_type=F32)
    gt_ref[...] = lax.dot_general(wgt_ref[...], xn, _NT, preferred_element_type=F32)

    left = CONV_WIDTH // 2
    y = None
    for j in range(CONV_WIDTH):
        sh = (left - j) % n_ext
        r = ext if sh == 0 else pltpu.roll(ext, sh, axis=0)
        term = convw_ref[j:j + 1, :] * r[HALO:HALO + tm, :]
        y = term if y is None else y + term
    y = _silu(y + convb_ref[...])
    qscale = M_QK_DIM ** -0.5
    for h in range(M_HEADS):
        mq_ref[0, h] = (y[:, h * M_QK_DIM:(h + 1) * M_QK_DIM] * qscale).astype(BF16)
        mk_ref[0, h] = y[:, M_QK + h * M_QK_DIM:M_QK + (h + 1) * M_QK_DIM].astype(BF16)
    for h in range(M_HEADS):
        mv_ref[0, h] = mv[:, h * M_V_DIM:(h + 1) * M_V_DIM].astype(BF16)
    rc, rs = rc_ref[...], rs_ref[...]
    half = ROPE_DIMS // 2
    first_half = (lax.broadcasted_iota(jnp.int32, rc.shape, 1) & (A_HEAD_DIM - 1)) < half

    def rope(a):
        partner = jnp.where(first_half, pltpu.roll(a, LANES - half, axis=1), pltpu.roll(a, half, axis=1))
        return a * rc + partner * rs

    ascale = A_HEAD_DIM ** -0.5 * math.log2(math.e)
    for h in range(A_HEADS):
        aq_ref[0, h] = (rope(aq[:, h * LANES:(h + 1) * LANES]) * ascale).astype(BF16)
        ak_ref[0, h] = rope(ak[:, h * LANES:(h + 1) * LANES]).astype(BF16)
    ones_tile = (lax.broadcasted_iota(jnp.int32, (BF16_ROWS, tm), 0) == 0).astype(BF16)
    for h in range(A_HEADS):
        avt_ref[0, 0, h, 0:A_V_DIM, :] = avt[h * A_V_DIM:(h + 1) * A_V_DIM, :].astype(BF16)
        avt_ref[0, 0, h, A_V_DIM:A_V_ROWS, :] = ones_tile


def _inproj_call(x2d, preg, wmqk, w1, wavt, wgt, convw, convb, rc, rs, B, S, tm):
    T = B * S
    tps = S // tm
    nh = tm // HALO
    const = lambda i: (0, 0)
    in_specs = [
        pl.BlockSpec((HALO, D_MODEL), lambda i: (jnp.maximum(i * nh - 1, 0), 0)),
        pl.BlockSpec((tm, D_MODEL), lambda i: (i, 0)),
        pl.BlockSpec((HALO, D_MODEL), lambda i: (jnp.minimum((i + 1) * nh, T // HALO - 1), 0)),
        pl.BlockSpec((1, D_MODEL), const),
        pl.BlockSpec(wmqk.shape, const),
        pl.BlockSpec(w1.shape, const),
        pl.BlockSpec(wavt.shape, const),
        pl.BlockSpec(wgt.shape, const),
        pl.BlockSpec(convw.shape, const),
        pl.BlockSpec(convb.shape, const),
        pl.BlockSpec((tm, LANES), lambda i: (lax.rem(i, tps), 0)),
        pl.BlockSpec((tm, LANES), lambda i: (lax.rem(i, tps), 0)),
    ]
    bt = lambda i: (i // tps, 0, lax.rem(i, tps), 0)
    out_shape = [
        jax.ShapeDtypeStruct((B, M_HEADS, S, M_QK_DIM), BF16),
        jax.ShapeDtypeStruct((B, M_HEADS, S, M_QK_DIM), BF16),
        jax.ShapeDtypeStruct((B, M_HEADS, S, M_V_DIM), BF16),
        jax.ShapeDtypeStruct((N_GATE_ROWS, T), F32),
        jax.ShapeDtypeStruct((B, A_HEADS, S, LANES), BF16),
        jax.ShapeDtypeStruct((B, A_HEADS, S, LANES), BF16),
        jax.ShapeDtypeStruct((B, tps, A_HEADS, A_V_ROWS, tm), BF16),
    ]
    out_specs = [
        pl.BlockSpec((1, M_HEADS, tm, M_QK_DIM), bt),
        pl.BlockSpec((1, M_HEADS, tm, M_QK_DIM), bt),
        pl.BlockSpec((1, M_HEADS, tm, M_V_DIM), bt),
        pl.BlockSpec((N_GATE_ROWS, tm), lambda i: (0, i)),
        pl.BlockSpec((1, A_HEADS, tm, LANES), bt),
        pl.BlockSpec((1, A_HEADS, tm, LANES), bt),
        pl.BlockSpec((1, 1, A_HEADS, A_V_ROWS, tm), lambda i: (i // tps, lax.rem(i, tps), 0, 0, 0)),
    ]
    return pl.pallas_call(
        functools.partial(_inproj_body, tps, tm),
        grid=(T // tm,),
        in_specs=in_specs,
        out_specs=out_specs,
        out_shape=out_shape,
        scratch_shapes=[pltpu.VMEM((tm + 2 * HALO, D_MODEL), BF16)],
        compiler_params=pltpu.CompilerParams(dimension_semantics=("arbitrary",),
                                             vmem_limit_bytes=VMEM_LIMIT),
        name="inproj",
    )(x2d, x2d, x2d, preg, wmqk, w1, wavt, wgt, convw, convb, rc, rs)


N_GATE_COLS = 5


def _gates_body(S, ct, f_ref, i_ref, fb_ref, ib_ref, rows_ref, cols_ref, scan_ref, stage_ref):
    h = pl.program_id(1)

    @pl.when(h == 0)
    def _():
        _gate_scans(S, f_ref, i_ref, fb_ref, ib_ref, scan_ref)

    rows_ref[...] = jnp.zeros_like(rows_ref)
    stage_ref[...] = jnp.zeros_like(stage_ref)
    for d in range(2):
        rows_ref[0, 0, d:d + 1, :] = scan_ref[N_GATE_COLS, pl.ds(d * M_HEADS + h, 1), :]
    for t in range(S // ct):
        sl = slice(t * ct, (t + 1) * ct)
        for k in range(N_GATE_COLS):
            for d in range(2):
                stage_ref[2 * k + d:2 * k + d + 1, :] = scan_ref[k, pl.ds(d * M_HEADS + h, 1), sl]
        cols_ref[0, 0, sl, :] = stage_ref[...].T


def _gate_scans(S, f_ref, i_ref, fb_ref, ib_ref, scan_ref):
    L = M_CHUNK
    f = f_ref[...] + fb_ref[...]
    ig = i_ref[...] + ib_ref[...]
    lf = jnp.minimum(f, 0.0) - jnp.log1p(jnp.exp(-jnp.abs(f)))
    lane = lax.broadcasted_iota(jnp.int32, f.shape, 1)
    pos = lane & (L - 1)
    fwd = lax.broadcasted_iota(jnp.int32, f.shape, 0) < M_HEADS

    def from_left(x, s):
        return pltpu.roll(x, s, axis=1)

    def from_right(x, s):
        return pltpu.roll(x, S - s, axis=1)

    def chunk_scans(x, op, ident):
        pre, suf = x, x
        s = 1
        while s < L:
            pre = op(pre, jnp.where(pos >= s, from_left(pre, s), ident))
            suf = op(suf, jnp.where(pos < L - s, from_right(suf, s), ident))
            s *= 2
        return pre, suf

    pre, suf = chunk_scans(lf, jnp.add, 0.0)
    a = jnp.where(fwd, pre, suf)
    a_tot = pre + suf - lf
    w = ig - a
    wpre, wsuf = chunk_scans(w, jnp.maximum, -jnp.inf)
    rowmax = a + jnp.where(fwd, wpre, wsuf)
    g_max = a_tot + jnp.maximum(wpre, wsuf)

    af, gf, ab, gb = a_tot, g_max, a_tot, g_max
    s = L
    while s < S:
        ok = lane >= s
        a_sh, g_sh = from_left(af, s), from_left(gf, s)
        gf = jnp.where(ok, jnp.maximum(g_sh + af, gf), gf)
        af = jnp.where(ok, af + a_sh, af)
        ok = lane < S - s
        a_sh, g_sh = from_right(ab, s), from_right(gb, s)
        gb = jnp.where(ok, jnp.maximum(g_sh + ab, gb), gb)
        ab = jnp.where(ok, ab + a_sh, ab)
        s *= 2
    m_f = jnp.maximum(af, gf)
    m_b = jnp.maximum(ab, gb)
    m_prev = jnp.where(fwd, jnp.where(lane >= L, from_left(m_f, L), 0.0),
                       jnp.where(lane < S - L, from_right(m_b, L), 0.0))

    inter = a + m_prev
    mj = jnp.maximum(inter, rowmax)
    m_new = jnp.maximum(a_tot + m_prev, g_max)
    scan_ref[0] = a - mj
    scan_ref[1] = jnp.exp(inter - mj)
    scan_ref[2] = jnp.exp(-mj)
    scan_ref[3] = jnp.exp(a_tot + w - m_new)
    scan_ref[4] = jnp.exp(a_tot + m_prev - m_new)
    scan_ref[N_GATE_COLS] = w


def _gates_call(gt, gb, B, S):
    ct = min(2048, S)
    return pl.pallas_call(
        functools.partial(_gates_body, S, ct),
        grid=(B, M_HEADS),
        in_specs=[pl.BlockSpec((8, S), lambda b, h: (0, b)),
                  pl.BlockSpec((8, S), lambda b, h: (1, b)),
                  pl.BlockSpec((8, 1), lambda b, h: (0, 0)),
                  pl.BlockSpec((8, 1), lambda b, h: (1, 0))],
        out_specs=[pl.BlockSpec((1, 1, 8, S), lambda b, h: (b, h, 0, 0)),
                   pl.BlockSpec((1, 1, S, LANES), lambda b, h: (b, h, 0, 0))],
        out_shape=[jax.ShapeDtypeStruct((B, M_HEADS, 8, S), F32),
                   jax.ShapeDtypeStruct((B, M_HEADS, S, LANES), F32)],
        scratch_shapes=[pltpu.VMEM((N_GATE_COLS + 1, 8, S), F32),
                        pltpu.VMEM((LANES, ct), F32)],
        compiler_params=pltpu.CompilerParams(dimension_semantics=("arbitrary", "arbitrary"),
                                             vmem_limit_bytes=VMEM_LIMIT),
        name="gates",
    )(gt, gt, gb, gb)


class _MlstmScan:
    def __init__(self, rev, nchunk, q_ref, k_ref, v_ref, rows_ref, cols_ref, h_ref, c_ref):
        self.rev, self.d = rev, (1 if rev else 0)
        self.q_ref, self.k_ref, self.v_ref = q_ref, k_ref, v_ref
        self.rows_ref, self.cols_ref, self.h_ref, self.c_ref = rows_ref, cols_ref, h_ref, c_ref
        self.order = list(range(nchunk - 1, -1, -1) if rev else range(nchunk))
        L = M_CHUNK
        row_i = lax.broadcasted_iota(jnp.int32, (L, L), 0)
        col_i = lax.broadcasted_iota(jnp.int32, (L, L), 1)
        self.visible = (col_i >= row_i) if rev else (col_i <= row_i)
        self.ones_col = (lax.broadcasted_iota(jnp.int32, (L, LANES), 1) == 0).astype(BF16)

    @staticmethod
    def rows_of(c):
        return slice(c * M_CHUNK, (c + 1) * M_CHUNK)

    def gate_col(self, c, k):
        return self.cols_ref[0, 0, self.rows_of(c), 2 * k + self.d:2 * k + self.d + 1]

    def v_ext(self, c):
        return jnp.concatenate([self.v_ref[0, 0, self.rows_of(c), :], self.ones_col], axis=1)

    def input_products(self):
        self.qk, self.kv = {}, {}
        for c in self.order:
            q, k = self.q_ref[0, 0, self.rows_of(c), :], self.k_ref[0, 0, self.rows_of(c), :]
            self.qk[c] = lax.dot_general(q, k, _NT, preferred_element_type=F32)
            wv = (self.gate_col(c, 3) * self.v_ext(c).astype(F32)).astype(BF16)
            self.kv[c] = lax.dot_general(k, wv, _TN, preferred_element_type=F32)

    def states(self):
        C = self.c_ref[...]
        self.c_in = {}
        for c in self.order:
            self.c_in[c] = C.astype(BF16)
            C = self.gate_col(c, 4)[0:1, :] * C + self.kv[c]
        self.c_ref[...] = C

    def read_out(self):
        for c in self.order:
            u, w_inter, e = self.gate_col(c, 0), self.gate_col(c, 1), self.gate_col(c, 2)
            w_row = self.rows_ref[0, 0, self.d:self.d + 1, self.rows_of(c)]
            p = jnp.exp(jnp.where(self.visible, u + w_row, -jnp.inf))
            s = self.qk[c] * p
            q = self.q_ref[0, 0, self.rows_of(c), :]
            num = (w_inter * jnp.dot(q, self.c_in[c], preferred_element_type=F32)
                   + jnp.dot(s.astype(BF16), self.v_ext(c), preferred_element_type=F32))
            den = num[:, M_V_DIM:M_V_DIM + 1]
            self.h_ref[0, self.rows_of(c), :] = num[:, 0:M_V_DIM] / jnp.maximum(jnp.abs(den), e)


def _mlstm_body(nchunk, qf_ref, kf_ref, vf_ref, rowsf_ref, colsf_ref, qb_ref, kb_ref, vb_ref, rowsb_ref,
                colsb_ref, hf_ref, hb_ref, cf_ref, cb_ref):
    @pl.when(pl.program_id(2) == 0)
    def _():
        cf_ref[...] = jnp.zeros_like(cf_ref)
        cb_ref[...] = jnp.zeros_like(cb_ref)

    fwd = _MlstmScan(False, nchunk, qf_ref, kf_ref, vf_ref, rowsf_ref, colsf_ref, hf_ref, cf_ref)
    bwd = _MlstmScan(True, nchunk, qb_ref, kb_ref, vb_ref, rowsb_ref, colsb_ref, hb_ref, cb_ref)
    fwd.input_products()
    bwd.input_products()
    fwd.states()
    bwd.states()
    fwd.read_out()
    bwd.read_out()


def _mlstm_call(mq, mk, mv, rows, cols, B, S, tr):
    ns = S // tr

    def specs(pos):
        return [pl.BlockSpec((1, 1, tr, M_QK_DIM), lambda b, h, s: (b, h, pos(s), 0)),
                pl.BlockSpec((1, 1, tr, M_QK_DIM), lambda b, h, s: (b, h, pos(s), 0)),
                pl.BlockSpec((1, 1, tr, M_V_DIM), lambda b, h, s: (b, h, pos(s), 0)),
                pl.BlockSpec((1, 1, 8, tr), lambda b, h, s: (b, h, 0, pos(s))),
                pl.BlockSpec((1, 1, tr, LANES), lambda b, h, s: (b, h, pos(s), 0))]

    fpos = lambda s: s
    bpos = lambda s: ns - 1 - s
    state = pltpu.VMEM((M_QK_DIM, M_V_DIM + LANES), F32)
    return pl.pallas_call(
        functools.partial(_mlstm_body, tr // M_CHUNK),
        grid=(B, M_HEADS, ns),
        in_specs=specs(fpos) + specs(bpos),
        out_specs=[pl.BlockSpec((1, tr, M_V_DIM), lambda b, h, s: (b, fpos(s), h)),
                   pl.BlockSpec((1, tr, M_V_DIM), lambda b, h, s: (b, bpos(s), h))],
        out_shape=[jax.ShapeDtypeStruct((B, S, M_WIDTH), F32),
                   jax.ShapeDtypeStruct((B, S, M_WIDTH), F32)],
        scratch_shapes=[state, state],
        compiler_params=pltpu.CompilerParams(dimension_semantics=("arbitrary", "arbitrary", "arbitrary"),
                                             vmem_limit_bytes=VMEM_LIMIT),
        name="mlstm",
    )(mq, mk, mv, rows, cols, mq, mk, mv, rows, cols)


def _attn_body(nkc, nt, tq, tk, tv, lam_init, q_ref, k_ref, vt_ref, lq_ref, g_ref, o_ref,
               qz_ref, st0_ref, st1_ref, m_ref, acc_ref):
    def tile_rows(t):
        return pl.ds(pl.multiple_of(t * tq, tq), tq)

    def load_queries(t):
        q = q_ref[0, 0, tile_rows(t), :]
        lane = lax.broadcasted_iota(jnp.int32, q.shape, 1)
        zero = jnp.zeros_like(q)
        qz_ref[0:tq, :] = jnp.where(lane < A_HEAD_DIM, q, zero)
        qz_ref[tq:2 * tq, :] = jnp.where(lane >= A_HEAD_DIM, q, zero)

    def reset_stats():
        m_ref[...] = jnp.full(m_ref.shape, -jnp.inf, F32)
        acc_ref[...] = jnp.zeros_like(acc_ref)

    strips = [slice(c * MXU_N, (c + 1) * MXU_N) for c in range(2 * tq // MXU_N)]
    nv = tk // tv

    def score_strip(j, st_ref, sl):
        kj = k_ref[0, 0, pl.ds(pl.multiple_of(j * tk, tk), tk), :]
        st = lax.dot_general(kj, qz_ref[sl, :], _NT, preferred_element_type=F32)
        st_ref[:, sl] = st
        return jnp.max(st, axis=0, keepdims=True)

    def accumulate_strip(j, st_ref, sl, cm):
        m_old = m_ref[:, sl]
        m_new = jnp.maximum(m_old, cm)
        alpha = jnp.exp2(m_old - m_new)
        p = jnp.exp2(st_ref[:, sl] - m_new).astype(BF16)
        pv = jnp.dot(vt_ref[0, j * nv, 0], p[0:tv], preferred_element_type=F32)
        for u in range(1, nv):
            pv = pv + jnp.dot(vt_ref[0, j * nv + u, 0], p[u * tv:(u + 1) * tv],
                              preferred_element_type=F32)
        acc_ref[:, sl] = alpha * acc_ref[:, sl] + pv
        m_ref[:, sl] = m_new

    def step(j_next, st_next_ref, j_cur, st_cur_ref, cmax_cur):
        cmax_next = []
        for sl, cm in zip(strips, cmax_cur):
            cmax_next.append(score_strip(j_next, st_next_ref, sl))
            accumulate_strip(j_cur, st_cur_ref, sl, cm)
        return tuple(cmax_next)

    def pair(i, cmax0):
        j = 2 * i
        cmax1 = step(j + 1, st1_ref, j, st0_ref, cmax0)
        return step(j + 2, st0_ref, j + 1, st1_ref, cmax1)

    def two_pairs(i, cmax0):
        return pair(2 * i + 1, pair(2 * i, cmax0))

    def all_but_last_chunk(cmax0):
        npairs = nkc // 2 - 1
        cmax0 = lax.fori_loop(0, npairs // 2, two_pairs, cmax0)
        if npairs % 2:
            cmax0 = pair(npairs - 1, cmax0)
        return step(nkc - 1, st1_ref, nkc - 2, st0_ref, cmax0)

    lq = lq_ref[...]
    lam = (jnp.exp(jnp.sum(lq[0:1] * lq[1:2], axis=1, keepdims=True))
           - jnp.exp(jnp.sum(lq[2:3] * lq[3:4], axis=1, keepdims=True)) + lam_init)

    def finalize(t):
        accn = acc_ref[0:A_V_DIM, :] / acc_ref[A_V_DIM:A_V_DIM + 1, :]
        ot = accn[:, 0:tq] - lam * accn[:, tq:2 * tq]
        otn = ot * lax.rsqrt(jnp.mean(ot * ot, axis=0, keepdims=True) + EPS)
        o_ref[tile_rows(t), :] = (otn.T * g_ref[...]) * (1.0 - lam_init)

    def tile(t, cmax0):
        cmax1 = all_but_last_chunk(cmax0)
        load_queries(t + 1)
        cmax_next = []
        for sl, cm in zip(strips, cmax1):
            cmax_next.append(score_strip(0, st0_ref, sl))
            accumulate_strip(nkc - 1, st1_ref, sl, cm)
        finalize(t)
        reset_stats()
        return tuple(cmax_next)

    load_queries(0)
    reset_stats()
    cmax0 = tuple(score_strip(0, st0_ref, sl) for sl in strips)
    cmax0 = lax.fori_loop(0, nt - 1, tile, cmax0)
    cmax1 = all_but_last_chunk(cmax0)
    for sl, cm in zip(strips, cmax1):
        accumulate_strip(nkc - 1, st1_ref, sl, cm)
    finalize(nt - 1)


def _attn_call(aq, ak, avt, lq, ag, B, S, tq, nt, tk, tv, lam_init):
    nkc = S // tk
    nq = S // (nt * tq)
    assert nkc >= 2 and nkc % 2 == 0, "key chunks are processed in pairs"
    assert tk % tv == 0 and S % (nt * tq) == 0
    return pl.pallas_call(
        functools.partial(_attn_body, nkc, nt, tq, tk, tv, lam_init),
        grid=(B, A_HEADS, nq),
        in_specs=[pl.BlockSpec((1, 1, nt * tq, LANES), lambda b, h, i: (b, h, i, 0)),
                  pl.BlockSpec((1, 1, S, LANES), lambda b, h, i: (b, h, 0, 0)),
                  pl.BlockSpec((1, S // tv, 1, A_V_ROWS, tv), lambda b, h, i: (b, 0, h, 0, 0)),
                  pl.BlockSpec(lq.shape, lambda b, h, i: (0, 0)),
                  pl.BlockSpec(ag.shape, lambda b, h, i: (0, 0))],
        out_specs=pl.BlockSpec((nt * tq, A_V_DIM), lambda b, h, i: (b * nq + i, h)),
        out_shape=jax.ShapeDtypeStruct((B * S, A_WIDTH), F32),
        scratch_shapes=[pltpu.VMEM((2 * tq, LANES), BF16),
                        pltpu.VMEM((tk, 2 * tq), F32),
                        pltpu.VMEM((tk, 2 * tq), F32),
                        pltpu.VMEM((1, 2 * tq), F32),
                        pltpu.VMEM((A_V_ROWS, 2 * tq), F32)],
        compiler_params=pltpu.CompilerParams(dimension_semantics=("arbitrary", "arbitrary", "arbitrary"),
                                             vmem_limit_bytes=VMEM_LIMIT),
        name="attn",
    )(aq, ak, avt, lq, ag)


def _epilogue_body(x_ref, hf_ref, hb_ref, o_ref, preg_ref, w4_ref, mng_ref, bm_ref, wbm_ref, wba_ref,
                   wout_ref, postg_ref, out_ref):
    x = x_ref[...]
    xn = (_rms(x) * preg_ref[...]).astype(BF16)

    def proj(lo, hi):
        return jnp.dot(xn, w4_ref[:, lo:hi], preferred_element_type=F32)

    h = hf_ref[...] + hb_ref[...]
    mng = mng_ref[...]
    hn = jnp.concatenate(
        [_rms(h[:, k * M_V_DIM:(k + 1) * M_V_DIM]) * mng[:, k * M_V_DIM:(k + 1) * M_V_DIM]
         for k in range(M_HEADS)], axis=1)
    h_a = jax.nn.sigmoid(proj(0, M_WIDTH)) * hn * _silu(proj(M_WIDTH, 2 * M_WIDTH))
    ya = jnp.dot(h_a.astype(BF16), wbm_ref[...], preferred_element_type=F32)
    h_b = o_ref[...] * _silu(proj(2 * M_WIDTH, 2 * M_WIDTH + A_WIDTH))
    yb = jnp.dot(h_b.astype(BF16), wba_ref[...], preferred_element_type=F32)
    g0 = 2 * M_WIDTH + A_WIDTH
    bm = bm_ref[...]
    g_a = jax.nn.sigmoid(proj(g0, g0 + D_MODEL) + bm[:, 0:D_MODEL])
    g_b = jax.nn.sigmoid(proj(g0 + D_MODEL, g0 + 2 * D_MODEL) + bm[:, D_MODEL:2 * D_MODEL])
    y = g_a * ya + g_b * yb
    out = jnp.dot(y.astype(BF16), wout_ref[...], preferred_element_type=F32)
    out_ref[...] = x + _rms(out) * postg_ref[...]


def _epilogue_call(x2d, hf, hb, o, preg, w4, mng, bm, wbm, wba, wout, postg, te):
    T = x2d.shape[0]
    row = lambda i: (i, 0)
    const = lambda i: (0, 0)
    single = pl.Buffered(1)
    return pl.pallas_call(
        _epilogue_body,
        grid=(T // te,),
        in_specs=[pl.BlockSpec((te, D_MODEL), row),
                  pl.BlockSpec((te, M_WIDTH), row),
                  pl.BlockSpec((te, M_WIDTH), row),
                  pl.BlockSpec((te, A_WIDTH), row),
                  pl.BlockSpec(preg.shape, const),
                  pl.BlockSpec(w4.shape, const, pipeline_mode=single),
                  pl.BlockSpec(mng.shape, const),
                  pl.BlockSpec(bm.shape, const),
                  pl.BlockSpec(wbm.shape, const, pipeline_mode=single),
                  pl.BlockSpec(wba.shape, const, pipeline_mode=single),
                  pl.BlockSpec(wout.shape, const, pipeline_mode=single),
                  pl.BlockSpec(postg.shape, const)],
        out_specs=pl.BlockSpec((te, D_MODEL), row),
        out_shape=jax.ShapeDtypeStruct((T, D_MODEL), F32),
        compiler_params=pltpu.CompilerParams(dimension_semantics=("arbitrary",),
                                             vmem_limit_bytes=VMEM_LIMIT),
        name="epilogue",
    )(x2d, hf, hb, o, preg, w4, mng, bm, wbm, wba, wout, postg)


def _rope_tables(S):
    f32 = np.float32
    inv = ROPE_THETA ** (-np.arange(0, ROPE_DIMS, 2, dtype=np.float64) / ROPE_DIMS)
    ang = np.arange(S, dtype=np.float64)[:, None] * inv[None, :]
    cos, sin = np.cos(ang).astype(f32), np.sin(ang).astype(f32)
    one = np.ones((S, A_HEAD_DIM - ROPE_DIMS), f32)
    zero = np.zeros((S, A_HEAD_DIM - ROPE_DIMS), f32)
    rc = np.concatenate([cos, cos, one], axis=1)
    rs = np.concatenate([-sin, sin, zero], axis=1)
    rep = LANES // A_HEAD_DIM
    return jnp.asarray(np.tile(rc, (1, rep))), jnp.asarray(np.tile(rs, (1, rep)))


def _gate_layout(m):
    return jnp.concatenate([m[..., 1, :], m[..., 3, :], m[..., 0, :], m[..., 2, :]], axis=-1)


def _layer(x, layer_idx, pre_g, w_in, conv_w, conv_b, gate_b, m_norm_g, lambda_qk, a_norm_g,
           w_branch_m, w_branch_a, b_merge, w_out, post_g):
    B, S, _ = x.shape
    T = B * S
    tm = min(512, S)
    tq = min(2048, S // 2)
    nt = min(4, S // tq)
    tk = min(512, S // 2)
    tr = min(512, S)
    te = min(256, T)
    lam_init = 0.8 - 0.6 * math.exp(-0.3 * layer_idx)

    cuts = [0]
    for sz in SPLIT_SIZES:
        cuts.append(cuts[-1] + sz)
    col = lambda k: w_in[:, cuts[k]:cuts[k + 1]]
    wmqk = col(0).astype(BF16)
    w1 = jnp.concatenate([col(1), col(5), col(6)], axis=1).astype(BF16)
    wavt = col(7).T.astype(BF16)
    wgt = _gate_layout(col(2).reshape(D_MODEL, 4, M_HEADS)).T.astype(BF16)
    gb = _gate_layout(gate_b).reshape(N_GATE_ROWS, 1).astype(F32)
    w4 = jnp.concatenate([col(3), col(4), col(8), col(9)], axis=1).astype(BF16)
    rc, rs = _rope_tables(S)

    x2d = x.reshape(T, D_MODEL)
    preg = pre_g.reshape(1, D_MODEL)
    mq, mk, mv, gt, aq, ak, avt = _inproj_call(
        x2d, preg, wmqk, w1, wavt, wgt, conv_w, conv_b.reshape(1, 2 * M_QK), rc, rs, B, S, tm)
    rows, cols = _gates_call(gt, gb, B, S)
    hf, hb = _mlstm_call(mq, mk, mv, rows, cols, B, S, tr)
    o = _attn_call(aq, ak, avt, lambda_qk.astype(F32), a_norm_g.reshape(1, A_V_DIM), B, S, tq, nt, tk, tm,
                   lam_init)
    out = _epilogue_call(
        x2d, hf.reshape(T, M_WIDTH), hb.reshape(T, M_WIDTH), o, preg, w4,
        m_norm_g.reshape(1, M_WIDTH), b_merge.reshape(1, 2 * D_MODEL),
        w_branch_m.astype(BF16), w_branch_a.astype(BF16), w_out.astype(BF16),
        post_g.reshape(1, D_MODEL), te)
    return out.reshape(B, S, D_MODEL)


def kernel(x, pre_norm_g, w_in, conv_w, conv_b, gate_b, m_norm_g, lambda_qk, a_norm_g, w_branch_m,
           w_branch_a, b_merge, w_out, post_norm_g):
    for l in range(pre_norm_g.shape[0]):
        x = _layer(x, l, pre_norm_g[l], w_in[l], conv_w[l], conv_b[l], gate_b[l], m_norm_g[l],
                   lambda_qk[l], a_norm_g[l], w_branch_m[l], w_branch_a[l], b_merge[l], w_out[l],
                   post_norm_g[l])
    return x
```

```python
import functools
import math

import jax
import jax.numpy as jnp
import numpy as np
from jax import lax
from jax.experimental import pallas as pl
from jax.experimental.pallas import tpu as pltpu

D_MODEL = 1024
M_HEADS = 4
M_QK_DIM = 128
M_V_DIM = 256
M_QK = M_HEADS * M_QK_DIM
M_WIDTH = M_HEADS * M_V_DIM
M_CHUNK = 64
CONV_WIDTH = 4
A_HEADS = 8
A_HEAD_DIM = 64
A_V_DIM = 2 * A_HEAD_DIM
A_QK = A_HEADS * 2 * A_HEAD_DIM
A_WIDTH = A_HEADS * A_V_DIM
ROPE_THETA = 500000.0
ROPE_DIMS = A_HEAD_DIM // 4
EPS = 1e-6
SPLIT_SIZES = (2 * M_QK, M_WIDTH, 4 * M_HEADS, M_WIDTH, M_WIDTH, A_QK, A_QK, A_WIDTH, A_WIDTH, 2 * D_MODEL)

LANES = 128
MXU_N = 256
F32_ROWS = 8
A_V_ROWS = A_V_DIM + F32_ROWS
HALO = 16
N_GATE_ROWS = 2 * 2 * M_HEADS
VMEM_LIMIT = 60 * 1024 * 1024

F32 = jnp.float32
BF16 = jnp.bfloat16
_NT = (((1,), (1,)), ((), ()))
_TN = (((0,), (0,)), ((), ()))


def _rms(xf):
    return xf * lax.rsqrt(jnp.mean(xf * xf, axis=-1, keepdims=True) + EPS)


def _silu(y):
    return y * jax.nn.sigmoid(y)


def _inproj_body(tps, tm, xprev_ref, x_ref, xnext_ref, preg_ref, wmqk_ref, w1_ref, wavt_ref, wgt_ref,
                 convw_ref, convb_ref, rc_ref, rs_ref,
                 mq_ref, mk_ref, mv_ref, gt_ref, aq_ref, ak_ref, avt_ref, xext_ref):
    t = lax.rem(pl.program_id(0), tps)
    g = preg_ref[...]
    xn = (_rms(x_ref[...]) * g).astype(BF16)
    xp = jnp.where(t != 0, _rms(xprev_ref[...]) * g, 0.0).astype(BF16)
    xq = jnp.where(t != tps - 1, _rms(xnext_ref[...]) * g, 0.0).astype(BF16)
    xext_ref[0:HALO, :] = xp
    xext_ref[HALO:HALO + tm, :] = xn
    xext_ref[HALO + tm:, :] = xq
    n_ext = tm + 2 * HALO
    ext = jnp.dot(xext_ref[...], wmqk_ref[...], preferred_element_type=F32)
    mv = jnp.dot(xn, w1_ref[:, 0:M_WIDTH], preferred_element_type=F32)
    aq = jnp.dot(xn, w1_ref[:, M_WIDTH:M_WIDTH + A_QK], preferred_element_type=F32)
    ak = jnp.dot(xn, w1_ref[:, M_WIDTH + A_QK:M_WIDTH + 2 * A_QK], preferred_element_type=F32)
    avt = lax.dot_general(wavt_ref[...], xn, _NT, preferred_element_type=F32)
    gt_ref[...] = lax.dot_general(wgt_ref[...], xn, _NT, preferred_element_type=F32)

    left = CONV_WIDTH // 2
    y = None
    for j in range(CONV_WIDTH):
        sh = (left - j) % n_ext
        r = ext if sh == 0 else pltpu.roll(ext, sh, axis=0)
        term = convw_ref[j:j + 1, :] * r[HALO:HALO + tm, :]
        y = term if y is None else y + term
    y = _silu(y + convb_ref[...])
    qscale = M_QK_DIM ** -0.5
    for h in range(M_HEADS):
        mq_ref[0, h] = (y[:, h * M_QK_DIM:(h + 1) * M_QK_DIM] * qscale).astype(BF16)
        mk_ref[0, h] = y[:, M_QK + h * M_QK_DIM:M_QK + (h + 1) * M_QK_DIM].astype(BF16)
    for h in range(M_HEADS):
        mv_ref[0, h] = mv[:, h * M_V_DIM:(h + 1) * M_V_DIM].astype(BF16)
    rc, rs = rc_ref[...], rs_ref[...]
    half = ROPE_DIMS // 2
    first_half = (lax.broadcasted_iota(jnp.int32, rc.shape, 1) & (A_HEAD_DIM - 1)) < half

    def rope(a):
        partner = jnp.where(first_half, pltpu.roll(a, LANES - half, axis=1), pltpu.roll(a, half, axis=1))
        return a * rc + partner * rs

    ascale = A_HEAD_DIM ** -0.5 * math.log2(math.e)
    for h in range(A_HEADS):
        aq_ref[0, h] = (rope(aq[:, h * LANES:(h + 1) * LANES]) * ascale).astype(BF16)
        ak_ref[0, h] = rope(ak[:, h * LANES:(h + 1) * LANES]).astype(BF16)
    ones_tile = (lax.broadcasted_iota(jnp.int32, (F32_ROWS, tm), 0) == 0).astype(F32)
    for h in range(A_HEADS):
        avt_ref[0, 0, h, 0:A_V_DIM, :] = avt[h * A_V_DIM:(h + 1) * A_V_DIM, :].astype(BF16).astype(F32)
        avt_ref[0, 0, h, A_V_DIM:A_V_ROWS, :] = ones_tile


def _inproj_call(x2d, preg, wmqk, w1, wavt, wgt, convw, convb, rc, rs, B, S, tm):
    T = B * S
    tps = S // tm
    nh = tm // HALO
    const = lambda i: (0, 0)
    in_specs = [
        pl.BlockSpec((HALO, D_MODEL), lambda i: (jnp.maximum(i * nh - 1, 0), 0)),
        pl.BlockSpec((tm, D_MODEL), lambda i: (i, 0)),
        pl.BlockSpec((HALO, D_MODEL), lambda i: (jnp.minimum((i + 1) * nh, T // HALO - 1), 0)),
        pl.BlockSpec((1, D_MODEL), const),
        pl.BlockSpec(wmqk.shape, const),
        pl.BlockSpec(w1.shape, const),
        pl.BlockSpec(wavt.shape, const),
        pl.BlockSpec(wgt.shape, const),
        pl.BlockSpec(convw.shape, const),
        pl.BlockSpec(convb.shape, const),
        pl.BlockSpec((tm, LANES), lambda i: (lax.rem(i, tps), 0)),
        pl.BlockSpec((tm, LANES), lambda i: (lax.rem(i, tps), 0)),
    ]
    bt = lambda i: (i // tps, 0, lax.rem(i, tps), 0)
    out_shape = [
        jax.ShapeDtypeStruct((B, M_HEADS, S, M_QK_DIM), BF16),
        jax.ShapeDtypeStruct((B, M_HEADS, S, M_QK_DIM), BF16),
        jax.ShapeDtypeStruct((B, M_HEADS, S, M_V_DIM), BF16),
        jax.ShapeDtypeStruct((N_GATE_ROWS, T), F32),
        jax.ShapeDtypeStruct((B, A_HEADS, S, LANES), BF16),
        jax.ShapeDtypeStruct((B, A_HEADS, S, LANES), BF16),
        jax.ShapeDtypeStruct((B, tps, A_HEADS, A_V_ROWS, tm), F32),
    ]
    out_specs = [
        pl.BlockSpec((1, M_HEADS, tm, M_QK_DIM), bt),
        pl.BlockSpec((1, M_HEADS, tm, M_QK_DIM), bt),
        pl.BlockSpec((1, M_HEADS, tm, M_V_DIM), bt),
        pl.BlockSpec((N_GATE_ROWS, tm), lambda i: (0, i)),
        pl.BlockSpec((1, A_HEADS, tm, LANES), bt),
        pl.BlockSpec((1, A_HEADS, tm, LANES), bt),
        pl.BlockSpec((1, 1, A_HEADS, A_V_ROWS, tm), lambda i: (i // tps, lax.rem(i, tps), 0, 0, 0)),
    ]
    return pl.pallas_call(
        functools.partial(_inproj_body, tps, tm),
        grid=(T // tm,),
        in_specs=in_specs,
        out_specs=out_specs,
        out_shape=out_shape,
        scratch_shapes=[pltpu.VMEM((tm + 2 * HALO, D_MODEL), BF16)],
        compiler_params=pltpu.CompilerParams(dimension_semantics=("arbitrary",),
                                             vmem_limit_bytes=VMEM_LIMIT),
        name="inproj",
    )(x2d, x2d, x2d, preg, wmqk, w1, wavt, wgt, convw, convb, rc, rs)


N_GATE_COLS = 5


def _gates_body(S, ct, f_ref, i_ref, fb_ref, ib_ref, rows_ref, cols_ref, scan_ref, stage_ref):
    h = pl.program_id(1)

    @pl.when(h == 0)
    def _():
        _gate_scans(S, f_ref, i_ref, fb_ref, ib_ref, scan_ref)

    rows_ref[...] = jnp.zeros_like(rows_ref)
    stage_ref[...] = jnp.zeros_like(stage_ref)
    for d in range(2):
        rows_ref[0, 0, d:d + 1, :] = scan_ref[N_GATE_COLS, pl.ds(d * M_HEADS + h, 1), :]
    for t in range(S // ct):
        sl = slice(t * ct, (t + 1) * ct)
        for k in range(N_GATE_COLS):
            for d in range(2):
                stage_ref[2 * k + d:2 * k + d + 1, :] = scan_ref[k, pl.ds(d * M_HEADS + h, 1), sl]
        cols_ref[0, 0, sl, :] = stage_ref[...].T


def _gate_scans(S, f_ref, i_ref, fb_ref, ib_ref, scan_ref):
    L = M_CHUNK
    f = f_ref[...] + fb_ref[...]
    ig = i_ref[...] + ib_ref[...]
    lf = jnp.minimum(f, 0.0) - jnp.log1p(jnp.exp(-jnp.abs(f)))
    lane = lax.broadcasted_iota(jnp.int32, f.shape, 1)
    pos = lane & (L - 1)
    fwd = lax.broadcasted_iota(jnp.int32, f.shape, 0) < M_HEADS

    def from_left(x, s):
        return pltpu.roll(x, s, axis=1)

    def from_right(x, s):
        return pltpu.roll(x, S - s, axis=1)

    def chunk_scans(x, op, ident):
        pre, suf = x, x
        s = 1
        while s < L:
            pre = op(pre, jnp.where(pos >= s, from_left(pre, s), ident))
            suf = op(suf, jnp.where(pos < L - s, from_right(suf, s), ident))
            s *= 2
        return pre, suf

    pre, suf = chunk_scans(lf, jnp.add, 0.0)
    a = jnp.where(fwd, pre, suf)
    a_tot = pre + suf - lf
    w = ig - a
    wpre, wsuf = chunk_scans(w, jnp.maximum, -jnp.inf)
    rowmax = a + jnp.where(fwd, wpre, wsuf)
    g_max = a_tot + jnp.maximum(wpre, wsuf)

    af, gf, ab, gb = a_tot, g_max, a_tot, g_max
    s = L
    while s < S:
        ok = lane >= s
        a_sh, g_sh = from_left(af, s), from_left(gf, s)
        gf = jnp.where(ok, jnp.maximum(g_sh + af, gf), gf)
        af = jnp.where(ok, af + a_sh, af)
        ok = lane < S - s
        a_sh, g_sh = from_right(ab, s), from_right(gb, s)
        gb = jnp.where(ok, jnp.maximum(g_sh + ab, gb), gb)
        ab = jnp.where(ok, ab + a_sh, ab)
        s *= 2
    m_f = jnp.maximum(af, gf)
    m_b = jnp.maximum(ab, gb)
    m_prev = jnp.where(fwd, jnp.where(lane >= L, from_left(m_f, L), 0.0),
                       jnp.where(lane < S - L, from_right(m_b, L), 0.0))

    inter = a + m_prev
    mj = jnp.maximum(inter, rowmax)
    m_new = jnp.maximum(a_tot + m_prev, g_max)
    scan_ref[0] = a - mj
    scan_ref[1] = jnp.exp(inter - mj)
    scan_ref[2] = jnp.exp(-mj)
    scan_ref[3] = jnp.exp(a_tot + w - m_new)
    scan_ref[4] = jnp.exp(a_tot + m_prev - m_new)
    scan_ref[N_GATE_COLS] = w


def _gates_call(gt, gb, B, S):
    ct = min(2048, S)
    return pl.pallas_call(
        functools.partial(_gates_body, S, ct),
        grid=(B, M_HEADS),
        in_specs=[pl.BlockSpec((8, S), lambda b, h: (0, b)),
                  pl.BlockSpec((8, S), lambda b, h: (1, b)),
                  pl.BlockSpec((8, 1), lambda b, h: (0, 0)),
                  pl.BlockSpec((8, 1), lambda b, h: (1, 0))],
        out_specs=[pl.BlockSpec((1, 1, 8, S), lambda b, h: (b, h, 0, 0)),
                   pl.BlockSpec((1, 1, S, LANES), lambda b, h: (b, h, 0, 0))],
        out_shape=[jax.ShapeDtypeStruct((B, M_HEADS, 8, S), F32),
                   jax.ShapeDtypeStruct((B, M_HEADS, S, LANES), F32)],
        scratch_shapes=[pltpu.VMEM((N_GATE_COLS + 1, 8, S), F32),
                        pltpu.VMEM((LANES, ct), F32)],
        compiler_params=pltpu.CompilerParams(dimension_semantics=("arbitrary", "arbitrary"),
                                             vmem_limit_bytes=VMEM_LIMIT),
        name="gates",
    )(gt, gt, gb, gb)


class _MlstmScan:
    def __init__(self, rev, nchunk, q_ref, k_ref, v_ref, rows_ref, cols_ref, h_ref, c_ref):
        self.rev, self.d = rev, (1 if rev else 0)
        self.q_ref, self.k_ref, self.v_ref = q_ref, k_ref, v_ref
        self.rows_ref, self.cols_ref, self.h_ref, self.c_ref = rows_ref, cols_ref, h_ref, c_ref
        self.order = list(range(nchunk - 1, -1, -1) if rev else range(nchunk))
        L = M_CHUNK
        row_i = lax.broadcasted_iota(jnp.int32, (L, L), 0)
        col_i = lax.broadcasted_iota(jnp.int32, (L, L), 1)
        self.visible = (col_i >= row_i) if rev else (col_i <= row_i)
        self.ones_col = (lax.broadcasted_iota(jnp.int32, (L, LANES), 1) == 0).astype(BF16)

    @staticmethod
    def rows_of(c):
        return slice(c * M_CHUNK, (c + 1) * M_CHUNK)

    def gate_col(self, c, k):
        return self.cols_ref[0, 0, self.rows_of(c), 2 * k + self.d:2 * k + self.d + 1]

    def v_ext(self, c):
        return jnp.concatenate([self.v_ref[0, 0, self.rows_of(c), :], self.ones_col], axis=1)

    def input_products(self):
        self.qk, self.kv = {}, {}
        for c in self.order:
            q, k = self.q_ref[0, 0, self.rows_of(c), :], self.k_ref[0, 0, self.rows_of(c), :]
            self.qk[c] = lax.dot_general(q, k, _NT, preferred_element_type=F32)
            wv = (self.gate_col(c, 3) * self.v_ext(c).astype(F32)).astype(BF16)
            self.kv[c] = lax.dot_general(k, wv, _TN, preferred_element_type=F32)

    def states(self):
        C = self.c_ref[...]
        self.c_in = {}
        for c in self.order:
            self.c_in[c] = C.astype(BF16)
            C = self.gate_col(c, 4)[0:1, :] * C + self.kv[c]
        self.c_ref[...] = C

    def read_out(self):
        for c in self.order:
            u, w_inter, e = self.gate_col(c, 0), self.gate_col(c, 1), self.gate_col(c, 2)
            w_row = self.rows_ref[0, 0, self.d:self.d + 1, self.rows_of(c)]
            p = jnp.exp(jnp.where(self.visible, u + w_row, -jnp.inf))
            s = self.qk[c] * p
            q = self.q_ref[0, 0, self.rows_of(c), :]
            num = (w_inter * jnp.dot(q, self.c_in[c], preferred_element_type=F32)
                   + jnp.dot(s.astype(BF16), self.v_ext(c), preferred_element_type=F32))
            den = num[:, M_V_DIM:M_V_DIM + 1]
            self.h_ref[0, self.rows_of(c), :] = num[:, 0:M_V_DIM] / jnp.maximum(jnp.abs(den), e)


def _mlstm_body(nchunk, qf_ref, kf_ref, vf_ref, rowsf_ref, colsf_ref, qb_ref, kb_ref, vb_ref, rowsb_ref,
                colsb_ref, hf_ref, hb_ref, cf_ref, cb_ref):
    @pl.when(pl.program_id(2) == 0)
    def _():
        cf_ref[...] = jnp.zeros_like(cf_ref)
        cb_ref[...] = jnp.zeros_like(cb_ref)

    fwd = _MlstmScan(False, nchunk, qf_ref, kf_ref, vf_ref, rowsf_ref, colsf_ref, hf_ref, cf_ref)
    bwd = _MlstmScan(True, nchunk, qb_ref, kb_ref, vb_ref, rowsb_ref, colsb_ref, hb_ref, cb_ref)
    fwd.input_products()
    bwd.input_products()
    fwd.states()
    bwd.states()
    fwd.read_out()
    bwd.read_out()


def _mlstm_call(mq, mk, mv, rows, cols, B, S, tr):
    ns = S // tr

    def specs(pos):
        return [pl.BlockSpec((1, 1, tr, M_QK_DIM), lambda b, h, s: (b, h, pos(s), 0)),
                pl.BlockSpec((1, 1, tr, M_QK_DIM), lambda b, h, s: (b, h, pos(s), 0)),
                pl.BlockSpec((1, 1, tr, M_V_DIM), lambda b, h, s: (b, h, pos(s), 0)),
                pl.BlockSpec((1, 1, 8, tr), lambda b, h, s: (b, h, 0, pos(s))),
                pl.BlockSpec((1, 1, tr, LANES), lambda b, h, s: (b, h, pos(s), 0))]

    fpos = lambda s: s
    bpos = lambda s: ns - 1 - s
    state = pltpu.VMEM((M_QK_DIM, M_V_DIM + LANES), F32)
    return pl.pallas_call(
        functools.partial(_mlstm_body, tr // M_CHUNK),
        grid=(B, M_HEADS, ns),
        in_specs=specs(fpos) + specs(bpos),
        out_specs=[pl.BlockSpec((1, tr, M_V_DIM), lambda b, h, s: (b, fpos(s), h)),
                   pl.BlockSpec((1, tr, M_V_DIM), lambda b, h, s: (b, bpos(s), h))],
        out_shape=[jax.ShapeDtypeStruct((B, S, M_WIDTH), F32),
                   jax.ShapeDtypeStruct((B, S, M_WIDTH), F32)],
        scratch_shapes=[state, state],
        compiler_params=pltpu.CompilerParams(dimension_semantics=("arbitrary", "arbitrary", "arbitrary"),
                                             vmem_limit_bytes=VMEM_LIMIT),
        name="mlstm",
    )(mq, mk, mv, rows, cols, mq, mk, mv, rows, cols)


def _attn_body(nkc, nt, tq, tk, tv, lam_init, q_ref, k_ref, vt_ref, lq_ref, g_ref, o_ref,
               qz_ref, st0_ref, st1_ref, m_ref, acc_ref):
    def tile_rows(t):
        return pl.ds(pl.multiple_of(t * tq, tq), tq)

    def load_queries(t):
        q = q_ref[0, 0, tile_rows(t), :]
        lane = lax.broadcasted_iota(jnp.int32, q.shape, 1)
        zero = jnp.zeros_like(q)
        qz_ref[0:tq, :] = jnp.where(lane < A_HEAD_DIM, q, zero)
        qz_ref[tq:2 * tq, :] = jnp.where(lane >= A_HEAD_DIM, q, zero)

    def reset_stats():
        m_ref[...] = jnp.full(m_ref.shape, -jnp.inf, F32)
        acc_ref[...] = jnp.zeros_like(acc_ref)

    strips = [slice(c * MXU_N, (c + 1) * MXU_N) for c in range(2 * tq // MXU_N)]
    nv = tk // tv

    def score_strip(j, st_ref, sl):
        kj = k_ref[0, 0, pl.ds(pl.multiple_of(j * tk, tk), tk), :]
        st = lax.dot_general(kj, qz_ref[sl, :], _NT, preferred_element_type=F32)
        st_ref[:, sl] = st
        return jnp.max(st, axis=0, keepdims=True)

    def accumulate_strip(j, st_ref, sl, cm):
        m_old = m_ref[:, sl]
        m_new = jnp.maximum(m_old, cm)
        alpha = jnp.exp2(m_old - m_new)
        p = jnp.exp2(st_ref[:, sl] - m_new)
        pv = jnp.dot(vt_ref[0, j * nv, 0], p[0:tv], preferred_element_type=F32)
        for u in range(1, nv):
            pv = pv + jnp.dot(vt_ref[0, j * nv + u, 0], p[u * tv:(u + 1) * tv],
                              preferred_element_type=F32)
        acc_ref[:, sl] = alpha * acc_ref[:, sl] + pv
        m_ref[:, sl] = m_new

    def step(j_next, st_next_ref, j_cur, st_cur_ref, cmax_cur):
        cmax_next = []
        for sl, cm in zip(strips, cmax_cur):
            cmax_next.append(score_strip(j_next, st_next_ref, sl))
            accumulate_strip(j_cur, st_cur_ref, sl, cm)
        return tuple(cmax_next)

    def pair(i, cmax0):
        j = 2 * i
        cmax1 = step(j + 1, st1_ref, j, st0_ref, cmax0)
        return step(j + 2, st0_ref, j + 1, st1_ref, cmax1)

    def two_pairs(i, cmax0):
        return pair(2 * i + 1, pair(2 * i, cmax0))

    def all_but_last_chunk(cmax0):
        npairs = nkc // 2 - 1
        cmax0 = lax.fori_loop(0, npairs // 2, two_pairs, cmax0)
        if npairs % 2:
            cmax0 = pair(npairs - 1, cmax0)
        return step(nkc - 1, st1_ref, nkc - 2, st0_ref, cmax0)

    lq = lq_ref[...]
    lam = (jnp.exp(jnp.sum(lq[0:1] * lq[1:2], axis=1, keepdims=True))
           - jnp.exp(jnp.sum(lq[2:3] * lq[3:4], axis=1, keepdims=True)) + lam_init)

    def finalize(t):
        accn = acc_ref[0:A_V_DIM, :] / acc_ref[A_V_DIM:A_V_DIM + 1, :]
        ot = accn[:, 0:tq] - lam * accn[:, tq:2 * tq]
        otn = ot * lax.rsqrt(jnp.mean(ot * ot, axis=0, keepdims=True) + EPS)
        o_ref[tile_rows(t), :] = (otn.T * g_ref[...]) * (1.0 - lam_init)

    def tile(t, cmax0):
        cmax1 = all_but_last_chunk(cmax0)
        load_queries(t + 1)
        cmax_next = []
        for sl, cm in zip(strips, cmax1):
            cmax_next.append(score_strip(0, st0_ref, sl))
            accumulate_strip(nkc - 1, st1_ref, sl, cm)
        finalize(t)
        reset_stats()
        return tuple(cmax_next)

    load_queries(0)
    reset_stats()
    cmax0 = tuple(score_strip(0, st0_ref, sl) for sl in strips)
    cmax0 = lax.fori_loop(0, nt - 1, tile, cmax0)
    cmax1 = all_but_last_chunk(cmax0)
    for sl, cm in zip(strips, cmax1):
        accumulate_strip(nkc - 1, st1_ref, sl, cm)
    finalize(nt - 1)


def _attn_call(aq, ak, avt, lq, ag, B, S, tq, nt, tk, tv, lam_init):
    nkc = S // tk
    nq = S // (nt * tq)
    assert nkc >= 2 and nkc % 2 == 0, "key chunks are processed in pairs"
    assert tk % tv == 0 and S % (nt * tq) == 0
    return pl.pallas_call(
        functools.partial(_attn_body, nkc, nt, tq, tk, tv, lam_init),
        grid=(B, A_HEADS, nq),
        in_specs=[pl.BlockSpec((1, 1, nt * tq, LANES), lambda b, h, i: (b, h, i, 0)),
                  pl.BlockSpec((1, 1, S, LANES), lambda b, h, i: (b, h, 0, 0)),
                  pl.BlockSpec((1, S // tv, 1, A_V_ROWS, tv), lambda b, h, i: (b, 0, h, 0, 0)),
                  pl.BlockSpec(lq.shape, lambda b, h, i: (0, 0)),
                  pl.BlockSpec(ag.shape, lambda b, h, i: (0, 0))],
        out_specs=pl.BlockSpec((nt * tq, A_V_DIM), lambda b, h, i: (b * nq + i, h)),
        out_shape=jax.ShapeDtypeStruct((B * S, A_WIDTH), F32),
        scratch_shapes=[pltpu.VMEM((2 * tq, LANES), BF16),
                        pltpu.VMEM((tk, 2 * tq), F32),
                        pltpu.VMEM((tk, 2 * tq), F32),
                        pltpu.VMEM((1, 2 * tq), F32),
                        pltpu.VMEM((A_V_ROWS, 2 * tq), F32)],
        compiler_params=pltpu.CompilerParams(dimension_semantics=("arbitrary", "arbitrary", "arbitrary"),
                                             vmem_limit_bytes=VMEM_LIMIT),
        name="attn",
    )(aq, ak, avt, lq, ag)


def _epilogue_body(x_ref, hf_ref, hb_ref, o_ref, preg_ref, w4_ref, mng_ref, bm_ref, wbm_ref, wba_ref,
                   wout_ref, postg_ref, out_ref):
    x = x_ref[...]
    xn = (_rms(x) * preg_ref[...]).astype(BF16)

    def proj(lo, hi):
        return jnp.dot(xn, w4_ref[:, lo:hi], preferred_element_type=F32)

    h = hf_ref[...] + hb_ref[...]
    mng = mng_ref[...]
    hn = jnp.concatenate(
        [_rms(h[:, k * M_V_DIM:(k + 1) * M_V_DIM]) * mng[:, k * M_V_DIM:(k + 1) * M_V_DIM]
         for k in range(M_HEADS)], axis=1)
    h_a = jax.nn.sigmoid(proj(0, M_WIDTH)) * hn * _silu(proj(M_WIDTH, 2 * M_WIDTH))
    ya = jnp.dot(h_a.astype(BF16), wbm_ref[...], preferred_element_type=F32)
    h_b = o_ref[...] * _silu(proj(2 * M_WIDTH, 2 * M_WIDTH + A_WIDTH))
    yb = jnp.dot(h_b.astype(BF16), wba_ref[...], preferred_element_type=F32)
    g0 = 2 * M_WIDTH + A_WIDTH
    bm = bm_ref[...]
    g_a = jax.nn.sigmoid(proj(g0, g0 + D_MODEL) + bm[:, 0:D_MODEL])
    g_b = jax.nn.sigmoid(proj(g0 + D_MODEL, g0 + 2 * D_MODEL) + bm[:, D_MODEL:2 * D_MODEL])
    y = g_a * ya + g_b * yb
    out = jnp.dot(y.astype(BF16), wout_ref[...], preferred_element_type=F32)
    out_ref[...] = x + _rms(out) * postg_ref[...]


def _epilogue_call(x2d, hf, hb, o, preg, w4, mng, bm, wbm, wba, wout, postg, te):
    T = x2d.shape[0]
    row = lambda i: (i, 0)
    const = lambda i: (0, 0)
    single = pl.Buffered(1)
    return pl.pallas_call(
        _epilogue_body,
        grid=(T // te,),
        in_specs=[pl.BlockSpec((te, D_MODEL), row),
                  pl.BlockSpec((te, M_WIDTH), row),
                  pl.BlockSpec((te, M_WIDTH), row),
                  pl.BlockSpec((te, A_WIDTH), row),
                  pl.BlockSpec(preg.shape, const),
                  pl.BlockSpec(w4.shape, const, pipeline_mode=single),
                  pl.BlockSpec(mng.shape, const),
                  pl.BlockSpec(bm.shape, const),
                  pl.BlockSpec(wbm.shape, const, pipeline_mode=single),
                  pl.BlockSpec(wba.shape, const, pipeline_mode=single),
                  pl.BlockSpec(wout.shape, const, pipeline_mode=single),
                  pl.BlockSpec(postg.shape, const)],
        out_specs=pl.BlockSpec((te, D_MODEL), row),
        out_shape=jax.ShapeDtypeStruct((T, D_MODEL), F32),
        compiler_params=pltpu.CompilerParams(dimension_semantics=("arbitrary",),
                                             vmem_limit_bytes=VMEM_LIMIT),
        name="epilogue",
    )(x2d, hf, hb, o, preg, w4, mng, bm, wbm, wba, wout, postg)


def _rope_tables(S):
    f32 = np.float32
    inv = ROPE_THETA ** (-np.arange(0, ROPE_DIMS, 2, dtype=np.float64) / ROPE_DIMS)
    ang = np.arange(S, dtype=np.float64)[:, None] * inv[None, :]
    cos, sin = np.cos(ang).astype(f32), np.sin(ang).astype(f32)
    one = np.ones((S, A_HEAD_DIM - ROPE_DIMS), f32)
    zero = np.zeros((S, A_HEAD_DIM - ROPE_DIMS), f32)
    rc = np.concatenate([cos, cos, one], axis=1)
    rs = np.concatenate([-sin, sin, zero], axis=1)
    rep = LANES // A_HEAD_DIM
    return jnp.asarray(np.tile(rc, (1, rep))), jnp.asarray(np.tile(rs, (1, rep)))


def _gate_layout(m):
    return jnp.concatenate([m[..., 1, :], m[..., 3, :], m[..., 0, :], m[..., 2, :]], axis=-1)


def _layer(x, layer_idx, pre_g, w_in, conv_w, conv_b, gate_b, m_norm_g, lambda_qk, a_norm_g,
           w_branch_m, w_branch_a, b_merge, w_out, post_g):
    B, S, _ = x.shape
    T = B * S
    tm = min(512, S)
    tq = min(2048, S // 2)
    nt = min(2, S // tq)
    tk = min(512, S // 2)
    tr = min(512, S)
    te = min(256, T)
    lam_init = 0.8 - 0.6 * math.exp(-0.3 * layer_idx)

    cuts = [0]
    for sz in SPLIT_SIZES:
        cuts.append(cuts[-1] + sz)
    col = lambda k: w_in[:, cuts[k]:cuts[k + 1]]
    wmqk = col(0).astype(BF16)
    w1 = jnp.concatenate([col(1), col(5), col(6)], axis=1).astype(BF16)
    wavt = col(7).T.astype(BF16)
    wgt = _gate_layout(col(2).reshape(D_MODEL, 4, M_HEADS)).T.astype(BF16)
    gb = _gate_layout(gate_b).reshape(N_GATE_ROWS, 1).astype(F32)
    w4 = jnp.concatenate([col(3), col(4), col(8), col(9)], axis=1).astype(BF16)
    rc, rs = _rope_tables(S)

    x2d = x.reshape(T, D_MODEL)
    preg = pre_g.reshape(1, D_MODEL)
    mq, mk, mv, gt, aq, ak, avt = _inproj_call(
        x2d, preg, wmqk, w1, wavt, wgt, conv_w, conv_b.reshape(1, 2 * M_QK), rc, rs, B, S, tm)
    rows, cols = _gates_call(gt, gb, B, S)
    hf, hb = _mlstm_call(mq, mk, mv, rows, cols, B, S, tr)
    o = _attn_call(aq, ak, avt, lambda_qk.astype(F32), a_norm_g.reshape(1, A_V_DIM), B, S, tq, nt, tk, tm,
                   lam_init)
    out = _epilogue_call(
        x2d, hf.reshape(T, M_WIDTH), hb.reshape(T, M_WIDTH), o, preg, w4,
        m_norm_g.reshape(1, M_WIDTH), b_merge.reshape(1, 2 * D_MODEL),
        w_branch_m.astype(BF16), w_branch_a.astype(BF16), w_out.astype(BF16),
        post_g.reshape(1, D_MODEL), te)
    return out.reshape(B, S, D_MODEL)


def kernel(x, pre_norm_g, w_in, conv_w, conv_b, gate_b, m_norm_g, lambda_qk, a_norm_g, w_branch_m,
           w_branch_a, b_merge, w_out, post_norm_g):
    for l in range(pre_norm_g.shape[0]):
        x = _layer(x, l, pre_norm_g[l], w_in[l], conv_w[l], conv_b[l], gate_b[l], m_norm_g[l],
                   lambda_qk[l], a_norm_g[l], w_branch_m[l], w_branch_a[l], b_merge[l], w_out[l],
                   post_norm_g[l])
    return x
```

```python
import functools
import math

import jax
import jax.numpy as jnp
import numpy as np
from jax import lax
from jax.experimental import pallas as pl
from jax.experimental.pallas import tpu as pltpu

D_MODEL = 1024
M_HEADS = 4
M_QK_DIM = 128
M_V_DIM = 256
M_QK = M_HEADS * M_QK_DIM
M_WIDTH = M_HEADS * M_V_DIM
M_CHUNK = 64
CONV_WIDTH = 4
A_HEADS = 8
A_HEAD_DIM = 64
A_V_DIM = 2 * A_HEAD_DIM
A_QK = A_HEADS * 2 * A_HEAD_DIM
A_WIDTH = A_HEADS * A_V_DIM
ROPE_THETA = 500000.0
ROPE_DIMS = A_HEAD_DIM // 4
EPS = 1e-6
SPLIT_SIZES = (2 * M_QK, M_WIDTH, 4 * M_HEADS, M_WIDTH, M_WIDTH, A_QK, A_QK, A_WIDTH, A_WIDTH, 2 * D_MODEL)

LANES = 128
MXU_N = 256
F32_ROWS = 8
A_V_ROWS = A_V_DIM + F32_ROWS
HALO = 16
N_GATE_ROWS = 2 * 2 * M_HEADS
VMEM_LIMIT = 60 * 1024 * 1024

F32 = jnp.float32
BF16 = jnp.bfloat16
_NT = (((1,), (1,)), ((), ()))
_TN = (((0,), (0,)), ((), ()))


def _rms(xf):
    return xf * lax.rsqrt(jnp.mean(xf * xf, axis=-1, keepdims=True) + EPS)


def _silu(y):
    return y * jax.nn.sigmoid(y)


def _inproj_body(tps, tm, xprev_ref, x_ref, xnext_ref, preg_ref, wmqk_ref, w1_ref, wavt_ref, wgt_ref,
                 convw_ref, convb_ref, rc_ref, rs_ref,
                 mq_ref, mk_ref, mv_ref, gt_ref, aq_ref, ak_ref, avt_ref, xext_ref):
    t = lax.rem(pl.program_id(0), tps)
    g = preg_ref[...]
    xn = (_rms(x_ref[...]) * g).astype(BF16)
    xp = jnp.where(t != 0, _rms(xprev_ref[...]) * g, 0.0).astype(BF16)
    xq = jnp.where(t != tps - 1, _rms(xnext_ref[...]) * g, 0.0).astype(BF16)
    xext_ref[0:HALO, :] = xp
    xext_ref[HALO:HALO + tm, :] = xn
    xext_ref[HALO + tm:, :] = xq
    n_ext = tm + 2 * HALO
    ext = jnp.dot(xext_ref[...], wmqk_ref[...], preferred_element_type=F32)
    mv = jnp.dot(xn, w1_ref[:, 0:M_WIDTH], preferred_element_type=F32)
    aq = jnp.dot(xn, w1_ref[:, M_WIDTH:M_WIDTH + A_QK], preferred_element_type=F32)
    ak = jnp.dot(xn, w1_ref[:, M_WIDTH + A_QK:M_WIDTH + 2 * A_QK], preferred_element_type=F32)
    avt = lax.dot_general(wavt_ref[...], xn, _NT, preferred_element_type=F32)
    gt_ref[...] = lax.dot_general(wgt_ref[...], xn, _NT, preferred_element_type=F32)

    left = CONV_WIDTH // 2
    y = None
    for j in range(CONV_WIDTH):
        sh = (left - j) % n_ext
        r = ext if sh == 0 else pltpu.roll(ext, sh, axis=0)
        term = convw_ref[j:j + 1, :] * r[HALO:HALO + tm, :]
        y = term if y is None else y + term
    y = _silu(y + convb_ref[...])
    qscale = M_QK_DIM ** -0.5
    for h in range(M_HEADS):
        mq_ref[0, h] = (y[:, h * M_QK_DIM:(h + 1) * M_QK_DIM] * qscale).astype(BF16)
        mk_ref[0, h] = y[:, M_QK + h * M_QK_DIM:M_QK + (h + 1) * M_QK_DIM].astype(BF16)
    for h in range(M_HEADS):
        mv_ref[0, h] = mv[:, h * M_V_DIM:(h + 1) * M_V_DIM].astype(BF16)
    rc, rs = rc_ref[...], rs_ref[...]
    half = ROPE_DIMS // 2
    first_half = (lax.broadcasted_iota(jnp.int32, rc.shape, 1) & (A_HEAD_DIM - 1)) < half

    def rope(a):
        partner = jnp.where(first_half, pltpu.roll(a, LANES - half, axis=1), pltpu.roll(a, half, axis=1))
        return a * rc + partner * rs

    ascale = A_HEAD_DIM ** -0.5 * math.log2(math.e)
    for h in range(A_HEADS):
        aq_ref[0, h] = (rope(aq[:, h * LANES:(h + 1) * LANES]) * ascale).astype(BF16)
        ak_ref[0, h] = rope(ak[:, h * LANES:(h + 1) * LANES]).astype(BF16)
    ones_tile = (lax.broadcasted_iota(jnp.int32, (F32_ROWS, tm), 0) == 0).astype(F32)
    for h in range(A_HEADS):
        avt_ref[0, 0, h, 0:A_V_DIM, :] = avt[h * A_V_DIM:(h + 1) * A_V_DIM, :].astype(BF16).astype(F32)
        avt_ref[0, 0, h, A_V_DIM:A_V_ROWS, :] = ones_tile


def _inproj_call(x2d, preg, wmqk, w1, wavt, wgt, convw, convb, rc, rs, B, S, tm):
    T = B * S
    tps = S // tm
    nh = tm // HALO
    const = lambda i: (0, 0)
    in_specs = [
        pl.BlockSpec((HALO, D_MODEL), lambda i: (jnp.maximum(i * nh - 1, 0), 0)),
        pl.BlockSpec((tm, D_MODEL), lambda i: (i, 0)),
        pl.BlockSpec((HALO, D_MODEL), lambda i: (jnp.minimum((i + 1) * nh, T // HALO - 1), 0)),
        pl.BlockSpec((1, D_MODEL), const),
        pl.BlockSpec(wmqk.shape, const),
        pl.BlockSpec(w1.shape, const),
        pl.BlockSpec(wavt.shape, const),
        pl.BlockSpec(wgt.shape, const),
        pl.BlockSpec(convw.shape, const),
        pl.BlockSpec(convb.shape, const),
        pl.BlockSpec((tm, LANES), lambda i: (lax.rem(i, tps), 0)),
        pl.BlockSpec((tm, LANES), lambda i: (lax.rem(i, tps), 0)),
    ]
    bt = lambda i: (i // tps, 0, lax.rem(i, tps), 0)
    out_shape = [
        jax.ShapeDtypeStruct((B, M_HEADS, S, M_QK_DIM), BF16),
        jax.ShapeDtypeStruct((B, M_HEADS, S, M_QK_DIM), BF16),
        jax.ShapeDtypeStruct((B, M_HEADS, S, M_V_DIM), BF16),
        jax.ShapeDtypeStruct((N_GATE_ROWS, T), F32),
        jax.ShapeDtypeStruct((B, A_HEADS, S, LANES), BF16),
        jax.ShapeDtypeStruct((B, A_HEADS, S, LANES), BF16),
        jax.ShapeDtypeStruct((B, tps, A_HEADS, A_V_ROWS, tm), F32),
    ]
    out_specs = [
        pl.BlockSpec((1, M_HEADS, tm, M_QK_DIM), bt),
        pl.BlockSpec((1, M_HEADS, tm, M_QK_DIM), bt),
        pl.BlockSpec((1, M_HEADS, tm, M_V_DIM), bt),
        pl.BlockSpec((N_GATE_ROWS, tm), lambda i: (0, i)),
        pl.BlockSpec((1, A_HEADS, tm, LANES), bt),
        pl.BlockSpec((1, A_HEADS, tm, LANES), bt),
        pl.BlockSpec((1, 1, A_HEADS, A_V_ROWS, tm), lambda i: (i // tps, lax.rem(i, tps), 0, 0, 0)),
    ]
    return pl.pallas_call(
        functools.partial(_inproj_body, tps, tm),
        grid=(T // tm,),
        in_specs=in_specs,
        out_specs=out_specs,
        out_shape=out_shape,
        scratch_shapes=[pltpu.VMEM((tm + 2 * HALO, D_MODEL), BF16)],
        compiler_params=pltpu.CompilerParams(dimension_semantics=("arbitrary",),
                                             vmem_limit_bytes=VMEM_LIMIT),
        name="inproj",
    )(x2d, x2d, x2d, preg, wmqk, w1, wavt, wgt, convw, convb, rc, rs)


N_GATE_COLS = 5


def _gates_body(S, ct, f_ref, i_ref, fb_ref, ib_ref, rows_ref, cols_ref, scan_ref, stage_ref):
    h = pl.program_id(1)

    @pl.when(h == 0)
    def _():
        _gate_scans(S, f_ref, i_ref, fb_ref, ib_ref, scan_ref)

    rows_ref[...] = jnp.zeros_like(rows_ref)
    stage_ref[...] = jnp.zeros_like(stage_ref)
    for d in range(2):
        rows_ref[0, 0, d:d + 1, :] = scan_ref[N_GATE_COLS, pl.ds(d * M_HEADS + h, 1), :]
    for t in range(S // ct):
        sl = slice(t * ct, (t + 1) * ct)
        for k in range(N_GATE_COLS):
            for d in range(2):
                stage_ref[2 * k + d:2 * k + d + 1, :] = scan_ref[k, pl.ds(d * M_HEADS + h, 1), sl]
        cols_ref[0, 0, sl, :] = stage_ref[...].T


def _gate_scans(S, f_ref, i_ref, fb_ref, ib_ref, scan_ref):
    L = M_CHUNK
    f = f_ref[...] + fb_ref[...]
    ig = i_ref[...] + ib_ref[...]
    lf = jnp.minimum(f, 0.0) - jnp.log1p(jnp.exp(-jnp.abs(f)))
    lane = lax.broadcasted_iota(jnp.int32, f.shape, 1)
    pos = lane & (L - 1)
    fwd = lax.broadcasted_iota(jnp.int32, f.shape, 0) < M_HEADS

    def from_left(x, s):
        return pltpu.roll(x, s, axis=1)

    def from_right(x, s):
        return pltpu.roll(x, S - s, axis=1)

    def chunk_scans(x, op, ident):
        pre, suf = x, x
        s = 1
        while s < L:
            pre = op(pre, jnp.where(pos >= s, from_left(pre, s), ident))
            suf = op(suf, jnp.where(pos < L - s, from_right(suf, s), ident))
            s *= 2
        return pre, suf

    pre, suf = chunk_scans(lf, jnp.add, 0.0)
    a = jnp.where(fwd, pre, suf)
    a_tot = pre + suf - lf
    w = ig - a
    wpre, wsuf = chunk_scans(w, jnp.maximum, -jnp.inf)
    rowmax = a + jnp.where(fwd, wpre, wsuf)
    g_max = a_tot + jnp.maximum(wpre, wsuf)

    af, gf, ab, gb = a_tot, g_max, a_tot, g_max
    s = L
    while s < S:
        ok = lane >= s
        a_sh, g_sh = from_left(af, s), from_left(gf, s)
        gf = jnp.where(ok, jnp.maximum(g_sh + af, gf), gf)
        af = jnp.where(ok, af + a_sh, af)
        ok = lane < S - s
        a_sh, g_sh = from_right(ab, s), from_right(gb, s)
        gb = jnp.where(ok, jnp.maximum(g_sh + ab, gb), gb)
        ab = jnp.where(ok, ab + a_sh, ab)
        s *= 2
    m_f = jnp.maximum(af, gf)
    m_b = jnp.maximum(ab, gb)
    m_prev = jnp.where(fwd, jnp.where(lane >= L, from_left(m_f, L), 0.0),
                       jnp.where(lane < S - L, from_right(m_b, L), 0.0))

    inter = a + m_prev
    mj = jnp.maximum(inter, rowmax)
    m_new = jnp.maximum(a_tot + m_prev, g_max)
    scan_ref[0] = a - mj
    scan_ref[1] = jnp.exp(inter - mj)
    scan_ref[2] = jnp.exp(-mj)
    scan_ref[3] = jnp.exp(a_tot + w - m_new)
    scan_ref[4] = jnp.exp(a_tot + m_prev - m_new)
    scan_ref[N_GATE_COLS] = w


def _gates_call(gt, gb, B, S):
    ct = min(2048, S)
    return pl.pallas_call(
        functools.partial(_gates_body, S, ct),
        grid=(B, M_HEADS),
        in_specs=[pl.BlockSpec((8, S), lambda b, h: (0, b)),
                  pl.BlockSpec((8, S), lambda b, h: (1, b)),
                  pl.BlockSpec((8, 1), lambda b, h: (0, 0)),
                  pl.BlockSpec((8, 1), lambda b, h: (1, 0))],
        out_specs=[pl.BlockSpec((1, 1, 8, S), lambda b, h: (b, h, 0, 0)),
                   pl.BlockSpec((1, 1, S, LANES), lambda b, h: (b, h, 0, 0))],
        out_shape=[jax.ShapeDtypeStruct((B, M_HEADS, 8, S), F32),
                   jax.ShapeDtypeStruct((B, M_HEADS, S, LANES), F32)],
        scratch_shapes=[pltpu.VMEM((N_GATE_COLS + 1, 8, S), F32),
                        pltpu.VMEM((LANES, ct), F32)],
        compiler_params=pltpu.CompilerParams(dimension_semantics=("arbitrary", "arbitrary"),
                                             vmem_limit_bytes=VMEM_LIMIT),
        name="gates",
    )(gt, gt, gb, gb)


class _MlstmScan:
    def __init__(self, rev, nchunk, q_ref, k_ref, v_ref, rows_ref, cols_ref, h_ref, c_ref):
        self.rev, self.d = rev, (1 if rev else 0)
        self.q_ref, self.k_ref, self.v_ref = q_ref, k_ref, v_ref
        self.rows_ref, self.cols_ref, self.h_ref, self.c_ref = rows_ref, cols_ref, h_ref, c_ref
        self.order = list(range(nchunk - 1, -1, -1) if rev else range(nchunk))
        L = M_CHUNK
        row_i = lax.broadcasted_iota(jnp.int32, (L, L), 0)
        col_i = lax.broadcasted_iota(jnp.int32, (L, L), 1)
        self.visible = (col_i >= row_i) if rev else (col_i <= row_i)
        self.ones_col = (lax.broadcasted_iota(jnp.int32, (L, LANES), 1) == 0).astype(BF16)

    @staticmethod
    def rows_of(c):
        return slice(c * M_CHUNK, (c + 1) * M_CHUNK)

    def gate_col(self, c, k):
        return self.cols_ref[0, 0, self.rows_of(c), 2 * k + self.d:2 * k + self.d + 1]

    def v_ext(self, c):
        return jnp.concatenate([self.v_ref[0, 0, self.rows_of(c), :], self.ones_col], axis=1)

    def input_products(self):
        self.qk, self.kv = {}, {}
        for c in self.order:
            q, k = self.q_ref[0, 0, self.rows_of(c), :], self.k_ref[0, 0, self.rows_of(c), :]
            self.qk[c] = lax.dot_general(q, k, _NT, preferred_element_type=F32)
            wv = (self.gate_col(c, 3) * self.v_ext(c).astype(F32)).astype(BF16)
            self.kv[c] = lax.dot_general(k, wv, _TN, preferred_element_type=F32)

    def states(self):
        C = self.c_ref[...]
        self.c_in = {}
        for c in self.order:
            self.c_in[c] = C.astype(BF16)
            C = self.gate_col(c, 4)[0:1, :] * C + self.kv[c]
        self.c_ref[...] = C

    def read_out(self):
        for c in self.order:
            u, w_inter, e = self.gate_col(c, 0), self.gate_col(c, 1), self.gate_col(c, 2)
            w_row = self.rows_ref[0, 0, self.d:self.d + 1, self.rows_of(c)]
            p = jnp.exp(jnp.where(self.visible, u + w_row, -jnp.inf))
            s = self.qk[c] * p
            q = self.q_ref[0, 0, self.rows_of(c), :]
            num = (w_inter * jnp.dot(q, self.c_in[c], preferred_element_type=F32)
                   + jnp.dot(s.astype(BF16), self.v_ext(c), preferred_element_type=F32))
            den = num[:, M_V_DIM:M_V_DIM + 1]
            self.h_ref[0, self.rows_of(c), :] = num[:, 0:M_V_DIM] / jnp.maximum(jnp.abs(den), e)


def _mlstm_body(nchunk, qf_ref, kf_ref, vf_ref, rowsf_ref, colsf_ref, qb_ref, kb_ref, vb_ref, rowsb_ref,
                colsb_ref, hf_ref, hb_ref, cf_ref, cb_ref):
    @pl.when(pl.program_id(2) == 0)
    def _():
        cf_ref[...] = jnp.zeros_like(cf_ref)
        cb_ref[...] = jnp.zeros_like(cb_ref)

    fwd = _MlstmScan(False, nchunk, qf_ref, kf_ref, vf_ref, rowsf_ref, colsf_ref, hf_ref, cf_ref)
    bwd = _MlstmScan(True, nchunk, qb_ref, kb_ref, vb_ref, rowsb_ref, colsb_ref, hb_ref, cb_ref)
    fwd.input_products()
    bwd.input_products()
    fwd.states()
    bwd.states()
    fwd.read_out()
    bwd.read_out()


def _mlstm_call(mq, mk, mv, rows, cols, B, S, tr):
    ns = S // tr

    def specs(pos):
        return [pl.BlockSpec((1, 1, tr, M_QK_DIM), lambda b, h, s: (b, h, pos(s), 0)),
                pl.BlockSpec((1, 1, tr, M_QK_DIM), lambda b, h, s: (b, h, pos(s), 0)),
                pl.BlockSpec((1, 1, tr, M_V_DIM), lambda b, h, s: (b, h, pos(s), 0)),
                pl.BlockSpec((1, 1, 8, tr), lambda b, h, s: (b, h, 0, pos(s))),
                pl.BlockSpec((1, 1, tr, LANES), lambda b, h, s: (b, h, pos(s), 0))]

    fpos = lambda s: s
    bpos = lambda s: ns - 1 - s
    state = pltpu.VMEM((M_QK_DIM, M_V_DIM + LANES), F32)
    return pl.pallas_call(
        functools.partial(_mlstm_body, tr // M_CHUNK),
        grid=(B, M_HEADS, ns),
        in_specs=specs(fpos) + specs(bpos),
        out_specs=[pl.BlockSpec((1, tr, M_V_DIM), lambda b, h, s: (b, fpos(s), h)),
                   pl.BlockSpec((1, tr, M_V_DIM), lambda b, h, s: (b, bpos(s), h))],
        out_shape=[jax.ShapeDtypeStruct((B, S, M_WIDTH), F32),
                   jax.ShapeDtypeStruct((B, S, M_WIDTH), F32)],
        scratch_shapes=[state, state],
        compiler_params=pltpu.CompilerParams(dimension_semantics=("arbitrary", "arbitrary", "arbitrary"),
                                             vmem_limit_bytes=VMEM_LIMIT),
        name="mlstm",
    )(mq, mk, mv, rows, cols, mq, mk, mv, rows, cols)


def _attn_body(nkc, nt, tq, tk, tv, lam_init, q_ref, k_ref, vt_ref, lq_ref, g_ref, o_ref,
               qz_ref, st0_ref, st1_ref, m_ref, acc_ref):
    def tile_rows(t):
        return pl.ds(pl.multiple_of(t * tq, tq), tq)

    def load_queries(t):
        q = q_ref[0, 0, tile_rows(t), :]
        lane = lax.broadcasted_iota(jnp.int32, q.shape, 1)
        zero = jnp.zeros_like(q)
        qz_ref[0:tq, :] = jnp.where(lane < A_HEAD_DIM, q, zero)
        qz_ref[tq:2 * tq, :] = jnp.where(lane >= A_HEAD_DIM, q, zero)

    def reset_stats():
        m_ref[...] = jnp.full(m_ref.shape, -jnp.inf, F32)
        acc_ref[...] = jnp.zeros_like(acc_ref)

    strips = [slice(c * MXU_N, (c + 1) * MXU_N) for c in range(2 * tq // MXU_N)]
    nv = tk // tv

    def score_strip(j, st_ref, sl):
        kj = k_ref[0, 0, pl.ds(pl.multiple_of(j * tk, tk), tk), :]
        st = lax.dot_general(kj, qz_ref[sl, :], _NT, preferred_element_type=F32)
        st_ref[:, sl] = st
        return jnp.max(st, axis=0, keepdims=True)

    def accumulate_strip(j, st_ref, sl, cm):
        m_old = m_ref[:, sl]
        m_new = jnp.maximum(m_old, cm)
        alpha = jnp.exp2(m_old - m_new)
        p = jnp.exp2(st_ref[:, sl] - m_new)
        pv = jnp.dot(vt_ref[0, j * nv, 0], p[0:tv], preferred_element_type=F32)
        for u in range(1, nv):
            pv = pv + jnp.dot(vt_ref[0, j * nv + u, 0], p[u * tv:(u + 1) * tv],
                              preferred_element_type=F32)
        acc_ref[:, sl] = alpha * acc_ref[:, sl] + pv
        m_ref[:, sl] = m_new

    def step(j_next, st_next_ref, j_cur, st_cur_ref, cmax_cur):
        cmax_next = []
        for sl, cm in zip(strips, cmax_cur):
            cmax_next.append(score_strip(j_next, st_next_ref, sl))
            accumulate_strip(j_cur, st_cur_ref, sl, cm)
        return tuple(cmax_next)

    def pair(i, cmax0):
        j = 2 * i
        cmax1 = step(j + 1, st1_ref, j, st0_ref, cmax0)
        return step(j + 2, st0_ref, j + 1, st1_ref, cmax1)

    def two_pairs(i, cmax0):
        return pair(2 * i + 1, pair(2 * i, cmax0))

    def all_but_last_chunk(cmax0):
        npairs = nkc // 2 - 1
        cmax0 = lax.fori_loop(0, npairs // 2, two_pairs, cmax0)
        if npairs % 2:
            cmax0 = pair(npairs - 1, cmax0)
        return step(nkc - 1, st1_ref, nkc - 2, st0_ref, cmax0)

    lq = lq_ref[...]
    lam = (jnp.exp(jnp.sum(lq[0:1] * lq[1:2], axis=1, keepdims=True))
           - jnp.exp(jnp.sum(lq[2:3] * lq[3:4], axis=1, keepdims=True)) + lam_init)

    def finalize(t):
        accn = acc_ref[0:A_V_DIM, :] / acc_ref[A_V_DIM:A_V_DIM + 1, :]
        ot = accn[:, 0:tq] - lam * accn[:, tq:2 * tq]
        otn = ot * lax.rsqrt(jnp.mean(ot * ot, axis=0, keepdims=True) + EPS)
        o_ref[tile_rows(t), :] = (otn.T * g_ref[...]) * (1.0 - lam_init)

    def tile(t, cmax0):
        cmax1 = all_but_last_chunk(cmax0)
        load_queries(t + 1)
        cmax_next = []
        for sl, cm in zip(strips, cmax1):
            cmax_next.append(score_strip(0, st0_ref, sl))
            accumulate_strip(nkc - 1, st1_ref, sl, cm)
        finalize(t)
        reset_stats()
        return tuple(cmax_next)

    load_queries(0)
    reset_stats()
    cmax0 = tuple(score_strip(0, st0_ref, sl) for sl in strips)
    cmax0 = lax.fori_loop(0, nt - 1, tile, cmax0)
    cmax1 = all_but_last_chunk(cmax0)
    for sl, cm in zip(strips, cmax1):
        accumulate_strip(nkc - 1, st1_ref, sl, cm)
    finalize(nt - 1)


def _attn_call(aq, ak, avt, lq, ag, B, S, tq, nt, tk, tv, lam_init):
    nkc = S // tk
    nq = S // (nt * tq)
    assert nkc >= 2 and nkc % 2 == 0, "key chunks are processed in pairs"
    assert tk % tv == 0 and S % (nt * tq) == 0
    return pl.pallas_call(
        functools.partial(_attn_body, nkc, nt, tq, tk, tv, lam_init),
        grid=(B, A_HEADS, nq),
        in_specs=[pl.BlockSpec((1, 1, nt * tq, LANES), lambda b, h, i: (b, h, i, 0)),
                  pl.BlockSpec((1, 1, S, LANES), lambda b, h, i: (b, h, 0, 0)),
                  pl.BlockSpec((1, S // tv, 1, A_V_ROWS, tv), lambda b, h, i: (b, 0, h, 0, 0)),
                  pl.BlockSpec(lq.shape, lambda b, h, i: (0, 0)),
                  pl.BlockSpec(ag.shape, lambda b, h, i: (0, 0))],
        out_specs=pl.BlockSpec((nt * tq, A_V_DIM), lambda b, h, i: (b * nq + i, h)),
        out_shape=jax.ShapeDtypeStruct((B * S, A_WIDTH), F32),
        scratch_shapes=[pltpu.VMEM((2 * tq, LANES), BF16),
                        pltpu.VMEM((tk, 2 * tq), F32),
                        pltpu.VMEM((tk, 2 * tq), F32),
                        pltpu.VMEM((1, 2 * tq), F32),
                        pltpu.VMEM((A_V_ROWS, 2 * tq), F32)],
        compiler_params=pltpu.CompilerParams(dimension_semantics=("arbitrary", "arbitrary", "arbitrary"),
                                             vmem_limit_bytes=VMEM_LIMIT),
        name="attn",
    )(aq, ak, avt, lq, ag)


def _epilogue_body(x_ref, hf_ref, hb_ref, o_ref, preg_ref, w4_ref, mng_ref, bm_ref, wbm_ref, wba_ref,
                   wout_ref, postg_ref, out_ref):
    x = x_ref[...]
    xn = (_rms(x) * preg_ref[...]).astype(BF16)

    def proj(lo, hi):
        return jnp.dot(xn, w4_ref[:, lo:hi], preferred_element_type=F32)

    h = hf_ref[...] + hb_ref[...]
    mng = mng_ref[...]
    hn = jnp.concatenate(
        [_rms(h[:, k * M_V_DIM:(k + 1) * M_V_DIM]) * mng[:, k * M_V_DIM:(k + 1) * M_V_DIM]
         for k in range(M_HEADS)], axis=1)
    h_a = jax.nn.sigmoid(proj(0, M_WIDTH)) * hn * _silu(proj(M_WIDTH, 2 * M_WIDTH))
    ya = jnp.dot(h_a.astype(BF16), wbm_ref[...], preferred_element_type=F32)
    h_b = o_ref[...] * _silu(proj(2 * M_WIDTH, 2 * M_WIDTH + A_WIDTH))
    yb = jnp.dot(h_b.astype(BF16), wba_ref[...], preferred_element_type=F32)
    g0 = 2 * M_WIDTH + A_WIDTH
    bm = bm_ref[...]
    g_a = jax.nn.sigmoid(proj(g0, g0 + D_MODEL) + bm[:, 0:D_MODEL])
    g_b = jax.nn.sigmoid(proj(g0 + D_MODEL, g0 + 2 * D_MODEL) + bm[:, D_MODEL:2 * D_MODEL])
    y = g_a * ya + g_b * yb
    out = jnp.dot(y.astype(BF16), wout_ref[...], preferred_element_type=F32)
    out_ref[...] = x + _rms(out) * postg_ref[...]


def _epilogue_call(x2d, hf, hb, o, preg, w4, mng, bm, wbm, wba, wout, postg, te):
    T = x2d.shape[0]
    row = lambda i: (i, 0)
    const = lambda i: (0, 0)
    single = pl.Buffered(1)
    return pl.pallas_call(
        _epilogue_body,
        grid=(T // te,),
        in_specs=[pl.BlockSpec((te, D_MODEL), row),
                  pl.BlockSpec((te, M_WIDTH), row),
                  pl.BlockSpec((te, M_WIDTH), row),
                  pl.BlockSpec((te, A_WIDTH), row),
                  pl.BlockSpec(preg.shape, const),
                  pl.BlockSpec(w4.shape, const, pipeline_mode=single),
                  pl.BlockSpec(mng.shape, const),
                  pl.BlockSpec(bm.shape, const),
                  pl.BlockSpec(wbm.shape, const, pipeline_mode=single),
                  pl.BlockSpec(wba.shape, const, pipeline_mode=single),
                  pl.BlockSpec(wout.shape, const, pipeline_mode=single),
                  pl.BlockSpec(postg.shape, const)],
        out_specs=pl.BlockSpec((te, D_MODEL), row),
        out_shape=jax.ShapeDtypeStruct((T, D_MODEL), F32),
        compiler_params=pltpu.CompilerParams(dimension_semantics=("arbitrary",),
                                             vmem_limit_bytes=VMEM_LIMIT),
        name="epilogue",
    )(x2d, hf, hb, o, preg, w4, mng, bm, wbm, wba, wout, postg)


def _rope_tables(S):
    f32 = np.float32
    inv = ROPE_THETA ** (-np.arange(0, ROPE_DIMS, 2, dtype=np.float64) / ROPE_DIMS)
    ang = np.arange(S, dtype=np.float64)[:, None] * inv[None, :]
    cos, sin = np.cos(ang).astype(f32), np.sin(ang).astype(f32)
    one = np.ones((S, A_HEAD_DIM - ROPE_DIMS), f32)
    zero = np.zeros((S, A_HEAD_DIM - ROPE_DIMS), f32)
    rc = np.concatenate([cos, cos, one], axis=1)
    rs = np.concatenate([-sin, sin, zero], axis=1)
    rep = LANES // A_HEAD_DIM
    return jnp.asarray(np.tile(rc, (1, rep))), jnp.asarray(np.tile(rs, (1, rep)))


def _gate_layout(m):
    return jnp.concatenate([m[..., 1, :], m[..., 3, :], m[..., 0, :], m[..., 2, :]], axis=-1)


def _layer(x, layer_idx, pre_g, w_in, conv_w, conv_b, gate_b, m_norm_g, lambda_qk, a_norm_g,
           w_branch_m, w_branch_a, b_merge, w_out, post_g):
    B, S, _ = x.shape
    T = B * S
    tm = min(512, S)
    tq = min(2048, S // 2)
    nt = min(2, S // tq)
    tk = min(512, S // 2)
    tr = min(1024, S)
    te = min(256, T)
    lam_init = 0.8 - 0.6 * math.exp(-0.3 * layer_idx)

    cuts = [0]
    for sz in SPLIT_SIZES:
        cuts.append(cuts[-1] + sz)
    col = lambda k: w_in[:, cuts[k]:cuts[k + 1]]
    wmqk = col(0).astype(BF16)
    w1 = jnp.concatenate([col(1), col(5), col(6)], axis=1).astype(BF16)
    wavt = col(7).T.astype(BF16)
    wgt = _gate_layout(col(2).reshape(D_MODEL, 4, M_HEADS)).T.astype(BF16)
    gb = _gate_layout(gate_b).reshape(N_GATE_ROWS, 1).astype(F32)
    w4 = jnp.concatenate([col(3), col(4), col(8), col(9)], axis=1).astype(BF16)
    rc, rs = _rope_tables(S)

    x2d = x.reshape(T, D_MODEL)
    preg = pre_g.reshape(1, D_MODEL)
    mq, mk, mv, gt, aq, ak, avt = _inproj_call(
        x2d, preg, wmqk, w1, wavt, wgt, conv_w, conv_b.reshape(1, 2 * M_QK), rc, rs, B, S, tm)
    rows, cols = _gates_call(gt, gb, B, S)
    hf, hb = _mlstm_call(mq, mk, mv, rows, cols, B, S, tr)
    o = _attn_call(aq, ak, avt, lambda_qk.astype(F32), a_norm_g.reshape(1, A_V_DIM), B, S, tq, nt, tk, tm,
                   lam_init)
    out = _epilogue_call(
        x2d, hf.reshape(T, M_WIDTH), hb.reshape(T, M_WIDTH), o, preg, w4,
        m_norm_g.reshape(1, M_WIDTH), b_merge.reshape(1, 2 * D_MODEL),
        w_branch_m.astype(BF16), w_branch_a.astype(BF16), w_out.astype(BF16),
        post_g.reshape(1, D_MODEL), te)
    return out.reshape(B, S, D_MODEL)


def kernel(x, pre_norm_g, w_in, conv_w, conv_b, gate_b, m_norm_g, lambda_qk, a_norm_g, w_branch_m,
           w_branch_a, b_merge, w_out, post_norm_g):
    for l in range(pre_norm_g.shape[0]):
        x = _layer(x, l, pre_norm_g[l], w_in[l], conv_w[l], conv_b[l], gate_b[l], m_norm_g[l],
                   lambda_qk[l], a_norm_g[l], w_branch_m[l], w_branch_a[l], b_merge[l], w_out[l],
                   post_norm_g[l])
    return x
```

```python
import functools
import math

import jax
import jax.numpy as jnp
import numpy as np
from jax import lax
from jax.experimental import pallas as pl
from jax.experimental.pallas import tpu as pltpu

D_MODEL = 1024
M_HEADS = 4
M_QK_DIM = 128
M_V_DIM = 256
M_QK = M_HEADS * M_QK_DIM
M_WIDTH = M_HEADS * M_V_DIM
M_CHUNK = 64
CONV_WIDTH = 4
A_HEADS = 8
A_HEAD_DIM = 64
A_V_DIM = 2 * A_HEAD_DIM
A_QK = A_HEADS * 2 * A_HEAD_DIM
A_WIDTH = A_HEADS * A_V_DIM
ROPE_THETA = 500000.0
ROPE_DIMS = A_HEAD_DIM // 4
EPS = 1e-6
SPLIT_SIZES = (2 * M_QK, M_WIDTH, 4 * M_HEADS, M_WIDTH, M_WIDTH, A_QK, A_QK, A_WIDTH, A_WIDTH, 2 * D_MODEL)

LANES = 128
MXU_N = 256
F32_ROWS = 8
A_V_ROWS = A_V_DIM + F32_ROWS
HALO = 16
N_GATE_ROWS = 2 * 2 * M_HEADS
VMEM_LIMIT = 60 * 1024 * 1024

F32 = jnp.float32
BF16 = jnp.bfloat16
_NT = (((1,), (1,)), ((), ()))
_TN = (((0,), (0,)), ((), ()))


def _rms(xf):
    return xf * lax.rsqrt(jnp.mean(xf * xf, axis=-1, keepdims=True) + EPS)


def _silu(y):
    return y * jax.nn.sigmoid(y)


def _inproj_body(tps, tm, xprev_ref, x_ref, xnext_ref, preg_ref, wmqk_ref, w1_ref, wavt_ref, wgt_ref,
                 convw_ref, convb_ref, rc_ref, rs_ref,
                 mq_ref, mk_ref, mv_ref, gt_ref, aq_ref, ak_ref, avt_ref, xext_ref):
    t = lax.rem(pl.program_id(0), tps)
    g = preg_ref[...]
    xn = (_rms(x_ref[...]) * g).astype(BF16)
    xp = jnp.where(t != 0, _rms(xprev_ref[...]) * g, 0.0).astype(BF16)
    xq = jnp.where(t != tps - 1, _rms(xnext_ref[...]) * g, 0.0).astype(BF16)
    xext_ref[0:HALO, :] = xp
    xext_ref[HALO:HALO + tm, :] = xn
    xext_ref[HALO + tm:, :] = xq
    n_ext = tm + 2 * HALO
    ext = jnp.dot(xext_ref[...], wmqk_ref[...], preferred_element_type=F32)
    mv = jnp.dot(xn, w1_ref[:, 0:M_WIDTH], preferred_element_type=F32)
    aq = jnp.dot(xn, w1_ref[:, M_WIDTH:M_WIDTH + A_QK], preferred_element_type=F32)
    ak = jnp.dot(xn, w1_ref[:, M_WIDTH + A_QK:M_WIDTH + 2 * A_QK], preferred_element_type=F32)
    avt = lax.dot_general(wavt_ref[...], xn, _NT, preferred_element_type=F32)
    gt_ref[...] = lax.dot_general(wgt_ref[...], xn, _NT, preferred_element_type=F32)

    left = CONV_WIDTH // 2
    y = None
    for j in range(CONV_WIDTH):
        sh = (left - j) % n_ext
        r = ext if sh == 0 else pltpu.roll(ext, sh, axis=0)
        term = convw_ref[j:j + 1, :] * r[HALO:HALO + tm, :]
        y = term if y is None else y + term
    y = _silu(y + convb_ref[...])
    qscale = M_QK_DIM ** -0.5
    for h in range(M_HEADS):
        mq_ref[0, h] = (y[:, h * M_QK_DIM:(h + 1) * M_QK_DIM] * qscale).astype(BF16)
        mk_ref[0, h] = y[:, M_QK + h * M_QK_DIM:M_QK + (h + 1) * M_QK_DIM].astype(BF16)
    for h in range(M_HEADS):
        mv_ref[0, h] = mv[:, h * M_V_DIM:(h + 1) * M_V_DIM].astype(BF16)
    rc, rs = rc_ref[...], rs_ref[...]
    half = ROPE_DIMS // 2
    first_half = (lax.broadcasted_iota(jnp.int32, rc.shape, 1) & (A_HEAD_DIM - 1)) < half

    def rope(a):
        partner = jnp.where(first_half, pltpu.roll(a, LANES - half, axis=1), pltpu.roll(a, half, axis=1))
        return a * rc + partner * rs

    ascale = A_HEAD_DIM ** -0.5 * math.log2(math.e)
    for h in range(A_HEADS):
        aq_ref[0, h] = (rope(aq[:, h * LANES:(h + 1) * LANES]) * ascale).astype(BF16)
        ak_ref[0, h] = rope(ak[:, h * LANES:(h + 1) * LANES]).astype(BF16)
    ones_tile = (lax.broadcasted_iota(jnp.int32, (F32_ROWS, tm), 0) == 0).astype(F32)
    for h in range(A_HEADS):
        avt_ref[0, 0, h, 0:A_V_DIM, :] = avt[h * A_V_DIM:(h + 1) * A_V_DIM, :].astype(BF16).astype(F32)
        avt_ref[0, 0, h, A_V_DIM:A_V_ROWS, :] = ones_tile


def _inproj_call(x2d, preg, wmqk, w1, wavt, wgt, convw, convb, rc, rs, B, S, tm):
    T = B * S
    tps = S // tm
    nh = tm // HALO
    const = lambda i: (0, 0)
    in_specs = [
        pl.BlockSpec((HALO, D_MODEL), lambda i: (jnp.maximum(i * nh - 1, 0), 0)),
        pl.BlockSpec((tm, D_MODEL), lambda i: (i, 0)),
        pl.BlockSpec((HALO, D_MODEL), lambda i: (jnp.minimum((i + 1) * nh, T // HALO - 1), 0)),
        pl.BlockSpec((1, D_MODEL), const),
        pl.BlockSpec(wmqk.shape, const),
        pl.BlockSpec(w1.shape, const),
        pl.BlockSpec(wavt.shape, const),
        pl.BlockSpec(wgt.shape, const),
        pl.BlockSpec(convw.shape, const),
        pl.BlockSpec(convb.shape, const),
        pl.BlockSpec((tm, LANES), lambda i: (lax.rem(i, tps), 0)),
        pl.BlockSpec((tm, LANES), lambda i: (lax.rem(i, tps), 0)),
    ]
    bt = lambda i: (i // tps, 0, lax.rem(i, tps), 0)
    out_shape = [
        jax.ShapeDtypeStruct((B, M_HEADS, S, M_QK_DIM), BF16),
        jax.ShapeDtypeStruct((B, M_HEADS, S, M_QK_DIM), BF16),
        jax.ShapeDtypeStruct((B, M_HEADS, S, M_V_DIM), BF16),
        jax.ShapeDtypeStruct((N_GATE_ROWS, T), F32),
        jax.ShapeDtypeStruct((B, A_HEADS, S, LANES), BF16),
        jax.ShapeDtypeStruct((B, A_HEADS, S, LANES), BF16),
        jax.ShapeDtypeStruct((B, tps, A_HEADS, A_V_ROWS, tm), F32),
    ]
    out_specs = [
        pl.BlockSpec((1, M_HEADS, tm, M_QK_DIM), bt),
        pl.BlockSpec((1, M_HEADS, tm, M_QK_DIM), bt),
        pl.BlockSpec((1, M_HEADS, tm, M_V_DIM), bt),
        pl.BlockSpec((N_GATE_ROWS, tm), lambda i: (0, i)),
        pl.BlockSpec((1, A_HEADS, tm, LANES), bt),
        pl.BlockSpec((1, A_HEADS, tm, LANES), bt),
        pl.BlockSpec((1, 1, A_HEADS, A_V_ROWS, tm), lambda i: (i // tps, lax.rem(i, tps), 0, 0, 0)),
    ]
    return pl.pallas_call(
        functools.partial(_inproj_body, tps, tm),
        grid=(T // tm,),
        in_specs=in_specs,
        out_specs=out_specs,
        out_shape=out_shape,
        scratch_shapes=[pltpu.VMEM((tm + 2 * HALO, D_MODEL), BF16)],
        compiler_params=pltpu.CompilerParams(dimension_semantics=("arbitrary",),
                                             vmem_limit_bytes=VMEM_LIMIT),
        name="inproj",
    )(x2d, x2d, x2d, preg, wmqk, w1, wavt, wgt, convw, convb, rc, rs)


N_GATE_COLS = 5


def _gates_body(S, ct, f_ref, i_ref, fb_ref, ib_ref, rows_ref, cols_ref, scan_ref, stage_ref):
    h = pl.program_id(1)

    @pl.when(h == 0)
    def _():
        _gate_scans(S, f_ref, i_ref, fb_ref, ib_ref, scan_ref)

    rows_ref[...] = jnp.zeros_like(rows_ref)
    stage_ref[...] = jnp.zeros_like(stage_ref)
    for d in range(2):
        rows_ref[0, 0, d:d + 1, :] = scan_ref[N_GATE_COLS, pl.ds(d * M_HEADS + h, 1), :]
    for t in range(S // ct):
        sl = slice(t * ct, (t + 1) * ct)
        for k in range(N_GATE_COLS):
            for d in range(2):
                stage_ref[2 * k + d:2 * k + d + 1, :] = scan_ref[k, pl.ds(d * M_HEADS + h, 1), sl]
        cols_ref[0, 0, sl, :] = stage_ref[...].T


def _gate_scans(S, f_ref, i_ref, fb_ref, ib_ref, scan_ref):
    L = M_CHUNK
    f = f_ref[...] + fb_ref[...]
    ig = i_ref[...] + ib_ref[...]
    lf = jnp.minimum(f, 0.0) - jnp.log1p(jnp.exp(-jnp.abs(f)))
    lane = lax.broadcasted_iota(jnp.int32, f.shape, 1)
    pos = lane & (L - 1)
    fwd = lax.broadcasted_iota(jnp.int32, f.shape, 0) < M_HEADS

    def from_left(x, s):
        return pltpu.roll(x, s, axis=1)

    def from_right(x, s):
        return pltpu.roll(x, S - s, axis=1)

    def chunk_scans(x, op, ident):
        pre, suf = x, x
        s = 1
        while s < L:
            pre = op(pre, jnp.where(pos >= s, from_left(pre, s), ident))
            suf = op(suf, jnp.where(pos < L - s, from_right(suf, s), ident))
            s *= 2
        return pre, suf

    pre, suf = chunk_scans(lf, jnp.add, 0.0)
    a = jnp.where(fwd, pre, suf)
    a_tot = pre + suf - lf
    w = ig - a
    wpre, wsuf = chunk_scans(w, jnp.maximum, -jnp.inf)
    rowmax = a + jnp.where(fwd, wpre, wsuf)
    g_max = a_tot + jnp.maximum(wpre, wsuf)

    af, gf, ab, gb = a_tot, g_max, a_tot, g_max
    s = L
    while s < S:
        ok = lane >= s
        a_sh, g_sh = from_left(af, s), from_left(gf, s)
        gf = jnp.where(ok, jnp.maximum(g_sh + af, gf), gf)
        af = jnp.where(ok, af + a_sh, af)
        ok = lane < S - s
        a_sh, g_sh = from_right(ab, s), from_right(gb, s)
        gb = jnp.where(ok, jnp.maximum(g_sh + ab, gb), gb)
        ab = jnp.where(ok, ab + a_sh, ab)
        s *= 2
    m_f = jnp.maximum(af, gf)
    m_b = jnp.maximum(ab, gb)
    m_prev = jnp.where(fwd, jnp.where(lane >= L, from_left(m_f, L), 0.0),
                       jnp.where(lane < S - L, from_right(m_b, L), 0.0))

    inter = a + m_prev
    mj = jnp.maximum(inter, rowmax)
    m_new = jnp.maximum(a_tot + m_prev, g_max)
    scan_ref[0] = a - mj
    scan_ref[1] = jnp.exp(inter - mj)
    scan_ref[2] = jnp.exp(-mj)
    scan_ref[3] = jnp.exp(a_tot + w - m_new)
    scan_ref[4] = jnp.exp(a_tot + m_prev - m_new)
    scan_ref[N_GATE_COLS] = w


def _gates_call(gt, gb, B, S):
    ct = min(2048, S)
    return pl.pallas_call(
        functools.partial(_gates_body, S, ct),
        grid=(B, M_HEADS),
        in_specs=[pl.BlockSpec((8, S), lambda b, h: (0, b)),
                  pl.BlockSpec((8, S), lambda b, h: (1, b)),
                  pl.BlockSpec((8, 1), lambda b, h: (0, 0)),
                  pl.BlockSpec((8, 1), lambda b, h: (1, 0))],
        out_specs=[pl.BlockSpec((1, 1, 8, S), lambda b, h: (b, h, 0, 0)),
                   pl.BlockSpec((1, 1, S, LANES), lambda b, h: (b, h, 0, 0))],
        out_shape=[jax.ShapeDtypeStruct((B, M_HEADS, 8, S), F32),
                   jax.ShapeDtypeStruct((B, M_HEADS, S, LANES), F32)],
        scratch_shapes=[pltpu.VMEM((N_GATE_COLS + 1, 8, S), F32),
                        pltpu.VMEM((LANES, ct), F32)],
        compiler_params=pltpu.CompilerParams(dimension_semantics=("arbitrary", "arbitrary"),
                                             vmem_limit_bytes=VMEM_LIMIT),
        name="gates",
    )(gt, gt, gb, gb)


class _MlstmScan:
    def __init__(self, rev, nchunk, q_ref, k_ref, v_ref, rows_ref, cols_ref, h_ref, c_ref, n_ref):
        self.rev, self.d = rev, (1 if rev else 0)
        self.q_ref, self.k_ref, self.v_ref = q_ref, k_ref, v_ref
        self.rows_ref, self.cols_ref, self.h_ref = rows_ref, cols_ref, h_ref
        self.c_ref, self.n_ref = c_ref, n_ref
        self.order = list(range(nchunk - 1, -1, -1) if rev else range(nchunk))
        L = M_CHUNK
        row_i = lax.broadcasted_iota(jnp.int32, (L, L), 0)
        col_i = lax.broadcasted_iota(jnp.int32, (L, L), 1)
        self.visible = (col_i >= row_i) if rev else (col_i <= row_i)

    @staticmethod
    def rows_of(c):
        return slice(c * M_CHUNK, (c + 1) * M_CHUNK)

    def gate_col(self, c, k):
        return self.cols_ref[0, 0, self.rows_of(c), 2 * k + self.d:2 * k + self.d + 1]

    def input_products(self):
        self.qk, self.kv, self.kn = {}, {}, {}
        for c in self.order:
            q, k = self.q_ref[0, 0, self.rows_of(c), :], self.k_ref[0, 0, self.rows_of(c), :]
            self.qk[c] = lax.dot_general(q, k, _NT, preferred_element_type=F32)
            wk = self.gate_col(c, 3)
            wv = (wk * self.v_ref[0, 0, self.rows_of(c), :].astype(F32)).astype(BF16)
            self.kv[c] = lax.dot_general(k, wv, _TN, preferred_element_type=F32)
            self.kn[c] = jnp.sum(wk * k.astype(F32), axis=0, keepdims=True)

    def states(self):
        C, n = self.c_ref[...], self.n_ref[...]
        self.c_in, self.n_in = {}, {}
        for c in self.order:
            self.c_in[c], self.n_in[c] = C.astype(BF16), n
            decay = self.gate_col(c, 4)[0:1, :]
            C = decay * C + self.kv[c]
            n = decay * n + self.kn[c]
        self.c_ref[...] = C
        self.n_ref[...] = n

    def read_out(self):
        for c in self.order:
            u, w_inter, e = self.gate_col(c, 0), self.gate_col(c, 1), self.gate_col(c, 2)
            w_row = self.rows_ref[0, 0, self.d:self.d + 1, self.rows_of(c)]
            p = jnp.exp(jnp.where(self.visible, u + w_row, -jnp.inf))
            s = self.qk[c] * p
            q = self.q_ref[0, 0, self.rows_of(c), :]
            num = (w_inter * jnp.dot(q, self.c_in[c], preferred_element_type=F32)
                   + jnp.dot(s.astype(BF16), self.v_ref[0, 0, self.rows_of(c), :],
                             preferred_element_type=F32))
            den = (w_inter * jnp.sum(q.astype(F32) * self.n_in[c], axis=1, keepdims=True)
                   + jnp.sum(s, axis=1, keepdims=True))
            self.h_ref[0, self.rows_of(c), :] = num / jnp.maximum(jnp.abs(den), e)


def _mlstm_body(nchunk, qf_ref, kf_ref, vf_ref, rowsf_ref, colsf_ref, qb_ref, kb_ref, vb_ref, rowsb_ref,
                colsb_ref, hf_ref, hb_ref, cf_ref, nf_ref, cb_ref, nb_ref):
    @pl.when(pl.program_id(2) == 0)
    def _():
        for ref in (cf_ref, nf_ref, cb_ref, nb_ref):
            ref[...] = jnp.zeros_like(ref)

    fwd = _MlstmScan(False, nchunk, qf_ref, kf_ref, vf_ref, rowsf_ref, colsf_ref, hf_ref, cf_ref, nf_ref)
    bwd = _MlstmScan(True, nchunk, qb_ref, kb_ref, vb_ref, rowsb_ref, colsb_ref, hb_ref, cb_ref, nb_ref)
    fwd.input_products()
    bwd.input_products()
    fwd.states()
    bwd.states()
    fwd.read_out()
    bwd.read_out()


def _mlstm_call(mq, mk, mv, rows, cols, B, S, tr):
    ns = S // tr

    def specs(pos):
        return [pl.BlockSpec((1, 1, tr, M_QK_DIM), lambda b, h, s: (b, h, pos(s), 0)),
                pl.BlockSpec((1, 1, tr, M_QK_DIM), lambda b, h, s: (b, h, pos(s), 0)),
                pl.BlockSpec((1, 1, tr, M_V_DIM), lambda b, h, s: (b, h, pos(s), 0)),
                pl.BlockSpec((1, 1, 8, tr), lambda b, h, s: (b, h, 0, pos(s))),
                pl.BlockSpec((1, 1, tr, LANES), lambda b, h, s: (b, h, pos(s), 0))]

    fpos = lambda s: s
    bpos = lambda s: ns - 1 - s
    state = pltpu.VMEM((M_QK_DIM, M_V_DIM), F32)
    norm = pltpu.VMEM((1, M_QK_DIM), F32)
    return pl.pallas_call(
        functools.partial(_mlstm_body, tr // M_CHUNK),
        grid=(B, M_HEADS, ns),
        in_specs=specs(fpos) + specs(bpos),
        out_specs=[pl.BlockSpec((1, tr, M_V_DIM), lambda b, h, s: (b, fpos(s), h)),
                   pl.BlockSpec((1, tr, M_V_DIM), lambda b, h, s: (b, bpos(s), h))],
        out_shape=[jax.ShapeDtypeStruct((B, S, M_WIDTH), F32),
                   jax.ShapeDtypeStruct((B, S, M_WIDTH), F32)],
        scratch_shapes=[state, norm, state, norm],
        compiler_params=pltpu.CompilerParams(dimension_semantics=("arbitrary", "arbitrary", "arbitrary"),
                                             vmem_limit_bytes=VMEM_LIMIT),
        name="mlstm",
    )(mq, mk, mv, rows, cols, mq, mk, mv, rows, cols)


def _attn_body(nkc, nt, tq, tk, tv, lam_init, q_ref, k_ref, vt_ref, lq_ref, g_ref, o_ref,
               qz_ref, st0_ref, st1_ref, m_ref, acc_ref):
    def tile_rows(t):
        return pl.ds(pl.multiple_of(t * tq, tq), tq)

    def load_queries(t):
        q = q_ref[0, 0, tile_rows(t), :]
        lane = lax.broadcasted_iota(jnp.int32, q.shape, 1)
        zero = jnp.zeros_like(q)
        qz_ref[0:tq, :] = jnp.where(lane < A_HEAD_DIM, q, zero)
        qz_ref[tq:2 * tq, :] = jnp.where(lane >= A_HEAD_DIM, q, zero)

    def reset_stats():
        m_ref[...] = jnp.full(m_ref.shape, -jnp.inf, F32)
        acc_ref[...] = jnp.zeros_like(acc_ref)

    strips = [slice(c * MXU_N, (c + 1) * MXU_N) for c in range(2 * tq // MXU_N)]
    nv = tk // tv

    def score_strip(j, st_ref, sl):
        kj = k_ref[0, 0, pl.ds(pl.multiple_of(j * tk, tk), tk), :]
        st = lax.dot_general(kj, qz_ref[sl, :], _NT, preferred_element_type=F32)
        st_ref[:, sl] = st
        return jnp.max(st, axis=0, keepdims=True)

    def accumulate_strip(j, st_ref, sl, cm):
        m_old = m_ref[:, sl]
        m_new = jnp.maximum(m_old, cm)
        alpha = jnp.exp2(m_old - m_new)
        p = jnp.exp2(st_ref[:, sl] - m_new)
        pv = jnp.dot(vt_ref[0, j * nv, 0], p[0:tv], preferred_element_type=F32)
        for u in range(1, nv):
            pv = pv + jnp.dot(vt_ref[0, j * nv + u, 0], p[u * tv:(u + 1) * tv],
                              preferred_element_type=F32)
        acc_ref[:, sl] = alpha * acc_ref[:, sl] + pv
        m_ref[:, sl] = m_new

    def step(j_next, st_next_ref, j_cur, st_cur_ref, cmax_cur):
        cmax_next = []
        for sl, cm in zip(strips, cmax_cur):
            cmax_next.append(score_strip(j_next, st_next_ref, sl))
            accumulate_strip(j_cur, st_cur_ref, sl, cm)
        return tuple(cmax_next)

    def pair(i, cmax0):
        j = 2 * i
        cmax1 = step(j + 1, st1_ref, j, st0_ref, cmax0)
        return step(j + 2, st0_ref, j + 1, st1_ref, cmax1)

    def two_pairs(i, cmax0):
        return pair(2 * i + 1, pair(2 * i, cmax0))

    def all_but_last_chunk(cmax0):
        npairs = nkc // 2 - 1
        cmax0 = lax.fori_loop(0, npairs // 2, two_pairs, cmax0)
        if npairs % 2:
            cmax0 = pair(npairs - 1, cmax0)
        return step(nkc - 1, st1_ref, nkc - 2, st0_ref, cmax0)

    lq = lq_ref[...]
    lam = (jnp.exp(jnp.sum(lq[0:1] * lq[1:2], axis=1, keepdims=True))
           - jnp.exp(jnp.sum(lq[2:3] * lq[3:4], axis=1, keepdims=True)) + lam_init)

    def finalize(t):
        accn = acc_ref[0:A_V_DIM, :] / acc_ref[A_V_DIM:A_V_DIM + 1, :]
        ot = accn[:, 0:tq] - lam * accn[:, tq:2 * tq]
        otn = ot * lax.rsqrt(jnp.mean(ot * ot, axis=0, keepdims=True) + EPS)
        o_ref[tile_rows(t), :] = (otn.T * g_ref[...]) * (1.0 - lam_init)

    def tile(t, cmax0):
        cmax1 = all_but_last_chunk(cmax0)
        load_queries(t + 1)
        cmax_next = []
        for sl, cm in zip(strips, cmax1):
            cmax_next.append(score_strip(0, st0_ref, sl))
            accumulate_strip(nkc - 1, st1_ref, sl, cm)
        finalize(t)
        reset_stats()
        return tuple(cmax_next)

    load_queries(0)
    reset_stats()
    cmax0 = tuple(score_strip(0, st0_ref, sl) for sl in strips)
    cmax0 = lax.fori_loop(0, nt - 1, tile, cmax0)
    cmax1 = all_but_last_chunk(cmax0)
    for sl, cm in zip(strips, cmax1):
        accumulate_strip(nkc - 1, st1_ref, sl, cm)
    finalize(nt - 1)


def _attn_call(aq, ak, avt, lq, ag, B, S, tq, nt, tk, tv, lam_init):
    nkc = S // tk
    nq = S // (nt * tq)
    assert nkc >= 2 and nkc % 2 == 0, "key chunks are processed in pairs"
    assert tk % tv == 0 and S % (nt * tq) == 0
    return pl.pallas_call(
        functools.partial(_attn_body, nkc, nt, tq, tk, tv, lam_init),
        grid=(B, A_HEADS, nq),
        in_specs=[pl.BlockSpec((1, 1, nt * tq, LANES), lambda b, h, i: (b, h, i, 0)),
                  pl.BlockSpec((1, 1, S, LANES), lambda b, h, i: (b, h, 0, 0)),
                  pl.BlockSpec((1, S // tv, 1, A_V_ROWS, tv), lambda b, h, i: (b, 0, h, 0, 0)),
                  pl.BlockSpec(lq.shape, lambda b, h, i: (0, 0)),
                  pl.BlockSpec(ag.shape, lambda b, h, i: (0, 0))],
        out_specs=pl.BlockSpec((nt * tq, A_V_DIM), lambda b, h, i: (b * nq + i, h)),
        out_shape=jax.ShapeDtypeStruct((B * S, A_WIDTH), F32),
        scratch_shapes=[pltpu.VMEM((2 * tq, LANES), BF16),
                        pltpu.VMEM((tk, 2 * tq), F32),
                        pltpu.VMEM((tk, 2 * tq), F32),
                        pltpu.VMEM((1, 2 * tq), F32),
                        pltpu.VMEM((A_V_ROWS, 2 * tq), F32)],
        compiler_params=pltpu.CompilerParams(dimension_semantics=("arbitrary", "arbitrary", "arbitrary"),
                                             vmem_limit_bytes=VMEM_LIMIT),
        name="attn",
    )(aq, ak, avt, lq, ag)


def _epilogue_body(x_ref, hf_ref, hb_ref, o_ref, preg_ref, w4_ref, mng_ref, bm_ref, wbm_ref, wba_ref,
                   wout_ref, postg_ref, out_ref):
    x = x_ref[...]
    xn = (_rms(x) * preg_ref[...]).astype(BF16)

    def proj(lo, hi):
        return jnp.dot(xn, w4_ref[:, lo:hi], preferred_element_type=F32)

    h = hf_ref[...] + hb_ref[...]
    mng = mng_ref[...]
    hn = jnp.concatenate(
        [_rms(h[:, k * M_V_DIM:(k + 1) * M_V_DIM]) * mng[:, k * M_V_DIM:(k + 1) * M_V_DIM]
         for k in range(M_HEADS)], axis=1)
    h_a = jax.nn.sigmoid(proj(0, M_WIDTH)) * hn * _silu(proj(M_WIDTH, 2 * M_WIDTH))
    ya = jnp.dot(h_a.astype(BF16), wbm_ref[...], preferred_element_type=F32)
    h_b = o_ref[...] * _silu(proj(2 * M_WIDTH, 2 * M_WIDTH + A_WIDTH))
    yb = jnp.dot(h_b.astype(BF16), wba_ref[...], preferred_element_type=F32)
    g0 = 2 * M_WIDTH + A_WIDTH
    bm = bm_ref[...]
    g_a = jax.nn.sigmoid(proj(g0, g0 + D_MODEL) + bm[:, 0:D_MODEL])
    g_b = jax.nn.sigmoid(proj(g0 + D_MODEL, g0 + 2 * D_MODEL) + bm[:, D_MODEL:2 * D_MODEL])
    y = g_a * ya + g_b * yb
    out = jnp.dot(y.astype(BF16), wout_ref[...], preferred_element_type=F32)
    out_ref[...] = x + _rms(out) * postg_ref[...]


def _epilogue_call(x2d, hf, hb, o, preg, w4, mng, bm, wbm, wba, wout, postg, te):
    T = x2d.shape[0]
    row = lambda i: (i, 0)
    const = lambda i: (0, 0)
    single = pl.Buffered(1)
    return pl.pallas_call(
        _epilogue_body,
        grid=(T // te,),
        in_specs=[pl.BlockSpec((te, D_MODEL), row),
                  pl.BlockSpec((te, M_WIDTH), row),
                  pl.BlockSpec((te, M_WIDTH), row),
                  pl.BlockSpec((te, A_WIDTH), row),
                  pl.BlockSpec(preg.shape, const),
                  pl.BlockSpec(w4.shape, const, pipeline_mode=single),
                  pl.BlockSpec(mng.shape, const),
                  pl.BlockSpec(bm.shape, const),
                  pl.BlockSpec(wbm.shape, const, pipeline_mode=single),
                  pl.BlockSpec(wba.shape, const, pipeline_mode=single),
                  pl.BlockSpec(wout.shape, const, pipeline_mode=single),
                  pl.BlockSpec(postg.shape, const)],
        out_specs=pl.BlockSpec((te, D_MODEL), row),
        out_shape=jax.ShapeDtypeStruct((T, D_MODEL), F32),
        compiler_params=pltpu.CompilerParams(dimension_semantics=("arbitrary",),
                                             vmem_limit_bytes=VMEM_LIMIT),
        name="epilogue",
    )(x2d, hf, hb, o, preg, w4, mng, bm, wbm, wba, wout, postg)


def _rope_tables(S):
    f32 = np.float32
    inv = ROPE_THETA ** (-np.arange(0, ROPE_DIMS, 2, dtype=np.float64) / ROPE_DIMS)
    ang = np.arange(S, dtype=np.float64)[:, None] * inv[None, :]
    cos, sin = np.cos(ang).astype(f32), np.sin(ang).astype(f32)
    one = np.ones((S, A_HEAD_DIM - ROPE_DIMS), f32)
    zero = np.zeros((S, A_HEAD_DIM - ROPE_DIMS), f32)
    rc = np.concatenate([cos, cos, one], axis=1)
    rs = np.concatenate([-sin, sin, zero], axis=1)
    rep = LANES // A_HEAD_DIM
    return jnp.asarray(np.tile(rc, (1, rep))), jnp.asarray(np.tile(rs, (1, rep)))


def _gate_layout(m):
    return jnp.concatenate([m[..., 1, :], m[..., 3, :], m[..., 0, :], m[..., 2, :]], axis=-1)


def _layer(x, layer_idx, pre_g, w_in, conv_w, conv_b, gate_b, m_norm_g, lambda_qk, a_norm_g,
           w_branch_m, w_branch_a, b_merge, w_out, post_g):
    B, S, _ = x.shape
    T = B * S
    tm = min(512, S)
    tq = min(2048, S // 2)
    nt = min(2, S // tq)
    tk = min(512, S // 2)
    tr = min(512, S)
    te = min(256, T)
    lam_init = 0.8 - 0.6 * math.exp(-0.3 * layer_idx)

    cuts = [0]
    for sz in SPLIT_SIZES:
        cuts.append(cuts[-1] + sz)
    col = lambda k: w_in[:, cuts[k]:cuts[k + 1]]
    wmqk = col(0).astype(BF16)
    w1 = jnp.concatenate([col(1), col(5), col(6)], axis=1).astype(BF16)
    wavt = col(7).T.astype(BF16)
    wgt = _gate_layout(col(2).reshape(D_MODEL, 4, M_HEADS)).T.astype(BF16)
    gb = _gate_layout(gate_b).reshape(N_GATE_ROWS, 1).astype(F32)
    w4 = jnp.concatenate([col(3), col(4), col(8), col(9)], axis=1).astype(BF16)
    rc, rs = _rope_tables(S)

    x2d = x.reshape(T, D_MODEL)
    preg = pre_g.reshape(1, D_MODEL)
    mq, mk, mv, gt, aq, ak, avt = _inproj_call(
        x2d, preg, wmqk, w1, wavt, wgt, conv_w, conv_b.reshape(1, 2 * M_QK), rc, rs, B, S, tm)
    rows, cols = _gates_call(gt, gb, B, S)
    hf, hb = _mlstm_call(mq, mk, mv, rows, cols, B, S, tr)
    o = _attn_call(aq, ak, avt, lambda_qk.astype(F32), a_norm_g.reshape(1, A_V_DIM), B, S, tq, nt, tk, tm,
                   lam_init)
    out = _epilogue_call(
        x2d, hf.reshape(T, M_WIDTH), hb.reshape(T, M_WIDTH), o, preg, w4,
        m_norm_g.reshape(1, M_WIDTH), b_merge.reshape(1, 2 * D_MODEL),
        w_branch_m.astype(BF16), w_branch_a.astype(BF16), w_out.astype(BF16),
        post_g.reshape(1, D_MODEL), te)
    return out.reshape(B, S, D_MODEL)


def kernel(x, pre_norm_g, w_in, conv_w, conv_b, gate_b, m_norm_g, lambda_qk, a_norm_g, w_branch_m,
           w_branch_a, b_merge, w_out, post_norm_g):
    for l in range(pre_norm_g.shape[0]):
        x = _layer(x, l, pre_norm_g[l], w_in[l], conv_w[l], conv_b[l], gate_b[l], m_norm_g[l],
                   lambda_qk[l], a_norm_g[l], w_branch_m[l], w_branch_a[l], b_merge[l], w_out[l],
                   post_norm_g[l])
    return x
```

```python
import functools
import math

import jax
import jax.numpy as jnp
import numpy as np
from jax import lax
from jax.experimental import pallas as pl
from jax.experimental.pallas import tpu as pltpu

D_MODEL = 1024
M_HEADS = 4
M_QK_DIM = 128
M_V_DIM = 256
M_QK = M_HEADS * M_QK_DIM
M_WIDTH = M_HEADS * M_V_DIM
M_CHUNK = 64
CONV_WIDTH = 4
A_HEADS = 8
A_HEAD_DIM = 64
A_V_DIM = 2 * A_HEAD_DIM
A_QK = A_HEADS * 2 * A_HEAD_DIM
A_WIDTH = A_HEADS * A_V_DIM
ROPE_THETA = 500000.0
ROPE_DIMS = A_HEAD_DIM // 4
EPS = 1e-6
SPLIT_SIZES = (2 * M_QK, M_WIDTH, 4 * M_HEADS, M_WIDTH, M_WIDTH, A_QK, A_QK, A_WIDTH, A_WIDTH, 2 * D_MODEL)

LANES = 128
MXU_N = 256
F32_ROWS = 8
A_V_ROWS = A_V_DIM + F32_ROWS
HALO = 16
N_GATE_ROWS = 2 * 2 * M_HEADS
EPILOGUE_SUB = 256
VMEM_LIMIT = 60 * 1024 * 1024

F32 = jnp.float32
BF16 = jnp.bfloat16
_NT = (((1,), (1,)), ((), ()))
_TN = (((0,), (0,)), ((), ()))


def _rms(xf):
    return xf * lax.rsqrt(jnp.mean(xf * xf, axis=-1, keepdims=True) + EPS)


def _silu(y):
    return y * jax.nn.sigmoid(y)


def _inproj_body(tps, tm, xprev_ref, x_ref, xnext_ref, preg_ref, wmqk_ref, w1_ref, wavt_ref, wgt_ref,
                 convw_ref, convb_ref, rc_ref, rs_ref,
                 mq_ref, mk_ref, mv_ref, gt_ref, aq_ref, ak_ref, avt_ref, xext_ref):
    t = lax.rem(pl.program_id(0), tps)
    g = preg_ref[...]
    xn = (_rms(x_ref[...]) * g).astype(BF16)
    xp = jnp.where(t != 0, _rms(xprev_ref[...]) * g, 0.0).astype(BF16)
    xq = jnp.where(t != tps - 1, _rms(xnext_ref[...]) * g, 0.0).astype(BF16)
    xext_ref[0:HALO, :] = xp
    xext_ref[HALO:HALO + tm, :] = xn
    xext_ref[HALO + tm:, :] = xq
    n_ext = tm + 2 * HALO
    ext = jnp.dot(xext_ref[...], wmqk_ref[...], preferred_element_type=F32)
    mv = jnp.dot(xn, w1_ref[:, 0:M_WIDTH], preferred_element_type=F32)
    aq = jnp.dot(xn, w1_ref[:, M_WIDTH:M_WIDTH + A_QK], preferred_element_type=F32)
    ak = jnp.dot(xn, w1_ref[:, M_WIDTH + A_QK:M_WIDTH + 2 * A_QK], preferred_element_type=F32)
    avt = lax.dot_general(wavt_ref[...], xn, _NT, preferred_element_type=F32)
    gt_ref[...] = lax.dot_general(wgt_ref[...], xn, _NT, preferred_element_type=F32)

    left = CONV_WIDTH // 2
    y = None
    for j in range(CONV_WIDTH):
        sh = (left - j) % n_ext
        r = ext if sh == 0 else pltpu.roll(ext, sh, axis=0)
        term = convw_ref[j:j + 1, :] * r[HALO:HALO + tm, :]
        y = term if y is None else y + term
    y = _silu(y + convb_ref[...])
    qscale = M_QK_DIM ** -0.5
    for h in range(M_HEADS):
        mq_ref[0, h] = (y[:, h * M_QK_DIM:(h + 1) * M_QK_DIM] * qscale).astype(BF16)
        mk_ref[0, h] = y[:, M_QK + h * M_QK_DIM:M_QK + (h + 1) * M_QK_DIM].astype(BF16)
    for h in range(M_HEADS):
        mv_ref[0, h] = mv[:, h * M_V_DIM:(h + 1) * M_V_DIM].astype(BF16)
    rc, rs = rc_ref[...], rs_ref[...]
    half = ROPE_DIMS // 2
    first_half = (lax.broadcasted_iota(jnp.int32, rc.shape, 1) & (A_HEAD_DIM - 1)) < half

    def rope(a):
        partner = jnp.where(first_half, pltpu.roll(a, LANES - half, axis=1), pltpu.roll(a, half, axis=1))
        return a * rc + partner * rs

    ascale = A_HEAD_DIM ** -0.5 * math.log2(math.e)
    for h in range(A_HEADS):
        aq_ref[0, h] = (rope(aq[:, h * LANES:(h + 1) * LANES]) * ascale).astype(BF16)
        ak_ref[0, h] = rope(ak[:, h * LANES:(h + 1) * LANES]).astype(BF16)
    ones_tile = (lax.broadcasted_iota(jnp.int32, (F32_ROWS, tm), 0) == 0).astype(F32)
    for h in range(A_HEADS):
        avt_ref[0, 0, h, 0:A_V_DIM, :] = avt[h * A_V_DIM:(h + 1) * A_V_DIM, :].astype(BF16).astype(F32)
        avt_ref[0, 0, h, A_V_DIM:A_V_ROWS, :] = ones_tile


def _inproj_call(x2d, preg, wmqk, w1, wavt, wgt, convw, convb, rc, rs, B, S, tm):
    T = B * S
    tps = S // tm
    nh = tm // HALO
    const = lambda i: (0, 0)
    in_specs = [
        pl.BlockSpec((HALO, D_MODEL), lambda i: (jnp.maximum(i * nh - 1, 0), 0)),
        pl.BlockSpec((tm, D_MODEL), lambda i: (i, 0)),
        pl.BlockSpec((HALO, D_MODEL), lambda i: (jnp.minimum((i + 1) * nh, T // HALO - 1), 0)),
        pl.BlockSpec((1, D_MODEL), const),
        pl.BlockSpec(wmqk.shape, const),
        pl.BlockSpec(w1.shape, const),
        pl.BlockSpec(wavt.shape, const),
        pl.BlockSpec(wgt.shape, const),
        pl.BlockSpec(convw.shape, const),
        pl.BlockSpec(convb.shape, const),
        pl.BlockSpec((tm, LANES), lambda i: (lax.rem(i, tps), 0)),
        pl.BlockSpec((tm, LANES), lambda i: (lax.rem(i, tps), 0)),
    ]
    bt = lambda i: (i // tps, 0, lax.rem(i, tps), 0)
    out_shape = [
        jax.ShapeDtypeStruct((B, M_HEADS, S, M_QK_DIM), BF16),
        jax.ShapeDtypeStruct((B, M_HEADS, S, M_QK_DIM), BF16),
        jax.ShapeDtypeStruct((B, M_HEADS, S, M_V_DIM), BF16),
        jax.ShapeDtypeStruct((N_GATE_ROWS, T), F32),
        jax.ShapeDtypeStruct((B, A_HEADS, S, LANES), BF16),
        jax.ShapeDtypeStruct((B, A_HEADS, S, LANES), BF16),
        jax.ShapeDtypeStruct((B, tps, A_HEADS, A_V_ROWS, tm), F32),
    ]
    out_specs = [
        pl.BlockSpec((1, M_HEADS, tm, M_QK_DIM), bt),
        pl.BlockSpec((1, M_HEADS, tm, M_QK_DIM), bt),
        pl.BlockSpec((1, M_HEADS, tm, M_V_DIM), bt),
        pl.BlockSpec((N_GATE_ROWS, tm), lambda i: (0, i)),
        pl.BlockSpec((1, A_HEADS, tm, LANES), bt),
        pl.BlockSpec((1, A_HEADS, tm, LANES), bt),
        pl.BlockSpec((1, 1, A_HEADS, A_V_ROWS, tm), lambda i: (i // tps, lax.rem(i, tps), 0, 0, 0)),
    ]
    return pl.pallas_call(
        functools.partial(_inproj_body, tps, tm),
        grid=(T // tm,),
        in_specs=in_specs,
        out_specs=out_specs,
        out_shape=out_shape,
        scratch_shapes=[pltpu.VMEM((tm + 2 * HALO, D_MODEL), BF16)],
        compiler_params=pltpu.CompilerParams(dimension_semantics=("arbitrary",),
                                             vmem_limit_bytes=VMEM_LIMIT),
        name="inproj",
    )(x2d, x2d, x2d, preg, wmqk, w1, wavt, wgt, convw, convb, rc, rs)


N_GATE_COLS = 5


def _gates_body(S, ct, f_ref, i_ref, fb_ref, ib_ref, rows_ref, cols_ref, scan_ref, stage_ref):
    h = pl.program_id(1)

    @pl.when(h == 0)
    def _():
        _gate_scans(S, f_ref, i_ref, fb_ref, ib_ref, scan_ref)

    rows_ref[...] = jnp.zeros_like(rows_ref)
    stage_ref[...] = jnp.zeros_like(stage_ref)
    for d in range(2):
        rows_ref[0, 0, d:d + 1, :] = scan_ref[N_GATE_COLS, pl.ds(d * M_HEADS + h, 1), :]
    for t in range(S // ct):
        sl = slice(t * ct, (t + 1) * ct)
        for k in range(N_GATE_COLS):
            for d in range(2):
                stage_ref[2 * k + d:2 * k + d + 1, :] = scan_ref[k, pl.ds(d * M_HEADS + h, 1), sl]
        cols_ref[0, 0, sl, :] = stage_ref[...].T


def _gate_scans(S, f_ref, i_ref, fb_ref, ib_ref, scan_ref):
    L = M_CHUNK
    f = f_ref[...] + fb_ref[...]
    ig = i_ref[...] + ib_ref[...]
    lf = jnp.minimum(f, 0.0) - jnp.log1p(jnp.exp(-jnp.abs(f)))
    lane = lax.broadcasted_iota(jnp.int32, f.shape, 1)
    pos = lane & (L - 1)
    fwd = lax.broadcasted_iota(jnp.int32, f.shape, 0) < M_HEADS

    def from_left(x, s):
        return pltpu.roll(x, s, axis=1)

    def from_right(x, s):
        return pltpu.roll(x, S - s, axis=1)

    def chunk_scans(x, op, ident):
        pre, suf = x, x
        s = 1
        while s < L:
            pre = op(pre, jnp.where(pos >= s, from_left(pre, s), ident))
            suf = op(suf, jnp.where(pos < L - s, from_right(suf, s), ident))
            s *= 2
        return pre, suf

    pre, suf = chunk_scans(lf, jnp.add, 0.0)
    a = jnp.where(fwd, pre, suf)
    a_tot = pre + suf - lf
    w = ig - a
    wpre, wsuf = chunk_scans(w, jnp.maximum, -jnp.inf)
    rowmax = a + jnp.where(fwd, wpre, wsuf)
    g_max = a_tot + jnp.maximum(wpre, wsuf)

    af, gf, ab, gb = a_tot, g_max, a_tot, g_max
    s = L
    while s < S:
        ok = lane >= s
        a_sh, g_sh = from_left(af, s), from_left(gf, s)
        gf = jnp.where(ok, jnp.maximum(g_sh + af, gf), gf)
        af = jnp.where(ok, af + a_sh, af)
        ok = lane < S - s
        a_sh, g_sh = from_right(ab, s), from_right(gb, s)
        gb = jnp.where(ok, jnp.maximum(g_sh + ab, gb), gb)
        ab = jnp.where(ok, ab + a_sh, ab)
        s *= 2
    m_f = jnp.maximum(af, gf)
    m_b = jnp.maximum(ab, gb)
    m_prev = jnp.where(fwd, jnp.where(lane >= L, from_left(m_f, L), 0.0),
                       jnp.where(lane < S - L, from_right(m_b, L), 0.0))

    inter = a + m_prev
    mj = jnp.maximum(inter, rowmax)
    m_new = jnp.maximum(a_tot + m_prev, g_max)
    scan_ref[0] = a - mj
    scan_ref[1] = jnp.exp(inter - mj)
    scan_ref[2] = jnp.exp(-mj)
    scan_ref[3] = jnp.exp(a_tot + w - m_new)
    scan_ref[4] = jnp.exp(a_tot + m_prev - m_new)
    scan_ref[N_GATE_COLS] = w


def _gates_call(gt, gb, B, S):
    ct = min(2048, S)
    return pl.pallas_call(
        functools.partial(_gates_body, S, ct),
        grid=(B, M_HEADS),
        in_specs=[pl.BlockSpec((8, S), lambda b, h: (0, b)),
                  pl.BlockSpec((8, S), lambda b, h: (1, b)),
                  pl.BlockSpec((8, 1), lambda b, h: (0, 0)),
                  pl.BlockSpec((8, 1), lambda b, h: (1, 0))],
        out_specs=[pl.BlockSpec((1, 1, 8, S), lambda b, h: (b, h, 0, 0)),
                   pl.BlockSpec((1, 1, S, LANES), lambda b, h: (b, h, 0, 0))],
        out_shape=[jax.ShapeDtypeStruct((B, M_HEADS, 8, S), F32),
                   jax.ShapeDtypeStruct((B, M_HEADS, S, LANES), F32)],
        scratch_shapes=[pltpu.VMEM((N_GATE_COLS + 1, 8, S), F32),
                        pltpu.VMEM((LANES, ct), F32)],
        compiler_params=pltpu.CompilerParams(dimension_semantics=("arbitrary", "arbitrary"),
                                             vmem_limit_bytes=VMEM_LIMIT),
        name="gates",
    )(gt, gt, gb, gb)


class _MlstmScan:
    def __init__(self, rev, nchunk, q_ref, k_ref, v_ref, rows_ref, cols_ref, h_ref, c_ref):
        self.rev, self.d = rev, (1 if rev else 0)
        self.q_ref, self.k_ref, self.v_ref = q_ref, k_ref, v_ref
        self.rows_ref, self.cols_ref, self.h_ref, self.c_ref = rows_ref, cols_ref, h_ref, c_ref
        self.order = list(range(nchunk - 1, -1, -1) if rev else range(nchunk))
        L = M_CHUNK
        row_i = lax.broadcasted_iota(jnp.int32, (L, L), 0)
        col_i = lax.broadcasted_iota(jnp.int32, (L, L), 1)
        self.visible = (col_i >= row_i) if rev else (col_i <= row_i)
        self.ones_col = (lax.broadcasted_iota(jnp.int32, (L, LANES), 1) == 0).astype(BF16)

    @staticmethod
    def rows_of(c):
        return slice(c * M_CHUNK, (c + 1) * M_CHUNK)

    def gate_col(self, c, k):
        return self.cols_ref[0, 0, self.rows_of(c), 2 * k + self.d:2 * k + self.d + 1]

    def v_ext(self, c):
        return jnp.concatenate([self.v_ref[0, 0, self.rows_of(c), :], self.ones_col], axis=1)

    def input_products(self):
        self.qk, self.kv = {}, {}
        for c in self.order:
            q, k = self.q_ref[0, 0, self.rows_of(c), :], self.k_ref[0, 0, self.rows_of(c), :]
            self.qk[c] = lax.dot_general(q, k, _NT, preferred_element_type=F32)
            wv = (self.gate_col(c, 3) * self.v_ext(c).astype(F32)).astype(BF16)
            self.kv[c] = lax.dot_general(k, wv, _TN, preferred_element_type=F32)

    def states(self):
        C = self.c_ref[...]
        self.c_in = {}
        for c in self.order:
            self.c_in[c] = C.astype(BF16)
            C = self.gate_col(c, 4)[0:1, :] * C + self.kv[c]
        self.c_ref[...] = C

    def read_out(self):
        for c in self.order:
            u, w_inter, e = self.gate_col(c, 0), self.gate_col(c, 1), self.gate_col(c, 2)
            w_row = self.rows_ref[0, 0, self.d:self.d + 1, self.rows_of(c)]
            p = jnp.exp(jnp.where(self.visible, u + w_row, -jnp.inf))
            s = self.qk[c] * p
            q = self.q_ref[0, 0, self.rows_of(c), :]
            num = (w_inter * jnp.dot(q, self.c_in[c], preferred_element_type=F32)
                   + jnp.dot(s.astype(BF16), self.v_ext(c), preferred_element_type=F32))
            den = num[:, M_V_DIM:M_V_DIM + 1]
            self.h_ref[0, self.rows_of(c), :] = num[:, 0:M_V_DIM] / jnp.maximum(jnp.abs(den), e)


def _mlstm_body(nchunk, qf_ref, kf_ref, vf_ref, rowsf_ref, colsf_ref, qb_ref, kb_ref, vb_ref, rowsb_ref,
                colsb_ref, hf_ref, hb_ref, cf_ref, cb_ref):
    @pl.when(pl.program_id(2) == 0)
    def _():
        cf_ref[...] = jnp.zeros_like(cf_ref)
        cb_ref[...] = jnp.zeros_like(cb_ref)

    fwd = _MlstmScan(False, nchunk, qf_ref, kf_ref, vf_ref, rowsf_ref, colsf_ref, hf_ref, cf_ref)
    bwd = _MlstmScan(True, nchunk, qb_ref, kb_ref, vb_ref, rowsb_ref, colsb_ref, hb_ref, cb_ref)
    fwd.input_products()
    bwd.input_products()
    fwd.states()
    bwd.states()
    fwd.read_out()
    bwd.read_out()


def _mlstm_call(mq, mk, mv, rows, cols, B, S, tr):
    ns = S // tr

    def specs(pos):
        return [pl.BlockSpec((1, 1, tr, M_QK_DIM), lambda b, h, s: (b, h, pos(s), 0)),
                pl.BlockSpec((1, 1, tr, M_QK_DIM), lambda b, h, s: (b, h, pos(s), 0)),
                pl.BlockSpec((1, 1, tr, M_V_DIM), lambda b, h, s: (b, h, pos(s), 0)),
                pl.BlockSpec((1, 1, 8, tr), lambda b, h, s: (b, h, 0, pos(s))),
                pl.BlockSpec((1, 1, tr, LANES), lambda b, h, s: (b, h, pos(s), 0))]

    fpos = lambda s: s
    bpos = lambda s: ns - 1 - s
    state = pltpu.VMEM((M_QK_DIM, M_V_DIM + LANES), F32)
    return pl.pallas_call(
        functools.partial(_mlstm_body, tr // M_CHUNK),
        grid=(B, M_HEADS, ns),
        in_specs=specs(fpos) + specs(bpos),
        out_specs=[pl.BlockSpec((1, tr, M_V_DIM), lambda b, h, s: (b, fpos(s), h)),
                   pl.BlockSpec((1, tr, M_V_DIM), lambda b, h, s: (b, bpos(s), h))],
        out_shape=[jax.ShapeDtypeStruct((B, S, M_WIDTH), F32),
                   jax.ShapeDtypeStruct((B, S, M_WIDTH), F32)],
        scratch_shapes=[state, state],
        compiler_params=pltpu.CompilerParams(dimension_semantics=("arbitrary", "arbitrary", "arbitrary"),
                                             vmem_limit_bytes=VMEM_LIMIT),
        name="mlstm",
    )(mq, mk, mv, rows, cols, mq, mk, mv, rows, cols)


def _attn_body(nkc, nt, tq, tk, tv, lam_init, q_ref, k_ref, vt_ref, lq_ref, g_ref, o_ref,
               qz_ref, st0_ref, st1_ref, m_ref, acc_ref):
    def tile_rows(t):
        return pl.ds(pl.multiple_of(t * tq, tq), tq)

    def load_queries(t):
        q = q_ref[0, 0, tile_rows(t), :]
        lane = lax.broadcasted_iota(jnp.int32, q.shape, 1)
        zero = jnp.zeros_like(q)
        qz_ref[0:tq, :] = jnp.where(lane < A_HEAD_DIM, q, zero)
        qz_ref[tq:2 * tq, :] = jnp.where(lane >= A_HEAD_DIM, q, zero)

    def reset_stats():
        m_ref[...] = jnp.full(m_ref.shape, -jnp.inf, F32)
        acc_ref[...] = jnp.zeros_like(acc_ref)

    strips = [slice(c * MXU_N, (c + 1) * MXU_N) for c in range(2 * tq // MXU_N)]
    nv = tk // tv

    def score_strip(j, st_ref, sl):
        kj = k_ref[0, 0, pl.ds(pl.multiple_of(j * tk, tk), tk), :]
        st = lax.dot_general(kj, qz_ref[sl, :], _NT, preferred_element_type=F32)
        st_ref[:, sl] = st
        return jnp.max(st, axis=0, keepdims=True)

    def accumulate_strip(j, st_ref, sl, cm):
        m_old = m_ref[:, sl]
        m_new = jnp.maximum(m_old, cm)
        alpha = jnp.exp2(m_old - m_new)
        p = jnp.exp2(st_ref[:, sl] - m_new)
        pv = jnp.dot(vt_ref[0, j * nv, 0], p[0:tv], preferred_element_type=F32)
        for u in range(1, nv):
            pv = pv + jnp.dot(vt_ref[0, j * nv + u, 0], p[u * tv:(u + 1) * tv],
                              preferred_element_type=F32)
        acc_ref[:, sl] = alpha * acc_ref[:, sl] + pv
        m_ref[:, sl] = m_new

    def step(j_next, st_next_ref, j_cur, st_cur_ref, cmax_cur):
        cmax_next = []
        for sl, cm in zip(strips, cmax_cur):
            cmax_next.append(score_strip(j_next, st_next_ref, sl))
            accumulate_strip(j_cur, st_cur_ref, sl, cm)
        return tuple(cmax_next)

    def pair(i, cmax0):
        j = 2 * i
        cmax1 = step(j + 1, st1_ref, j, st0_ref, cmax0)
        return step(j + 2, st0_ref, j + 1, st1_ref, cmax1)

    def two_pairs(i, cmax0):
        return pair(2 * i + 1, pair(2 * i, cmax0))

    def all_but_last_chunk(cmax0):
        npairs = nkc // 2 - 1
        cmax0 = lax.fori_loop(0, npairs // 2, two_pairs, cmax0)
        if npairs % 2:
            cmax0 = pair(npairs - 1, cmax0)
        return step(nkc - 1, st1_ref, nkc - 2, st0_ref, cmax0)

    lq = lq_ref[...]
    lam = (jnp.exp(jnp.sum(lq[0:1] * lq[1:2], axis=1, keepdims=True))
           - jnp.exp(jnp.sum(lq[2:3] * lq[3:4], axis=1, keepdims=True)) + lam_init)

    def finalize(t):
        accn = acc_ref[0:A_V_DIM, :] / acc_ref[A_V_DIM:A_V_DIM + 1, :]
        ot = accn[:, 0:tq] - lam * accn[:, tq:2 * tq]
        otn = ot * lax.rsqrt(jnp.mean(ot * ot, axis=0, keepdims=True) + EPS)
        o_ref[tile_rows(t), :] = (otn.T * g_ref[...]) * (1.0 - lam_init)

    def tile(t, cmax0):
        cmax1 = all_but_last_chunk(cmax0)
        load_queries(t + 1)
        cmax_next = []
        for sl, cm in zip(strips, cmax1):
            cmax_next.append(score_strip(0, st0_ref, sl))
            accumulate_strip(nkc - 1, st1_ref, sl, cm)
        finalize(t)
        reset_stats()
        return tuple(cmax_next)

    load_queries(0)
    reset_stats()
    cmax0 = tuple(score_strip(0, st0_ref, sl) for sl in strips)
    cmax0 = lax.fori_loop(0, nt - 1, tile, cmax0)
    cmax1 = all_but_last_chunk(cmax0)
    for sl, cm in zip(strips, cmax1):
        accumulate_strip(nkc - 1, st1_ref, sl, cm)
    finalize(nt - 1)


def _attn_call(aq, ak, avt, lq, ag, B, S, tq, nt, tk, tv, lam_init):
    nkc = S // tk
    nq = S // (nt * tq)
    assert nkc >= 2 and nkc % 2 == 0, "key chunks are processed in pairs"
    assert tk % tv == 0 and S % (nt * tq) == 0
    return pl.pallas_call(
        functools.partial(_attn_body, nkc, nt, tq, tk, tv, lam_init),
        grid=(B, A_HEADS, nq),
        in_specs=[pl.BlockSpec((1, 1, nt * tq, LANES), lambda b, h, i: (b, h, i, 0)),
                  pl.BlockSpec((1, 1, S, LANES), lambda b, h, i: (b, h, 0, 0)),
                  pl.BlockSpec((1, S // tv, 1, A_V_ROWS, tv), lambda b, h, i: (b, 0, h, 0, 0)),
                  pl.BlockSpec(lq.shape, lambda b, h, i: (0, 0)),
                  pl.BlockSpec(ag.shape, lambda b, h, i: (0, 0))],
        out_specs=pl.BlockSpec((nt * tq, A_V_DIM), lambda b, h, i: (b * nq + i, h)),
        out_shape=jax.ShapeDtypeStruct((B * S, A_WIDTH), F32),
        scratch_shapes=[pltpu.VMEM((2 * tq, LANES), BF16),
                        pltpu.VMEM((tk, 2 * tq), F32),
                        pltpu.VMEM((tk, 2 * tq), F32),
                        pltpu.VMEM((1, 2 * tq), F32),
                        pltpu.VMEM((A_V_ROWS, 2 * tq), F32)],
        compiler_params=pltpu.CompilerParams(dimension_semantics=("arbitrary", "arbitrary", "arbitrary"),
                                             vmem_limit_bytes=VMEM_LIMIT),
        name="attn",
    )(aq, ak, avt, lq, ag)


def _epilogue_body(nsub, x_ref, hf_ref, hb_ref, o_ref, preg_ref, w4_ref, mng_ref, bm_ref, wbm_ref, wba_ref,
                   wout_ref, postg_ref, out_ref):
    ts = x_ref.shape[0] // nsub
    mng = mng_ref[...]
    bm = bm_ref[...]
    g0 = 2 * M_WIDTH + A_WIDTH
    for r in range(nsub):
        rows = slice(r * ts, (r + 1) * ts)
        x = x_ref[rows, :]
        xn = (_rms(x) * preg_ref[...]).astype(BF16)

        def proj(lo, hi, xn=xn):
            return jnp.dot(xn, w4_ref[:, lo:hi], preferred_element_type=F32)

        h = hf_ref[rows, :] + hb_ref[rows, :]
        hn = jnp.concatenate(
            [_rms(h[:, k * M_V_DIM:(k + 1) * M_V_DIM]) * mng[:, k * M_V_DIM:(k + 1) * M_V_DIM]
             for k in range(M_HEADS)], axis=1)
        h_a = jax.nn.sigmoid(proj(0, M_WIDTH)) * hn * _silu(proj(M_WIDTH, 2 * M_WIDTH))
        ya = jnp.dot(h_a.astype(BF16), wbm_ref[...], preferred_element_type=F32)
        h_b = o_ref[rows, :] * _silu(proj(2 * M_WIDTH, 2 * M_WIDTH + A_WIDTH))
        yb = jnp.dot(h_b.astype(BF16), wba_ref[...], preferred_element_type=F32)
        g_a = jax.nn.sigmoid(proj(g0, g0 + D_MODEL) + bm[:, 0:D_MODEL])
        g_b = jax.nn.sigmoid(proj(g0 + D_MODEL, g0 + 2 * D_MODEL) + bm[:, D_MODEL:2 * D_MODEL])
        y = g_a * ya + g_b * yb
        out = jnp.dot(y.astype(BF16), wout_ref[...], preferred_element_type=F32)
        out_ref[rows, :] = x + _rms(out) * postg_ref[...]


def _epilogue_call(x2d, hf, hb, o, preg, w4, mng, bm, wbm, wba, wout, postg, te):
    T = x2d.shape[0]
    row = lambda i: (i, 0)
    const = lambda i: (0, 0)
    single = pl.Buffered(1)
    return pl.pallas_call(
        functools.partial(_epilogue_body, te // EPILOGUE_SUB if te >= EPILOGUE_SUB else 1),
        grid=(T // te,),
        in_specs=[pl.BlockSpec((te, D_MODEL), row),
                  pl.BlockSpec((te, M_WIDTH), row),
                  pl.BlockSpec((te, M_WIDTH), row),
                  pl.BlockSpec((te, A_WIDTH), row),
                  pl.BlockSpec(preg.shape, const),
                  pl.BlockSpec(w4.shape, const, pipeline_mode=single),
                  pl.BlockSpec(mng.shape, const),
                  pl.BlockSpec(bm.shape, const),
                  pl.BlockSpec(wbm.shape, const, pipeline_mode=single),
                  pl.BlockSpec(wba.shape, const, pipeline_mode=single),
                  pl.BlockSpec(wout.shape, const, pipeline_mode=single),
                  pl.BlockSpec(postg.shape, const)],
        out_specs=pl.BlockSpec((te, D_MODEL), row),
        out_shape=jax.ShapeDtypeStruct((T, D_MODEL), F32),
        compiler_params=pltpu.CompilerParams(dimension_semantics=("arbitrary",),
                                             vmem_limit_bytes=VMEM_LIMIT),
        name="epilogue",
    )(x2d, hf, hb, o, preg, w4, mng, bm, wbm, wba, wout, postg)


def _rope_tables(S):
    f32 = np.float32
    inv = ROPE_THETA ** (-np.arange(0, ROPE_DIMS, 2, dtype=np.float64) / ROPE_DIMS)
    ang = np.arange(S, dtype=np.float64)[:, None] * inv[None, :]
    cos, sin = np.cos(ang).astype(f32), np.sin(ang).astype(f32)
    one = np.ones((S, A_HEAD_DIM - ROPE_DIMS), f32)
    zero = np.zeros((S, A_HEAD_DIM - ROPE_DIMS), f32)
    rc = np.concatenate([cos, cos, one], axis=1)
    rs = np.concatenate([-sin, sin, zero], axis=1)
    rep = LANES // A_HEAD_DIM
    return jnp.asarray(np.tile(rc, (1, rep))), jnp.asarray(np.tile(rs, (1, rep)))


def _gate_layout(m):
    return jnp.concatenate([m[..., 1, :], m[..., 3, :], m[..., 0, :], m[..., 2, :]], axis=-1)


def _layer(x, layer_idx, pre_g, w_in, conv_w, conv_b, gate_b, m_norm_g, lambda_qk, a_norm_g,
           w_branch_m, w_branch_a, b_merge, w_out, post_g):
    B, S, _ = x.shape
    T = B * S
    tm = min(512, S)
    tq = min(2048, S // 2)
    nt = min(2, S // tq)
    tk = min(512, S // 2)
    tr = min(512, S)
    te = min(512, T)
    lam_init = 0.8 - 0.6 * math.exp(-0.3 * layer_idx)

    cuts = [0]
    for sz in SPLIT_SIZES:
        cuts.append(cuts[-1] + sz)
    col = lambda k: w_in[:, cuts[k]:cuts[k + 1]]
    wmqk = col(0).astype(BF16)
    w1 = jnp.concatenate([col(1), col(5), col(6)], axis=1).astype(BF16)
    wavt = col(7).T.astype(BF16)
    wgt = _gate_layout(col(2).reshape(D_MODEL, 4, M_HEADS)).T.astype(BF16)
    gb = _gate_layout(gate_b).reshape(N_GATE_ROWS, 1).astype(F32)
    w4 = jnp.concatenate([col(3), col(4), col(8), col(9)], axis=1).astype(BF16)
    rc, rs = _rope_tables(S)

    x2d = x.reshape(T, D_MODEL)
    preg = pre_g.reshape(1, D_MODEL)
    mq, mk, mv, gt, aq, ak, avt = _inproj_call(
        x2d, preg, wmqk, w1, wavt, wgt, conv_w, conv_b.reshape(1, 2 * M_QK), rc, rs, B, S, tm)
    rows, cols = _gates_call(gt, gb, B, S)
    hf, hb = _mlstm_call(mq, mk, mv, rows, cols, B, S, tr)
    o = _attn_call(aq, ak, avt, lambda_qk.astype(F32), a_norm_g.reshape(1, A_V_DIM), B, S, tq, nt, tk, tm,
                   lam_init)
    out = _epilogue_call(
        x2d, hf.reshape(T, M_WIDTH), hb.reshape(T, M_WIDTH), o, preg, w4,
        m_norm_g.reshape(1, M_WIDTH), b_merge.reshape(1, 2 * D_MODEL),
        w_branch_m.astype(BF16), w_branch_a.astype(BF16), w_out.astype(BF16),
        post_g.reshape(1, D_MODEL), te)
    return out.reshape(B, S, D_MODEL)


def kernel(x, pre_norm_g, w_in, conv_w, conv_b, gate_b, m_norm_g, lambda_qk, a_norm_g, w_branch_m,
           w_branch_a, b_merge, w_out, post_norm_g):
    for l in range(pre_norm_g.shape[0]):
        x = _layer(x, l, pre_norm_g[l], w_in[l], conv_w[l], conv_b[l], gate_b[l], m_norm_g[l],
                   lambda_qk[l], a_norm_g[l], w_branch_m[l], w_branch_a[l], b_merge[l], w_out[l],
                   post_norm_g[l])
    return x
```

```python
import functools
import math

import jax
import jax.numpy as jnp
import numpy as np
from jax import lax
from jax.experimental import pallas as pl
from jax.experimental.pallas import tpu as pltpu

D_MODEL = 1024
M_HEADS = 4
M_QK_DIM = 128
M_V_DIM = 256
M_QK = M_HEADS * M_QK_DIM
M_WIDTH = M_HEADS * M_V_DIM
M_CHUNK = 64
CONV_WIDTH = 4
A_HEADS = 8
A_HEAD_DIM = 64
A_V_DIM = 2 * A_HEAD_DIM
A_QK = A_HEADS * 2 * A_HEAD_DIM
A_WIDTH = A_HEADS * A_V_DIM
ROPE_THETA = 500000.0
ROPE_DIMS = A_HEAD_DIM // 4
EPS = 1e-6
SPLIT_SIZES = (2 * M_QK, M_WIDTH, 4 * M_HEADS, M_WIDTH, M_WIDTH, A_QK, A_QK, A_WIDTH, A_WIDTH, 2 * D_MODEL)

LANES = 128
MXU_N = 256
F32_ROWS = 8
A_V_ROWS = A_V_DIM + F32_ROWS
HALO = 16
N_GATE_ROWS = 2 * 2 * M_HEADS
VMEM_LIMIT = 60 * 1024 * 1024

F32 = jnp.float32
BF16 = jnp.bfloat16
_NT = (((1,), (1,)), ((), ()))
_TN = (((0,), (0,)), ((), ()))


def _rms(xf):
    return xf * lax.rsqrt(jnp.mean(xf * xf, axis=-1, keepdims=True) + EPS)


def _silu(y):
    return y * jax.nn.sigmoid(y)


def _inproj_body(tps, tm, xprev_ref, x_ref, xnext_ref, preg_ref, wmqk_ref, w1_ref, wavt_ref, wgt_ref,
                 convw_ref, convb_ref, rc_ref, rs_ref,
                 mq_ref, mk_ref, mv_ref, gt_ref, aq_ref, ak_ref, avt_ref, xext_ref):
    t = lax.rem(pl.program_id(0), tps)
    g = preg_ref[...]
    xn = (_rms(x_ref[...]) * g).astype(BF16)
    xp = jnp.where(t != 0, _rms(xprev_ref[...]) * g, 0.0).astype(BF16)
    xq = jnp.where(t != tps - 1, _rms(xnext_ref[...]) * g, 0.0).astype(BF16)
    xext_ref[0:HALO, :] = xp
    xext_ref[HALO:HALO + tm, :] = xn
    xext_ref[HALO + tm:, :] = xq
    n_ext = tm + 2 * HALO
    ext = jnp.dot(xext_ref[...], wmqk_ref[...], preferred_element_type=F32)
    mv = jnp.dot(xn, w1_ref[:, 0:M_WIDTH], preferred_element_type=F32)
    aq = jnp.dot(xn, w1_ref[:, M_WIDTH:M_WIDTH + A_QK], preferred_element_type=F32)
    ak = jnp.dot(xn, w1_ref[:, M_WIDTH + A_QK:M_WIDTH + 2 * A_QK], preferred_element_type=F32)
    avt = lax.dot_general(wavt_ref[...], xn, _NT, preferred_element_type=F32)
    gt_ref[...] = lax.dot_general(wgt_ref[...], xn, _NT, preferred_element_type=F32)

    left = CONV_WIDTH // 2
    y = None
    for j in range(CONV_WIDTH):
        lo = HALO - left + j
        term = convw_ref[j:j + 1, :] * ext[lo:lo + tm, :]
        y = term if y is None else y + term
    y = _silu(y + convb_ref[...])
    qscale = M_QK_DIM ** -0.5
    for h in range(M_HEADS):
        mq_ref[0, h] = (y[:, h * M_QK_DIM:(h + 1) * M_QK_DIM] * qscale).astype(BF16)
        mk_ref[0, h] = y[:, M_QK + h * M_QK_DIM:M_QK + (h + 1) * M_QK_DIM].astype(BF16)
    for h in range(M_HEADS):
        mv_ref[0, h] = mv[:, h * M_V_DIM:(h + 1) * M_V_DIM].astype(BF16)
    rc, rs = rc_ref[...], rs_ref[...]
    half = ROPE_DIMS // 2
    first_half = (lax.broadcasted_iota(jnp.int32, rc.shape, 1) & (A_HEAD_DIM - 1)) < half

    def rope(a):
        partner = jnp.where(first_half, pltpu.roll(a, LANES - half, axis=1), pltpu.roll(a, half, axis=1))
        return a * rc + partner * rs

    ascale = A_HEAD_DIM ** -0.5 * math.log2(math.e)
    for h in range(A_HEADS):
        aq_ref[0, h] = (rope(aq[:, h * LANES:(h + 1) * LANES]) * ascale).astype(BF16)
        ak_ref[0, h] = rope(ak[:, h * LANES:(h + 1) * LANES]).astype(BF16)
    ones_tile = (lax.broadcasted_iota(jnp.int32, (F32_ROWS, tm), 0) == 0).astype(F32)
    for h in range(A_HEADS):
        avt_ref[0, 0, h, 0:A_V_DIM, :] = avt[h * A_V_DIM:(h + 1) * A_V_DIM, :].astype(BF16).astype(F32)
        avt_ref[0, 0, h, A_V_DIM:A_V_ROWS, :] = ones_tile


def _inproj_call(x2d, preg, wmqk, w1, wavt, wgt, convw, convb, rc, rs, B, S, tm):
    T = B * S
    tps = S // tm
    nh = tm // HALO
    const = lambda i: (0, 0)
    in_specs = [
        pl.BlockSpec((HALO, D_MODEL), lambda i: (jnp.maximum(i * nh - 1, 0), 0)),
        pl.BlockSpec((tm, D_MODEL), lambda i: (i, 0)),
        pl.BlockSpec((HALO, D_MODEL), lambda i: (jnp.minimum((i + 1) * nh, T // HALO - 1), 0)),
        pl.BlockSpec((1, D_MODEL), const),
        pl.BlockSpec(wmqk.shape, const),
        pl.BlockSpec(w1.shape, const),
        pl.BlockSpec(wavt.shape, const),
        pl.BlockSpec(wgt.shape, const),
        pl.BlockSpec(convw.shape, const),
        pl.BlockSpec(convb.shape, const),
        pl.BlockSpec((tm, LANES), lambda i: (lax.rem(i, tps), 0)),
        pl.BlockSpec((tm, LANES), lambda i: (lax.rem(i, tps), 0)),
    ]
    bt = lambda i: (i // tps, 0, lax.rem(i, tps), 0)
    out_shape = [
        jax.ShapeDtypeStruct((B, M_HEADS, S, M_QK_DIM), BF16),
        jax.ShapeDtypeStruct((B, M_HEADS, S, M_QK_DIM), BF16),
        jax.ShapeDtypeStruct((B, M_HEADS, S, M_V_DIM), BF16),
        jax.ShapeDtypeStruct((N_GATE_ROWS, T), F32),
        jax.ShapeDtypeStruct((B, A_HEADS, S, LANES), BF16),
        jax.ShapeDtypeStruct((B, A_HEADS, S, LANES), BF16),
        jax.ShapeDtypeStruct((B, tps, A_HEADS, A_V_ROWS, tm), F32),
    ]
    out_specs = [
        pl.BlockSpec((1, M_HEADS, tm, M_QK_DIM), bt),
        pl.BlockSpec((1, M_HEADS, tm, M_QK_DIM), bt),
        pl.BlockSpec((1, M_HEADS, tm, M_V_DIM), bt),
        pl.BlockSpec((N_GATE_ROWS, tm), lambda i: (0, i)),
        pl.BlockSpec((1, A_HEADS, tm, LANES), bt),
        pl.BlockSpec((1, A_HEADS, tm, LANES), bt),
        pl.BlockSpec((1, 1, A_HEADS, A_V_ROWS, tm), lambda i: (i // tps, lax.rem(i, tps), 0, 0, 0)),
    ]
    return pl.pallas_call(
        functools.partial(_inproj_body, tps, tm),
        grid=(T // tm,),
        in_specs=in_specs,
        out_specs=out_specs,
        out_shape=out_shape,
        scratch_shapes=[pltpu.VMEM((tm + 2 * HALO, D_MODEL), BF16)],
        compiler_params=pltpu.CompilerParams(dimension_semantics=("arbitrary",),
                                             vmem_limit_bytes=VMEM_LIMIT),
        name="inproj",
    )(x2d, x2d, x2d, preg, wmqk, w1, wavt, wgt, convw, convb, rc, rs)


N_GATE_COLS = 5


def _gates_body(S, ct, f_ref, i_ref, fb_ref, ib_ref, rows_ref, cols_ref, scan_ref, stage_ref):
    h = pl.program_id(1)

    @pl.when(h == 0)
    def _():
        _gate_scans(S, f_ref, i_ref, fb_ref, ib_ref, scan_ref)

    rows_ref[...] = jnp.zeros_like(rows_ref)
    stage_ref[...] = jnp.zeros_like(stage_ref)
    for d in range(2):
        rows_ref[0, 0, d:d + 1, :] = scan_ref[N_GATE_COLS, pl.ds(d * M_HEADS + h, 1), :]
    for t in range(S // ct):
        sl = slice(t * ct, (t + 1) * ct)
        for k in range(N_GATE_COLS):
            for d in range(2):
                stage_ref[2 * k + d:2 * k + d + 1, :] = scan_ref[k, pl.ds(d * M_HEADS + h, 1), sl]
        cols_ref[0, 0, sl, :] = stage_ref[...].T


def _gate_scans(S, f_ref, i_ref, fb_ref, ib_ref, scan_ref):
    L = M_CHUNK
    f = f_ref[...] + fb_ref[...]
    ig = i_ref[...] + ib_ref[...]
    lf = jnp.minimum(f, 0.0) - jnp.log1p(jnp.exp(-jnp.abs(f)))
    lane = lax.broadcasted_iota(jnp.int32, f.shape, 1)
    pos = lane & (L - 1)
    fwd = lax.broadcasted_iota(jnp.int32, f.shape, 0) < M_HEADS

    def from_left(x, s):
        return pltpu.roll(x, s, axis=1)

    def from_right(x, s):
        return pltpu.roll(x, S - s, axis=1)

    def chunk_scans(x, op, ident):
        pre, suf = x, x
        s = 1
        while s < L:
            pre = op(pre, jnp.where(pos >= s, from_left(pre, s), ident))
            suf = op(suf, jnp.where(pos < L - s, from_right(suf, s), ident))
            s *= 2
        return pre, suf

    pre, suf = chunk_scans(lf, jnp.add, 0.0)
    a = jnp.where(fwd, pre, suf)
    a_tot = pre + suf - lf
    w = ig - a
    wpre, wsuf = chunk_scans(w, jnp.maximum, -jnp.inf)
    rowmax = a + jnp.where(fwd, wpre, wsuf)
    g_max = a_tot + jnp.maximum(wpre, wsuf)

    af, gf, ab, gb = a_tot, g_max, a_tot, g_max
    s = L
    while s < S:
        ok = lane >= s
        a_sh, g_sh = from_left(af, s), from_left(gf, s)
        gf = jnp.where(ok, jnp.maximum(g_sh + af, gf), gf)
        af = jnp.where(ok, af + a_sh, af)
        ok = lane < S - s
        a_sh, g_sh = from_right(ab, s), from_right(gb, s)
        gb = jnp.where(ok, jnp.maximum(g_sh + ab, gb), gb)
        ab = jnp.where(ok, ab + a_sh, ab)
        s *= 2
    m_f = jnp.maximum(af, gf)
    m_b = jnp.maximum(ab, gb)
    m_prev = jnp.where(fwd, jnp.where(lane >= L, from_left(m_f, L), 0.0),
                       jnp.where(lane < S - L, from_right(m_b, L), 0.0))

    inter = a + m_prev
    mj = jnp.maximum(inter, rowmax)
    m_new = jnp.maximum(a_tot + m_prev, g_max)
    scan_ref[0] = a - mj
    scan_ref[1] = jnp.exp(inter - mj)
    scan_ref[2] = jnp.exp(-mj)
    scan_ref[3] = jnp.exp(a_tot + w - m_new)
    scan_ref[4] = jnp.exp(a_tot + m_prev - m_new)
    scan_ref[N_GATE_COLS] = w


def _gates_call(gt, gb, B, S):
    ct = min(2048, S)
    return pl.pallas_call(
        functools.partial(_gates_body, S, ct),
        grid=(B, M_HEADS),
        in_specs=[pl.BlockSpec((8, S), lambda b, h: (0, b)),
                  pl.BlockSpec((8, S), lambda b, h: (1, b)),
                  pl.BlockSpec((8, 1), lambda b, h: (0, 0)),
                  pl.BlockSpec((8, 1), lambda b, h: (1, 0))],
        out_specs=[pl.BlockSpec((1, 1, 8, S), lambda b, h: (b, h, 0, 0)),
                   pl.BlockSpec((1, 1, S, LANES), lambda b, h: (b, h, 0, 0))],
        out_shape=[jax.ShapeDtypeStruct((B, M_HEADS, 8, S), F32),
                   jax.ShapeDtypeStruct((B, M_HEADS, S, LANES), F32)],
        scratch_shapes=[pltpu.VMEM((N_GATE_COLS + 1, 8, S), F32),
                        pltpu.VMEM((LANES, ct), F32)],
        compiler_params=pltpu.CompilerParams(dimension_semantics=("arbitrary", "arbitrary"),
                                             vmem_limit_bytes=VMEM_LIMIT),
        name="gates",
    )(gt, gt, gb, gb)


class _MlstmScan:
    def __init__(self, rev, nchunk, q_ref, k_ref, v_ref, rows_ref, cols_ref, h_ref, c_ref):
        self.rev, self.d = rev, (1 if rev else 0)
        self.q_ref, self.k_ref, self.v_ref = q_ref, k_ref, v_ref
        self.rows_ref, self.cols_ref, self.h_ref, self.c_ref = rows_ref, cols_ref, h_ref, c_ref
        self.order = list(range(nchunk - 1, -1, -1) if rev else range(nchunk))
        L = M_CHUNK
        row_i = lax.broadcasted_iota(jnp.int32, (L, L), 0)
        col_i = lax.broadcasted_iota(jnp.int32, (L, L), 1)
        self.visible = (col_i >= row_i) if rev else (col_i <= row_i)
        self.ones_col = (lax.broadcasted_iota(jnp.int32, (L, LANES), 1) == 0).astype(BF16)

    @staticmethod
    def rows_of(c):
        return slice(c * M_CHUNK, (c + 1) * M_CHUNK)

    def gate_col(self, c, k):
        return self.cols_ref[0, 0, self.rows_of(c), 2 * k + self.d:2 * k + self.d + 1]

    def v_ext(self, c):
        return jnp.concatenate([self.v_ref[0, 0, self.rows_of(c), :], self.ones_col], axis=1)

    def input_products(self):
        self.qk, self.kv = {}, {}
        for c in self.order:
            q, k = self.q_ref[0, 0, self.rows_of(c), :], self.k_ref[0, 0, self.rows_of(c), :]
            self.qk[c] = lax.dot_general(q, k, _NT, preferred_element_type=F32)
            wv = (self.gate_col(c, 3) * self.v_ext(c).astype(F32)).astype(BF16)
            self.kv[c] = lax.dot_general(k, wv, _TN, preferred_element_type=F32)

    def states(self):
        C = self.c_ref[...]
        self.c_in = {}
        for c in self.order:
            self.c_in[c] = C.astype(BF16)
            C = self.gate_col(c, 4)[0:1, :] * C + self.kv[c]
        self.c_ref[...] = C

    def read_out(self):
        for c in self.order:
            u, w_inter, e = self.gate_col(c, 0), self.gate_col(c, 1), self.gate_col(c, 2)
            w_row = self.rows_ref[0, 0, self.d:self.d + 1, self.rows_of(c)]
            p = jnp.exp(jnp.where(self.visible, u + w_row, -jnp.inf))
            s = self.qk[c] * p
            q = self.q_ref[0, 0, self.rows_of(c), :]
            num = (w_inter * jnp.dot(q, self.c_in[c], preferred_element_type=F32)
                   + jnp.dot(s.astype(BF16), self.v_ext(c), preferred_element_type=F32))
            den = num[:, M_V_DIM:M_V_DIM + 1]
            self.h_ref[0, self.rows_of(c), :] = num[:, 0:M_V_DIM] / jnp.maximum(jnp.abs(den), e)


def _mlstm_body(nchunk, qf_ref, kf_ref, vf_ref, rowsf_ref, colsf_ref, qb_ref, kb_ref, vb_ref, rowsb_ref,
                colsb_ref, hf_ref, hb_ref, cf_ref, cb_ref):
    @pl.when(pl.program_id(2) == 0)
    def _():
        cf_ref[...] = jnp.zeros_like(cf_ref)
        cb_ref[...] = jnp.zeros_like(cb_ref)

    fwd = _MlstmScan(False, nchunk, qf_ref, kf_ref, vf_ref, rowsf_ref, colsf_ref, hf_ref, cf_ref)
    bwd = _MlstmScan(True, nchunk, qb_ref, kb_ref, vb_ref, rowsb_ref, colsb_ref, hb_ref, cb_ref)
    fwd.input_products()
    bwd.input_products()
    fwd.states()
    bwd.states()
    fwd.read_out()
    bwd.read_out()


def _mlstm_call(mq, mk, mv, rows, cols, B, S, tr):
    ns = S // tr

    def specs(pos):
        return [pl.BlockSpec((1, 1, tr, M_QK_DIM), lambda b, h, s: (b, h, pos(s), 0)),
                pl.BlockSpec((1, 1, tr, M_QK_DIM), lambda b, h, s: (b, h, pos(s), 0)),
                pl.BlockSpec((1, 1, tr, M_V_DIM), lambda b, h, s: (b, h, pos(s), 0)),
                pl.BlockSpec((1, 1, 8, tr), lambda b, h, s: (b, h, 0, pos(s))),
                pl.BlockSpec((1, 1, tr, LANES), lambda b, h, s: (b, h, pos(s), 0))]

    fpos = lambda s: s
    bpos = lambda s: ns - 1 - s
    state = pltpu.VMEM((M_QK_DIM, M_V_DIM + LANES), F32)
    return pl.pallas_call(
        functools.partial(_mlstm_body, tr // M_CHUNK),
        grid=(B, M_HEADS, ns),
        in_specs=specs(fpos) + specs(bpos),
        out_specs=[pl.BlockSpec((1, tr, M_V_DIM), lambda b, h, s: (b, fpos(s), h)),
                   pl.BlockSpec((1, tr, M_V_DIM), lambda b, h, s: (b, bpos(s), h))],
        out_shape=[jax.ShapeDtypeStruct((B, S, M_WIDTH), F32),
                   jax.ShapeDtypeStruct((B, S, M_WIDTH), F32)],
        scratch_shapes=[state, state],
        compiler_params=pltpu.CompilerParams(dimension_semantics=("arbitrary", "arbitrary", "arbitrary"),
                                             vmem_limit_bytes=VMEM_LIMIT),
        name="mlstm",
    )(mq, mk, mv, rows, cols, mq, mk, mv, rows, cols)


def _attn_body(nkc, nt, tq, tk, tv, lam_init, q_ref, k_ref, vt_ref, lq_ref, g_ref, o_ref,
               qz_ref, st0_ref, st1_ref, m_ref, acc_ref):
    def tile_rows(t):
        return pl.ds(pl.multiple_of(t * tq, tq), tq)

    def load_queries(t):
        q = q_ref[0, 0, tile_rows(t), :]
        lane = lax.broadcasted_iota(jnp.int32, q.shape, 1)
        zero = jnp.zeros_like(q)
        qz_ref[0:tq, :] = jnp.where(lane < A_HEAD_DIM, q, zero)
        qz_ref[tq:2 * tq, :] = jnp.where(lane >= A_HEAD_DIM, q, zero)

    def reset_stats():
        m_ref[...] = jnp.full(m_ref.shape, -jnp.inf, F32)
        acc_ref[...] = jnp.zeros_like(acc_ref)

    strips = [slice(c * MXU_N, (c + 1) * MXU_N) for c in range(2 * tq // MXU_N)]
    nv = tk // tv

    def score_strip(j, st_ref, sl):
        kj = k_ref[0, 0, pl.ds(pl.multiple_of(j * tk, tk), tk), :]
        st = lax.dot_general(kj, qz_ref[sl, :], _NT, preferred_element_type=F32)
        st_ref[:, sl] = st
        return jnp.max(st, axis=0, keepdims=True)

    def accumulate_strip(j, st_ref, sl, cm):
        m_old = m_ref[:, sl]
        m_new = jnp.maximum(m_old, cm)
        alpha = jnp.exp2(m_old - m_new)
        p = jnp.exp2(st_ref[:, sl] - m_new)
        pv = jnp.dot(vt_ref[0, j * nv, 0], p[0:tv], preferred_element_type=F32)
        for u in range(1, nv):
            pv = pv + jnp.dot(vt_ref[0, j * nv + u, 0], p[u * tv:(u + 1) * tv],
                              preferred_element_type=F32)
        acc_ref[:, sl] = alpha * acc_ref[:, sl] + pv
        m_ref[:, sl] = m_new

    def step(j_next, st_next_ref, j_cur, st_cur_ref, cmax_cur):
        cmax_next = []
        for sl, cm in zip(strips, cmax_cur):
            cmax_next.append(score_strip(j_next, st_next_ref, sl))
            accumulate_strip(j_cur, st_cur_ref, sl, cm)
        return tuple(cmax_next)

    def pair(i, cmax0):
        j = 2 * i
        cmax1 = step(j + 1, st1_ref, j, st0_ref, cmax0)
        return step(j + 2, st0_ref, j + 1, st1_ref, cmax1)

    def two_pairs(i, cmax0):
        return pair(2 * i + 1, pair(2 * i, cmax0))

    def all_but_last_chunk(cmax0):
        npairs = nkc // 2 - 1
        cmax0 = lax.fori_loop(0, npairs // 2, two_pairs, cmax0)
        if npairs % 2:
            cmax0 = pair(npairs - 1, cmax0)
        return step(nkc - 1, st1_ref, nkc - 2, st0_ref, cmax0)

    lq = lq_ref[...]
    lam = (jnp.exp(jnp.sum(lq[0:1] * lq[1:2], axis=1, keepdims=True))
           - jnp.exp(jnp.sum(lq[2:3] * lq[3:4], axis=1, keepdims=True)) + lam_init)

    def finalize(t):
        accn = acc_ref[0:A_V_DIM, :] / acc_ref[A_V_DIM:A_V_DIM + 1, :]
        ot = accn[:, 0:tq] - lam * accn[:, tq:2 * tq]
        otn = ot * lax.rsqrt(jnp.mean(ot * ot, axis=0, keepdims=True) + EPS)
        o_ref[tile_rows(t), :] = (otn.T * g_ref[...]) * (1.0 - lam_init)

    def tile(t, cmax0):
        cmax1 = all_but_last_chunk(cmax0)
        load_queries(t + 1)
        cmax_next = []
        for sl, cm in zip(strips, cmax1):
            cmax_next.append(score_strip(0, st0_ref, sl))
            accumulate_strip(nkc - 1, st1_ref, sl, cm)
        finalize(t)
        reset_stats()
        return tuple(cmax_next)

    load_queries(0)
    reset_stats()
    cmax0 = tuple(score_strip(0, st0_ref, sl) for sl in strips)
    cmax0 = lax.fori_loop(0, nt - 1, tile, cmax0)
    cmax1 = all_but_last_chunk(cmax0)
    for sl, cm in zip(strips, cmax1):
        accumulate_strip(nkc - 1, st1_ref, sl, cm)
    finalize(nt - 1)


def _attn_call(aq, ak, avt, lq, ag, B, S, tq, nt, tk, tv, lam_init):
    nkc = S // tk
    nq = S // (nt * tq)
    assert nkc >= 2 and nkc % 2 == 0, "key chunks are processed in pairs"
    assert tk % tv == 0 and S % (nt * tq) == 0
    return pl.pallas_call(
        functools.partial(_attn_body, nkc, nt, tq, tk, tv, lam_init),
        grid=(B, A_HEADS, nq),
        in_specs=[pl.BlockSpec((1, 1, nt * tq, LANES), lambda b, h, i: (b, h, i, 0)),
                  pl.BlockSpec((1, 1, S, LANES), lambda b, h, i: (b, h, 0, 0)),
                  pl.BlockSpec((1, S // tv, 1, A_V_ROWS, tv), lambda b, h, i: (b, 0, h, 0, 0)),
                  pl.BlockSpec(lq.shape, lambda b, h, i: (0, 0)),
                  pl.BlockSpec(ag.shape, lambda b, h, i: (0, 0))],
        out_specs=pl.BlockSpec((nt * tq, A_V_DIM), lambda b, h, i: (b * nq + i, h)),
        out_shape=jax.ShapeDtypeStruct((B * S, A_WIDTH), F32),
        scratch_shapes=[pltpu.VMEM((2 * tq, LANES), BF16),
                        pltpu.VMEM((tk, 2 * tq), F32),
                        pltpu.VMEM((tk, 2 * tq), F32),
                        pltpu.VMEM((1, 2 * tq), F32),
                        pltpu.VMEM((A_V_ROWS, 2 * tq), F32)],
        compiler_params=pltpu.CompilerParams(dimension_semantics=("arbitrary", "arbitrary", "arbitrary"),
                                             vmem_limit_bytes=VMEM_LIMIT),
        name="attn",
    )(aq, ak, avt, lq, ag)


def _epilogue_body(x_ref, hf_ref, hb_ref, o_ref, preg_ref, w4_ref, mng_ref, bm_ref, wbm_ref, wba_ref,
                   wout_ref, postg_ref, out_ref):
    x = x_ref[...]
    xn = (_rms(x) * preg_ref[...]).astype(BF16)

    def proj(lo, hi):
        return jnp.dot(xn, w4_ref[:, lo:hi], preferred_element_type=F32)

    h = hf_ref[...] + hb_ref[...]
    mng = mng_ref[...]
    hn = jnp.concatenate(
        [_rms(h[:, k * M_V_DIM:(k + 1) * M_V_DIM]) * mng[:, k * M_V_DIM:(k + 1) * M_V_DIM]
         for k in range(M_HEADS)], axis=1)
    h_a = jax.nn.sigmoid(proj(0, M_WIDTH)) * hn * _silu(proj(M_WIDTH, 2 * M_WIDTH))
    ya = jnp.dot(h_a.astype(BF16), wbm_ref[...], preferred_element_type=F32)
    h_b = o_ref[...] * _silu(proj(2 * M_WIDTH, 2 * M_WIDTH + A_WIDTH))
    yb = jnp.dot(h_b.astype(BF16), wba_ref[...], preferred_element_type=F32)
    g0 = 2 * M_WIDTH + A_WIDTH
    bm = bm_ref[...]
    g_a = jax.nn.sigmoid(proj(g0, g0 + D_MODEL) + bm[:, 0:D_MODEL])
    g_b = jax.nn.sigmoid(proj(g0 + D_MODEL, g0 + 2 * D_MODEL) + bm[:, D_MODEL:2 * D_MODEL])
    y = g_a * ya + g_b * yb
    out = jnp.dot(y.astype(BF16), wout_ref[...], preferred_element_type=F32)
    out_ref[...] = x + _rms(out) * postg_ref[...]


def _epilogue_call(x2d, hf, hb, o, preg, w4, mng, bm, wbm, wba, wout, postg, te):
    T = x2d.shape[0]
    row = lambda i: (i, 0)
    const = lambda i: (0, 0)
    single = pl.Buffered(1)
    return pl.pallas_call(
        _epilogue_body,
        grid=(T // te,),
        in_specs=[pl.BlockSpec((te, D_MODEL), row),
                  pl.BlockSpec((te, M_WIDTH), row),
                  pl.BlockSpec((te, M_WIDTH), row),
                  pl.BlockSpec((te, A_WIDTH), row),
                  pl.BlockSpec(preg.shape, const),
                  pl.BlockSpec(w4.shape, const, pipeline_mode=single),
                  pl.BlockSpec(mng.shape, const),
                  pl.BlockSpec(bm.shape, const),
                  pl.BlockSpec(wbm.shape, const, pipeline_mode=single),
                  pl.BlockSpec(wba.shape, const, pipeline_mode=single),
                  pl.BlockSpec(wout.shape, const, pipeline_mode=single),
                  pl.BlockSpec(postg.shape, const)],
        out_specs=pl.BlockSpec((te, D_MODEL), row),
        out_shape=jax.ShapeDtypeStruct((T, D_MODEL), F32),
        compiler_params=pltpu.CompilerParams(dimension_semantics=("arbitrary",),
                                             vmem_limit_bytes=VMEM_LIMIT),
        name="epilogue",
    )(x2d, hf, hb, o, preg, w4, mng, bm, wbm, wba, wout, postg)


def _rope_tables(S):
    f32 = np.float32
    inv = ROPE_THETA ** (-np.arange(0, ROPE_DIMS, 2, dtype=np.float64) / ROPE_DIMS)
    ang = np.arange(S, dtype=np.float64)[:, None] * inv[None, :]
    cos, sin = np.cos(ang).astype(f32), np.sin(ang).astype(f32)
    one = np.ones((S, A_HEAD_DIM - ROPE_DIMS), f32)
    zero = np.zeros((S, A_HEAD_DIM - ROPE_DIMS), f32)
    rc = np.concatenate([cos, cos, one], axis=1)
    rs = np.concatenate([-sin, sin, zero], axis=1)
    rep = LANES // A_HEAD_DIM
    return jnp.asarray(np.tile(rc, (1, rep))), jnp.asarray(np.tile(rs, (1, rep)))


def _gate_layout(m):
    return jnp.concatenate([m[..., 1, :], m[..., 3, :], m[..., 0, :], m[..., 2, :]], axis=-1)


def _layer(x, layer_idx, pre_g, w_in, conv_w, conv_b, gate_b, m_norm_g, lambda_qk, a_norm_g,
           w_branch_m, w_branch_a, b_merge, w_out, post_g):
    B, S, _ = x.shape
    T = B * S
    tm = min(512, S)
    tq = min(2048, S // 2)
    nt = min(2, S // tq)
    tk = min(512, S // 2)
    tr = min(512, S)
    te = min(256, T)
    lam_init = 0.8 - 0.6 * math.exp(-0.3 * layer_idx)

    cuts = [0]
    for sz in SPLIT_SIZES:
        cuts.append(cuts[-1] + sz)
    col = lambda k: w_in[:, cuts[k]:cuts[k + 1]]
    wmqk = col(0).astype(BF16)
    w1 = jnp.concatenate([col(1), col(5), col(6)], axis=1).astype(BF16)
    wavt = col(7).T.astype(BF16)
    wgt = _gate_layout(col(2).reshape(D_MODEL, 4, M_HEADS)).T.astype(BF16)
    gb = _gate_layout(gate_b).reshape(N_GATE_ROWS, 1).astype(F32)
    w4 = jnp.concatenate([col(3), col(4), col(8), col(9)], axis=1).astype(BF16)
    rc, rs = _rope_tables(S)

    x2d = x.reshape(T, D_MODEL)
    preg = pre_g.reshape(1, D_MODEL)
    mq, mk, mv, gt, aq, ak, avt = _inproj_call(
        x2d, preg, wmqk, w1, wavt, wgt, conv_w, conv_b.reshape(1, 2 * M_QK), rc, rs, B, S, tm)
    rows, cols = _gates_call(gt, gb, B, S)
    hf, hb = _mlstm_call(mq, mk, mv, rows, cols, B, S, tr)
    o = _attn_call(aq, ak, avt, lambda_qk.astype(F32), a_norm_g.reshape(1, A_V_DIM), B, S, tq, nt, tk, tm,
                   lam_init)
    out = _epilogue_call(
        x2d, hf.reshape(T, M_WIDTH), hb.reshape(T, M_WIDTH), o, preg, w4,
        m_norm_g.reshape(1, M_WIDTH), b_merge.reshape(1, 2 * D_MODEL),
        w_branch_m.astype(BF16), w_branch_a.astype(BF16), w_out.astype(BF16),
        post_g.reshape(1, D_MODEL), te)
    return out.reshape(B, S, D_MODEL)


def kernel(x, pre_norm_g, w_in, conv_w, conv_b, gate_b, m_norm_g, lambda_qk, a_norm_g, w_branch_m,
           w_branch_a, b_merge, w_out, post_norm_g):
    for l in range(pre_norm_g.shape[0]):
        x = _layer(x, l, pre_norm_g[l], w_in[l], conv_w[l], conv_b[l], gate_b[l], m_norm_g[l],
                   lambda_qk[l], a_norm_g[l], w_branch_m[l], w_branch_a[l], b_merge[l], w_out[l],
                   post_norm_g[l])
    return x
```

```python
import functools
import math

import jax
import jax.numpy as jnp
import numpy as np
from jax import lax
from jax.experimental import pallas as pl
from jax.experimental.pallas import tpu as pltpu

D_MODEL = 1024
M_HEADS = 4
M_QK_DIM = 128
M_V_DIM = 256
M_QK = M_HEADS * M_QK_DIM
M_WIDTH = M_HEADS * M_V_DIM
M_CHUNK = 64
CONV_WIDTH = 4
A_HEADS = 8
A_HEAD_DIM = 64
A_V_DIM = 2 * A_HEAD_DIM
A_QK = A_HEADS * 2 * A_HEAD_DIM
A_WIDTH = A_HEADS * A_V_DIM
ROPE_THETA = 500000.0
ROPE_DIMS = A_HEAD_DIM // 4
EPS = 1e-6
SPLIT_SIZES = (2 * M_QK, M_WIDTH, 4 * M_HEADS, M_WIDTH, M_WIDTH, A_QK, A_QK, A_WIDTH, A_WIDTH, 2 * D_MODEL)

LANES = 128
MXU_N = 256
F32_ROWS = 8
A_V_ROWS = A_V_DIM + F32_ROWS
HALO = 16
N_GATE_ROWS = 2 * 2 * M_HEADS
VMEM_LIMIT = 60 * 1024 * 1024

F32 = jnp.float32
BF16 = jnp.bfloat16
_NT = (((1,), (1,)), ((), ()))
_TN = (((0,), (0,)), ((), ()))


def _rms(xf):
    return xf * lax.rsqrt(jnp.mean(xf * xf, axis=-1, keepdims=True) + EPS)


def _silu(y):
    return y * jax.nn.sigmoid(y)


def _inproj_body(tps, tm, xprev_ref, x_ref, xnext_ref, preg_ref, wmqk_ref, w1_ref, wavt_ref, wgt_ref,
                 convw_ref, convb_ref, rc_ref, rs_ref,
                 mq_ref, mk_ref, mv_ref, gt_ref, aq_ref, ak_ref, avt_ref, xext_ref):
    t = lax.rem(pl.program_id(0), tps)
    g = preg_ref[...]
    xn = (_rms(x_ref[...]) * g).astype(BF16)
    xp = jnp.where(t != 0, _rms(xprev_ref[...]) * g, 0.0).astype(BF16)
    xq = jnp.where(t != tps - 1, _rms(xnext_ref[...]) * g, 0.0).astype(BF16)
    xext_ref[0:HALO, :] = xp
    xext_ref[HALO:HALO + tm, :] = xn
    xext_ref[HALO + tm:, :] = xq
    n_ext = tm + 2 * HALO
    ext = jnp.dot(xext_ref[...], wmqk_ref[...], preferred_element_type=F32)
    mv = jnp.dot(xn, w1_ref[:, 0:M_WIDTH], preferred_element_type=F32)
    aq = jnp.dot(xn, w1_ref[:, M_WIDTH:M_WIDTH + A_QK], preferred_element_type=F32)
    ak = jnp.dot(xn, w1_ref[:, M_WIDTH + A_QK:M_WIDTH + 2 * A_QK], preferred_element_type=F32)
    avt = lax.dot_general(wavt_ref[...], xn, _NT, preferred_element_type=F32)
    gt_ref[...] = lax.dot_general(wgt_ref[...], xn, _NT, preferred_element_type=F32)

    left = CONV_WIDTH // 2
    y = None
    for j in range(CONV_WIDTH):
        sh = (left - j) % n_ext
        r = ext if sh == 0 else pltpu.roll(ext, sh, axis=0)
        term = convw_ref[j:j + 1, :] * r[HALO:HALO + tm, :]
        y = term if y is None else y + term
    y = _silu(y + convb_ref[...])
    qscale = M_QK_DIM ** -0.5
    for h in range(M_HEADS):
        mq_ref[0, h] = (y[:, h * M_QK_DIM:(h + 1) * M_QK_DIM] * qscale).astype(BF16)
        mk_ref[0, h] = y[:, M_QK + h * M_QK_DIM:M_QK + (h + 1) * M_QK_DIM].astype(BF16)
    for h in range(M_HEADS):
        mv_ref[0, h] = mv[:, h * M_V_DIM:(h + 1) * M_V_DIM].astype(BF16)
    rc, rs = rc_ref[...], rs_ref[...]
    half = ROPE_DIMS // 2
    first_half = (lax.broadcasted_iota(jnp.int32, rc.shape, 1) & (A_HEAD_DIM - 1)) < half

    def rope(a):
        partner = jnp.where(first_half, pltpu.roll(a, LANES - half, axis=1), pltpu.roll(a, half, axis=1))
        return a * rc + partner * rs

    ascale = A_HEAD_DIM ** -0.5 * math.log2(math.e)
    for h in range(A_HEADS):
        aq_ref[0, h] = (rope(aq[:, h * LANES:(h + 1) * LANES]) * ascale).astype(BF16)
        ak_ref[0, h] = rope(ak[:, h * LANES:(h + 1) * LANES]).astype(BF16)
    ones_tile = (lax.broadcasted_iota(jnp.int32, (F32_ROWS, tm), 0) == 0).astype(F32)
    for h in range(A_HEADS):
        avt_ref[0, 0, h, 0:A_V_DIM, :] = avt[h * A_V_DIM:(h + 1) * A_V_DIM, :].astype(BF16).astype(F32)
        avt_ref[0, 0, h, A_V_DIM:A_V_ROWS, :] = ones_tile


def _inproj_call(x2d, preg, wmqk, w1, wavt, wgt, convw, convb, rc, rs, B, S, tm):
    T = B * S
    tps = S // tm
    nh = tm // HALO
    const = lambda i: (0, 0)
    in_specs = [
        pl.BlockSpec((HALO, D_MODEL), lambda i: (jnp.maximum(i * nh - 1, 0), 0)),
        pl.BlockSpec((tm, D_MODEL), lambda i: (i, 0)),
        pl.BlockSpec((HALO, D_MODEL), lambda i: (jnp.minimum((i + 1) * nh, T // HALO - 1), 0)),
        pl.BlockSpec((1, D_MODEL), const),
        pl.BlockSpec(wmqk.shape, const),
        pl.BlockSpec(w1.shape, const),
        pl.BlockSpec(wavt.shape, const),
        pl.BlockSpec(wgt.shape, const),
        pl.BlockSpec(convw.shape, const),
        pl.BlockSpec(convb.shape, const),
        pl.BlockSpec((tm, LANES), lambda i: (lax.rem(i, tps), 0)),
        pl.BlockSpec((tm, LANES), lambda i: (lax.rem(i, tps), 0)),
    ]
    bt = lambda i: (i // tps, 0, lax.rem(i, tps), 0)
    out_shape = [
        jax.ShapeDtypeStruct((B, M_HEADS, S, M_QK_DIM), BF16),
        jax.ShapeDtypeStruct((B, M_HEADS, S, M_QK_DIM), BF16),
        jax.ShapeDtypeStruct((B, M_HEADS, S, M_V_DIM), BF16),
        jax.ShapeDtypeStruct((N_GATE_ROWS, T), F32),
        jax.ShapeDtypeStruct((B, A_HEADS, S, LANES), BF16),
        jax.ShapeDtypeStruct((B, A_HEADS, S, LANES), BF16),
        jax.ShapeDtypeStruct((B, tps, A_HEADS, A_V_ROWS, tm), F32),
    ]
    out_specs = [
        pl.BlockSpec((1, M_HEADS, tm, M_QK_DIM), bt),
        pl.BlockSpec((1, M_HEADS, tm, M_QK_DIM), bt),
        pl.BlockSpec((1, M_HEADS, tm, M_V_DIM), bt),
        pl.BlockSpec((N_GATE_ROWS, tm), lambda i: (0, i)),
        pl.BlockSpec((1, A_HEADS, tm, LANES), bt),
        pl.BlockSpec((1, A_HEADS, tm, LANES), bt),
        pl.BlockSpec((1, 1, A_HEADS, A_V_ROWS, tm), lambda i: (i // tps, lax.rem(i, tps), 0, 0, 0)),
    ]
    return pl.pallas_call(
        functools.partial(_inproj_body, tps, tm),
        grid=(T // tm,),
        in_specs=in_specs,
        out_specs=out_specs,
        out_shape=out_shape,
        scratch_shapes=[pltpu.VMEM((tm + 2 * HALO, D_MODEL), BF16)],
        compiler_params=pltpu.CompilerParams(dimension_semantics=("arbitrary",),
                                             vmem_limit_bytes=VMEM_LIMIT),
        name="inproj",
    )(x2d, x2d, x2d, preg, wmqk, w1, wavt, wgt, convw, convb, rc, rs)


N_GATE_COLS = 5


def _gates_body(S, ct, f_ref, i_ref, fb_ref, ib_ref, rows_ref, cols_ref, scan_ref, stage_ref):
    h = pl.program_id(1)

    @pl.when(h == 0)
    def _():
        _gate_scans(S, f_ref, i_ref, fb_ref, ib_ref, scan_ref)

    rows_ref[...] = jnp.zeros_like(rows_ref)
    stage_ref[...] = jnp.zeros_like(stage_ref)
    for d in range(2):
        rows_ref[0, 0, d:d + 1, :] = scan_ref[N_GATE_COLS, pl.ds(d * M_HEADS + h, 1), :]
    for t in range(S // ct):
        sl = slice(t * ct, (t + 1) * ct)
        for k in range(N_GATE_COLS):
            for d in range(2):
                stage_ref[2 * k + d:2 * k + d + 1, :] = scan_ref[k, pl.ds(d * M_HEADS + h, 1), sl]
        cols_ref[0, 0, sl, :] = stage_ref[...].T


def _gate_scans(S, f_ref, i_ref, fb_ref, ib_ref, scan_ref):
    L = M_CHUNK
    f = f_ref[...] + fb_ref[...]
    ig = i_ref[...] + ib_ref[...]
    lf = jnp.minimum(f, 0.0) - jnp.log1p(jnp.exp(-jnp.abs(f)))
    lane = lax.broadcasted_iota(jnp.int32, f.shape, 1)
    pos = lane & (L - 1)
    fwd = lax.broadcasted_iota(jnp.int32, f.shape, 0) < M_HEADS

    def from_left(x, s):
        return pltpu.roll(x, s, axis=1)

    def from_right(x, s):
        return pltpu.roll(x, S - s, axis=1)

    def chunk_scans(x, op, ident):
        pre, suf = x, x
        s = 1
        while s < L:
            pre = op(pre, jnp.where(pos >= s, from_left(pre, s), ident))
            suf = op(suf, jnp.where(pos < L - s, from_right(suf, s), ident))
            s *= 2
        return pre, suf

    pre, suf = chunk_scans(lf, jnp.add, 0.0)
    a = jnp.where(fwd, pre, suf)
    a_tot = pre + suf - lf
    w = ig - a
    wpre, wsuf = chunk_scans(w, jnp.maximum, -jnp.inf)
    rowmax = a + jnp.where(fwd, wpre, wsuf)
    g_max = a_tot + jnp.maximum(wpre, wsuf)

    af, gf, ab, gb = a_tot, g_max, a_tot, g_max
    s = L
    while s < S:
        ok = lane >= s
        a_sh, g_sh = from_left(af, s), from_left(gf, s)
        gf = jnp.where(ok, jnp.maximum(g_sh + af, gf), gf)
        af = jnp.where(ok, af + a_sh, af)
        ok = lane < S - s
        a_sh, g_sh = from_right(ab, s), from_right(gb, s)
        gb = jnp.where(ok, jnp.maximum(g_sh + ab, gb), gb)
        ab = jnp.where(ok, ab + a_sh, ab)
        s *= 2
    m_f = jnp.maximum(af, gf)
    m_b = jnp.maximum(ab, gb)
    m_prev = jnp.where(fwd, jnp.where(lane >= L, from_left(m_f, L), 0.0),
                       jnp.where(lane < S - L, from_right(m_b, L), 0.0))

    inter = a + m_prev
    mj = jnp.maximum(inter, rowmax)
    m_new = jnp.maximum(a_tot + m_prev, g_max)
    scan_ref[0] = a - mj
    scan_ref[1] = jnp.exp(inter - mj)
    scan_ref[2] = jnp.exp(-mj)
    scan_ref[3] = jnp.exp(a_tot + w - m_new)
    scan_ref[4] = jnp.exp(a_tot + m_prev - m_new)
    scan_ref[N_GATE_COLS] = w


def _gates_call(gt, gb, B, S):
    ct = min(2048, S)
    return pl.pallas_call(
        functools.partial(_gates_body, S, ct),
        grid=(B, M_HEADS),
        in_specs=[pl.BlockSpec((8, S), lambda b, h: (0, b)),
                  pl.BlockSpec((8, S), lambda b, h: (1, b)),
                  pl.BlockSpec((8, 1), lambda b, h: (0, 0)),
                  pl.BlockSpec((8, 1), lambda b, h: (1, 0))],
        out_specs=[pl.BlockSpec((1, 1, 8, S), lambda b, h: (b, h, 0, 0)),
                   pl.BlockSpec((1, 1, S, LANES), lambda b, h: (b, h, 0, 0))],
        out_shape=[jax.ShapeDtypeStruct((B, M_HEADS, 8, S), F32),
                   jax.ShapeDtypeStruct((B, M_HEADS, S, LANES), F32)],
        scratch_shapes=[pltpu.VMEM((N_GATE_COLS + 1, 8, S), F32),
                        pltpu.VMEM((LANES, ct), F32)],
        compiler_params=pltpu.CompilerParams(dimension_semantics=("arbitrary", "arbitrary"),
                                             vmem_limit_bytes=VMEM_LIMIT),
        name="gates",
    )(gt, gt, gb, gb)


class _MlstmScan:
    def __init__(self, rev, nchunk, q_ref, k_ref, v_ref, rows_ref, cols_ref, h_ref, c_ref):
        self.rev, self.d = rev, (1 if rev else 0)
        self.q_ref, self.k_ref, self.v_ref = q_ref, k_ref, v_ref
        self.rows_ref, self.cols_ref, self.h_ref, self.c_ref = rows_ref, cols_ref, h_ref, c_ref
        self.order = list(range(nchunk - 1, -1, -1) if rev else range(nchunk))
        L = M_CHUNK
        row_i = lax.broadcasted_iota(jnp.int32, (L, L), 0)
        col_i = lax.broadcasted_iota(jnp.int32, (L, L), 1)
        self.visible = (col_i >= row_i) if rev else (col_i <= row_i)
        self.ones_col = (lax.broadcasted_iota(jnp.int32, (L, LANES), 1) == 0).astype(BF16)

    @staticmethod
    def rows_of(c):
        return slice(c * M_CHUNK, (c + 1) * M_CHUNK)

    def gate_col(self, c, k):
        return self.cols_ref[0, 0, self.rows_of(c), 2 * k + self.d:2 * k + self.d + 1]

    def v_ext(self, c):
        return jnp.concatenate([self.v_ref[0, 0, self.rows_of(c), :], self.ones_col], axis=1)

    def input_products(self):
        self.qk, self.kv = {}, {}
        for c in self.order:
            q, k = self.q_ref[0, 0, self.rows_of(c), :], self.k_ref[0, 0, self.rows_of(c), :]
            self.qk[c] = lax.dot_general(q, k, _NT, preferred_element_type=F32)
            wv = (self.gate_col(c, 3) * self.v_ext(c).astype(F32)).astype(BF16)
            self.kv[c] = lax.dot_general(k, wv, _TN, preferred_element_type=F32)

    def states(self):
        C = self.c_ref[...]
        self.c_in = {}
        for c in self.order:
            self.c_in[c] = C.astype(BF16)
            C = self.gate_col(c, 4)[0:1, :] * C + self.kv[c]
        self.c_ref[...] = C

    def read_out(self):
        for c in self.order:
            u, w_inter, e = self.gate_col(c, 0), self.gate_col(c, 1), self.gate_col(c, 2)
            w_row = self.rows_ref[0, 0, self.d:self.d + 1, self.rows_of(c)]
            p = jnp.exp(jnp.where(self.visible, u + w_row, -jnp.inf))
            s = self.qk[c] * p
            q = self.q_ref[0, 0, self.rows_of(c), :]
            num = (w_inter * jnp.dot(q, self.c_in[c], preferred_element_type=F32)
                   + jnp.dot(s.astype(BF16), self.v_ext(c), preferred_element_type=F32))
            den = num[:, M_V_DIM:M_V_DIM + 1]
            self.h_ref[0, self.rows_of(c), :] = num[:, 0:M_V_DIM] * (1.0 / jnp.maximum(jnp.abs(den), e))


def _mlstm_body(nchunk, qf_ref, kf_ref, vf_ref, rowsf_ref, colsf_ref, qb_ref, kb_ref, vb_ref, rowsb_ref,
                colsb_ref, hf_ref, hb_ref, cf_ref, cb_ref):
    @pl.when(pl.program_id(2) == 0)
    def _():
        cf_ref[...] = jnp.zeros_like(cf_ref)
        cb_ref[...] = jnp.zeros_like(cb_ref)

    fwd = _MlstmScan(False, nchunk, qf_ref, kf_ref, vf_ref, rowsf_ref, colsf_ref, hf_ref, cf_ref)
    bwd = _MlstmScan(True, nchunk, qb_ref, kb_ref, vb_ref, rowsb_ref, colsb_ref, hb_ref, cb_ref)
    fwd.input_products()
    bwd.input_products()
    fwd.states()
    bwd.states()
    fwd.read_out()
    bwd.read_out()


def _mlstm_call(mq, mk, mv, rows, cols, B, S, tr):
    ns = S // tr

    def specs(pos):
        return [pl.BlockSpec((1, 1, tr, M_QK_DIM), lambda b, h, s: (b, h, pos(s), 0)),
                pl.BlockSpec((1, 1, tr, M_QK_DIM), lambda b, h, s: (b, h, pos(s), 0)),
                pl.BlockSpec((1, 1, tr, M_V_DIM), lambda b, h, s: (b, h, pos(s), 0)),
                pl.BlockSpec((1, 1, 8, tr), lambda b, h, s: (b, h, 0, pos(s))),
                pl.BlockSpec((1, 1, tr, LANES), lambda b, h, s: (b, h, pos(s), 0))]

    fpos = lambda s: s
    bpos = lambda s: ns - 1 - s
    state = pltpu.VMEM((M_QK_DIM, M_V_DIM + LANES), F32)
    return pl.pallas_call(
        functools.partial(_mlstm_body, tr // M_CHUNK),
        grid=(B, M_HEADS, ns),
        in_specs=specs(fpos) + specs(bpos),
        out_specs=[pl.BlockSpec((1, tr, M_V_DIM), lambda b, h, s: (b, fpos(s), h)),
                   pl.BlockSpec((1, tr, M_V_DIM), lambda b, h, s: (b, bpos(s), h))],
        out_shape=[jax.ShapeDtypeStruct((B, S, M_WIDTH), F32),
                   jax.ShapeDtypeStruct((B, S, M_WIDTH), F32)],
        scratch_shapes=[state, state],
        compiler_params=pltpu.CompilerParams(dimension_semantics=("arbitrary", "arbitrary", "arbitrary"),
                                             vmem_limit_bytes=VMEM_LIMIT),
        name="mlstm",
    )(mq, mk, mv, rows, cols, mq, mk, mv, rows, cols)


def _attn_body(nkc, nt, tq, tk, tv, lam_init, q_ref, k_ref, vt_ref, lq_ref, g_ref, o_ref,
               qz_ref, st0_ref, st1_ref, m_ref, acc_ref):
    def tile_rows(t):
        return pl.ds(pl.multiple_of(t * tq, tq), tq)

    def load_queries(t):
        q = q_ref[0, 0, tile_rows(t), :]
        lane = lax.broadcasted_iota(jnp.int32, q.shape, 1)
        zero = jnp.zeros_like(q)
        qz_ref[0:tq, :] = jnp.where(lane < A_HEAD_DIM, q, zero)
        qz_ref[tq:2 * tq, :] = jnp.where(lane >= A_HEAD_DIM, q, zero)

    def reset_stats():
        m_ref[...] = jnp.full(m_ref.shape, -jnp.inf, F32)
        acc_ref[...] = jnp.zeros_like(acc_ref)

    strips = [slice(c * MXU_N, (c + 1) * MXU_N) for c in range(2 * tq // MXU_N)]
    nv = tk // tv

    def score_strip(j, st_ref, sl):
        kj = k_ref[0, 0, pl.ds(pl.multiple_of(j * tk, tk), tk), :]
        st = lax.dot_general(kj, qz_ref[sl, :], _NT, preferred_element_type=F32)
        st_ref[:, sl] = st
        return jnp.max(st, axis=0, keepdims=True)

    def accumulate_strip(j, st_ref, sl, cm):
        m_old = m_ref[:, sl]
        m_new = jnp.maximum(m_old, cm)
        alpha = jnp.exp2(m_old - m_new)
        p = jnp.exp2(st_ref[:, sl] - m_new)
        pv = jnp.dot(vt_ref[0, j * nv, 0], p[0:tv], preferred_element_type=F32)
        for u in range(1, nv):
            pv = pv + jnp.dot(vt_ref[0, j * nv + u, 0], p[u * tv:(u + 1) * tv],
                              preferred_element_type=F32)
        acc_ref[:, sl] = alpha * acc_ref[:, sl] + pv
        m_ref[:, sl] = m_new

    def step(j_next, st_next_ref, j_cur, st_cur_ref, cmax_cur):
        cmax_next = []
        for sl, cm in zip(strips, cmax_cur):
            cmax_next.append(score_strip(j_next, st_next_ref, sl))
            accumulate_strip(j_cur, st_cur_ref, sl, cm)
        return tuple(cmax_next)

    def pair(i, cmax0):
        j = 2 * i
        cmax1 = step(j + 1, st1_ref, j, st0_ref, cmax0)
        return step(j + 2, st0_ref, j + 1, st1_ref, cmax1)

    def two_pairs(i, cmax0):
        return pair(2 * i + 1, pair(2 * i, cmax0))

    def all_but_last_chunk(cmax0):
        npairs = nkc // 2 - 1
        cmax0 = lax.fori_loop(0, npairs // 2, two_pairs, cmax0)
        if npairs % 2:
            cmax0 = pair(npairs - 1, cmax0)
        return step(nkc - 1, st1_ref, nkc - 2, st0_ref, cmax0)

    lq = lq_ref[...]
    lam = (jnp.exp(jnp.sum(lq[0:1] * lq[1:2], axis=1, keepdims=True))
           - jnp.exp(jnp.sum(lq[2:3] * lq[3:4], axis=1, keepdims=True)) + lam_init)

    def finalize(t):
        accn = acc_ref[0:A_V_DIM, :] * (1.0 / acc_ref[A_V_DIM:A_V_DIM + 1, :])
        ot = accn[:, 0:tq] - lam * accn[:, tq:2 * tq]
        otn = ot * lax.rsqrt(jnp.mean(ot * ot, axis=0, keepdims=True) + EPS)
        o_ref[tile_rows(t), :] = (otn.T * g_ref[...]) * (1.0 - lam_init)

    def tile(t, cmax0):
        cmax1 = all_but_last_chunk(cmax0)
        load_queries(t + 1)
        cmax_next = []
        for sl, cm in zip(strips, cmax1):
            cmax_next.append(score_strip(0, st0_ref, sl))
            accumulate_strip(nkc - 1, st1_ref, sl, cm)
        finalize(t)
        reset_stats()
        return tuple(cmax_next)

    load_queries(0)
    reset_stats()
    cmax0 = tuple(score_strip(0, st0_ref, sl) for sl in strips)
    cmax0 = lax.fori_loop(0, nt - 1, tile, cmax0)
    cmax1 = all_but_last_chunk(cmax0)
    for sl, cm in zip(strips, cmax1):
        accumulate_strip(nkc - 1, st1_ref, sl, cm)
    finalize(nt - 1)


def _attn_call(aq, ak, avt, lq, ag, B, S, tq, nt, tk, tv, lam_init):
    nkc = S // tk
    nq = S // (nt * tq)
    assert nkc >= 2 and nkc % 2 == 0, "key chunks are processed in pairs"
    assert tk % tv == 0 and S % (nt * tq) == 0
    return pl.pallas_call(
        functools.partial(_attn_body, nkc, nt, tq, tk, tv, lam_init),
        grid=(B, A_HEADS, nq),
        in_specs=[pl.BlockSpec((1, 1, nt * tq, LANES), lambda b, h, i: (b, h, i, 0)),
                  pl.BlockSpec((1, 1, S, LANES), lambda b, h, i: (b, h, 0, 0)),
                  pl.BlockSpec((1, S // tv, 1, A_V_ROWS, tv), lambda b, h, i: (b, 0, h, 0, 0)),
                  pl.BlockSpec(lq.shape, lambda b, h, i: (0, 0)),
                  pl.BlockSpec(ag.shape, lambda b, h, i: (0, 0))],
        out_specs=pl.BlockSpec((nt * tq, A_V_DIM), lambda b, h, i: (b * nq + i, h)),
        out_shape=jax.ShapeDtypeStruct((B * S, A_WIDTH), F32),
        scratch_shapes=[pltpu.VMEM((2 * tq, LANES), BF16),
                        pltpu.VMEM((tk, 2 * tq), F32),
                        pltpu.VMEM((tk, 2 * tq), F32),
                        pltpu.VMEM((1, 2 * tq), F32),
                        pltpu.VMEM((A_V_ROWS, 2 * tq), F32)],
        compiler_params=pltpu.CompilerParams(dimension_semantics=("arbitrary", "arbitrary", "arbitrary"),
                                             vmem_limit_bytes=VMEM_LIMIT),
        name="attn",
    )(aq, ak, avt, lq, ag)


def _epilogue_body(x_ref, hf_ref, hb_ref, o_ref, preg_ref, w4_ref, mng_ref, bm_ref, wbm_ref, wba_ref,
                   wout_ref, postg_ref, out_ref):
    x = x_ref[...]
    xn = (_rms(x) * preg_ref[...]).astype(BF16)

    def proj(lo, hi):
        return jnp.dot(xn, w4_ref[:, lo:hi], preferred_element_type=F32)

    h = hf_ref[...] + hb_ref[...]
    mng = mng_ref[...]
    hn = jnp.concatenate(
        [_rms(h[:, k * M_V_DIM:(k + 1) * M_V_DIM]) * mng[:, k * M_V_DIM:(k + 1) * M_V_DIM]
         for k in range(M_HEADS)], axis=1)
    h_a = jax.nn.sigmoid(proj(0, M_WIDTH)) * hn * _silu(proj(M_WIDTH, 2 * M_WIDTH))
    ya = jnp.dot(h_a.astype(BF16), wbm_ref[...], preferred_element_type=F32)
    h_b = o_ref[...] * _silu(proj(2 * M_WIDTH, 2 * M_WIDTH + A_WIDTH))
    yb = jnp.dot(h_b.astype(BF16), wba_ref[...], preferred_element_type=F32)
    g0 = 2 * M_WIDTH + A_WIDTH
    bm = bm_ref[...]
    g_a = jax.nn.sigmoid(proj(g0, g0 + D_MODEL) + bm[:, 0:D_MODEL])
    g_b = jax.nn.sigmoid(proj(g0 + D_MODEL, g0 + 2 * D_MODEL) + bm[:, D_MODEL:2 * D_MODEL])
    y = g_a * ya + g_b * yb
    out = jnp.dot(y.astype(BF16), wout_ref[...], preferred_element_type=F32)
    out_ref[...] = x + _rms(out) * postg_ref[...]


def _epilogue_call(x2d, hf, hb, o, preg, w4, mng, bm, wbm, wba, wout, postg, te):
    T = x2d.shape[0]
    row = lambda i: (i, 0)
    const = lambda i: (0, 0)
    single = pl.Buffered(1)
    return pl.pallas_call(
        _epilogue_body,
        grid=(T // te,),
        in_specs=[pl.BlockSpec((te, D_MODEL), row),
                  pl.BlockSpec((te, M_WIDTH), row),
                  pl.BlockSpec((te, M_WIDTH), row),
                  pl.BlockSpec((te, A_WIDTH), row),
                  pl.BlockSpec(preg.shape, const),
                  pl.BlockSpec(w4.shape, const, pipeline_mode=single),
                  pl.BlockSpec(mng.shape, const),
                  pl.BlockSpec(bm.shape, const),
                  pl.BlockSpec(wbm.shape, const, pipeline_mode=single),
                  pl.BlockSpec(wba.shape, const, pipeline_mode=single),
                  pl.BlockSpec(wout.shape, const, pipeline_mode=single),
                  pl.BlockSpec(postg.shape, const)],
        out_specs=pl.BlockSpec((te, D_MODEL), row),
        out_shape=jax.ShapeDtypeStruct((T, D_MODEL), F32),
        compiler_params=pltpu.CompilerParams(dimension_semantics=("arbitrary",),
                                             vmem_limit_bytes=VMEM_LIMIT),
        name="epilogue",
    )(x2d, hf, hb, o, preg, w4, mng, bm, wbm, wba, wout, postg)


def _rope_tables(S):
    f32 = np.float32
    inv = ROPE_THETA ** (-np.arange(0, ROPE_DIMS, 2, dtype=np.float64) / ROPE_DIMS)
    ang = np.arange(S, dtype=np.float64)[:, None] * inv[None, :]
    cos, sin = np.cos(ang).astype(f32), np.sin(ang).astype(f32)
    one = np.ones((S, A_HEAD_DIM - ROPE_DIMS), f32)
    zero = np.zeros((S, A_HEAD_DIM - ROPE_DIMS), f32)
    rc = np.concatenate([cos, cos, one], axis=1)
    rs = np.concatenate([-sin, sin, zero], axis=1)
    rep = LANES // A_HEAD_DIM
    return jnp.asarray(np.tile(rc, (1, rep))), jnp.asarray(np.tile(rs, (1, rep)))


def _gate_layout(m):
    return jnp.concatenate([m[..., 1, :], m[..., 3, :], m[..., 0, :], m[..., 2, :]], axis=-1)


def _layer(x, layer_idx, pre_g, w_in, conv_w, conv_b, gate_b, m_norm_g, lambda_qk, a_norm_g,
           w_branch_m, w_branch_a, b_merge, w_out, post_g):
    B, S, _ = x.shape
    T = B * S
    tm = min(512, S)
    tq = min(2048, S // 2)
    nt = min(2, S // tq)
    tk = min(512, S // 2)
    tr = min(512, S)
    te = min(256, T)
    lam_init = 0.8 - 0.6 * math.exp(-0.3 * layer_idx)

    cuts = [0]
    for sz in SPLIT_SIZES:
        cuts.append(cuts[-1] + sz)
    col = lambda k: w_in[:, cuts[k]:cuts[k + 1]]
    wmqk = col(0).astype(BF16)
    w1 = jnp.concatenate([col(1), col(5), col(6)], axis=1).astype(BF16)
    wavt = col(7).T.astype(BF16)
    wgt = _gate_layout(col(2).reshape(D_MODEL, 4, M_HEADS)).T.astype(BF16)
    gb = _gate_layout(gate_b).reshape(N_GATE_ROWS, 1).astype(F32)
    w4 = jnp.concatenate([col(3), col(4), col(8), col(9)], axis=1).astype(BF16)
    rc, rs = _rope_tables(S)

    x2d = x.reshape(T, D_MODEL)
    preg = pre_g.reshape(1, D_MODEL)
    mq, mk, mv, gt, aq, ak, avt = _inproj_call(
        x2d, preg, wmqk, w1, wavt, wgt, conv_w, conv_b.reshape(1, 2 * M_QK), rc, rs, B, S, tm)
    rows, cols = _gates_call(gt, gb, B, S)
    hf, hb = _mlstm_call(mq, mk, mv, rows, cols, B, S, tr)
    o = _attn_call(aq, ak, avt, lambda_qk.astype(F32), a_norm_g.reshape(1, A_V_DIM), B, S, tq, nt, tk, tm,
                   lam_init)
    out = _epilogue_call(
        x2d, hf.reshape(T, M_WIDTH), hb.reshape(T, M_WIDTH), o, preg, w4,
        m_norm_g.reshape(1, M_WIDTH), b_merge.reshape(1, 2 * D_MODEL),
        w_branch_m.astype(BF16), w_branch_a.astype(BF16), w_out.astype(BF16),
        post_g.reshape(1, D_MODEL), te)
    return out.reshape(B, S, D_MODEL)


def kernel(x, pre_norm_g, w_in, conv_w, conv_b, gate_b, m_norm_g, lambda_qk, a_norm_g, w_branch_m,
           w_branch_a, b_merge, w_out, post_norm_g):
    for l in range(pre_norm_g.shape[0]):
        x = _layer(x, l, pre_norm_g[l], w_in[l], conv_w[l], conv_b[l], gate_b[l], m_norm_g[l],
                   lambda_qk[l], a_norm_g[l], w_branch_m[l], w_branch_a[l], b_merge[l], w_out[l],
                   post_norm_g[l])
    return x
```

```python
import functools
import math

import jax
import jax.numpy as jnp
import numpy as np
from jax import lax
from jax.experimental import pallas as pl
from jax.experimental.pallas import tpu as pltpu

D_MODEL = 1024
M_HEADS = 4
M_QK_DIM = 128
M_V_DIM = 256
M_QK = M_HEADS * M_QK_DIM
M_WIDTH = M_HEADS * M_V_DIM
M_CHUNK = 64
CONV_WIDTH = 4
A_HEADS = 8
A_HEAD_DIM = 64
A_V_DIM = 2 * A_HEAD_DIM
A_QK = A_HEADS * 2 * A_HEAD_DIM
A_WIDTH = A_HEADS * A_V_DIM
ROPE_THETA = 500000.0
ROPE_DIMS = A_HEAD_DIM // 4
EPS = 1e-6
SPLIT_SIZES = (2 * M_QK, M_WIDTH, 4 * M_HEADS, M_WIDTH, M_WIDTH, A_QK, A_QK, A_WIDTH, A_WIDTH, 2 * D_MODEL)

LANES = 128
MXU_N = 256
F32_ROWS = 8
A_V_ROWS = A_V_DIM + F32_ROWS
HALO = 16
N_GATE_ROWS = 2 * 2 * M_HEADS
VMEM_LIMIT = 60 * 1024 * 1024

F32 = jnp.float32
BF16 = jnp.bfloat16
_NT = (((1,), (1,)), ((), ()))
_TN = (((0,), (0,)), ((), ()))


def _rms(xf):
    return xf * lax.rsqrt(jnp.mean(xf * xf, axis=-1, keepdims=True) + EPS)


def _silu(y):
    return y * jax.nn.sigmoid(y)


def _inproj_body(tps, tm, xprev_ref, x_ref, xnext_ref, preg_ref, wmqk_ref, w1_ref, wavt_ref, wgt_ref,
                 convw_ref, convb_ref, rc_ref, rs_ref,
                 mq_ref, mk_ref, mv_ref, gt_ref, aq_ref, ak_ref, avt_ref, xext_ref):
    t = lax.rem(pl.program_id(0), tps)
    g = preg_ref[...]
    xn = (_rms(x_ref[...]) * g).astype(BF16)
    xp = jnp.where(t != 0, _rms(xprev_ref[...]) * g, 0.0).astype(BF16)
    xq = jnp.where(t != tps - 1, _rms(xnext_ref[...]) * g, 0.0).astype(BF16)
    xext_ref[0:HALO, :] = xp
    xext_ref[HALO:HALO + tm, :] = xn
    xext_ref[HALO + tm:, :] = xq
    n_ext = tm + 2 * HALO
    ext = jnp.dot(xext_ref[...], wmqk_ref[...], preferred_element_type=F32)
    mv = jnp.dot(xn, w1_ref[:, 0:M_WIDTH], preferred_element_type=F32)
    aq = jnp.dot(xn, w1_ref[:, M_WIDTH:M_WIDTH + A_QK], preferred_element_type=F32)
    ak = jnp.dot(xn, w1_ref[:, M_WIDTH + A_QK:M_WIDTH + 2 * A_QK], preferred_element_type=F32)
    avt = lax.dot_general(wavt_ref[...], xn, _NT, preferred_element_type=F32)
    gt_ref[...] = lax.dot_general(wgt_ref[...], xn, _NT, preferred_element_type=F32)

    left = CONV_WIDTH // 2
    y = None
    for j in range(CONV_WIDTH):
        sh = (left - j) % n_ext
        r = ext if sh == 0 else pltpu.roll(ext, sh, axis=0)
        term = convw_ref[j:j + 1, :] * r[HALO:HALO + tm, :]
        y = term if y is None else y + term
    y = _silu(y + convb_ref[...])
    qscale = M_QK_DIM ** -0.5
    for h in range(M_HEADS):
        mq_ref[0, h] = (y[:, h * M_QK_DIM:(h + 1) * M_QK_DIM] * qscale).astype(BF16)
        mk_ref[0, h] = y[:, M_QK + h * M_QK_DIM:M_QK + (h + 1) * M_QK_DIM].astype(BF16)
    for h in range(M_HEADS):
        mv_ref[0, h] = mv[:, h * M_V_DIM:(h + 1) * M_V_DIM].astype(BF16)
    rc, rs = rc_ref[...], rs_ref[...]
    half = ROPE_DIMS // 2
    first_half = (lax.broadcasted_iota(jnp.int32, rc.shape, 1) & (A_HEAD_DIM - 1)) < half

    def rope(a):
        partner = jnp.where(first_half, pltpu.roll(a, LANES - half, axis=1), pltpu.roll(a, half, axis=1))
        return a * rc + partner * rs

    ascale = A_HEAD_DIM ** -0.5 * math.log2(math.e)
    for h in range(A_HEADS):
        aq_ref[0, h] = (rope(aq[:, h * LANES:(h + 1) * LANES]) * ascale).astype(BF16)
        ak_ref[0, h] = rope(ak[:, h * LANES:(h + 1) * LANES]).astype(BF16)
    ones_tile = (lax.broadcasted_iota(jnp.int32, (F32_ROWS, tm), 0) == 0).astype(F32)
    for h in range(A_HEADS):
        avt_ref[0, 0, h, 0:A_V_DIM, :] = avt[h * A_V_DIM:(h + 1) * A_V_DIM, :].astype(BF16).astype(F32)
        avt_ref[0, 0, h, A_V_DIM:A_V_ROWS, :] = ones_tile


def _inproj_call(x2d, preg, wmqk, w1, wavt, wgt, convw, convb, rc, rs, B, S, tm):
    T = B * S
    tps = S // tm
    nh = tm // HALO
    const = lambda i: (0, 0)
    in_specs = [
        pl.BlockSpec((HALO, D_MODEL), lambda i: (jnp.maximum(i * nh - 1, 0), 0)),
        pl.BlockSpec((tm, D_MODEL), lambda i: (i, 0)),
        pl.BlockSpec((HALO, D_MODEL), lambda i: (jnp.minimum((i + 1) * nh, T // HALO - 1), 0)),
        pl.BlockSpec((1, D_MODEL), const),
        pl.BlockSpec(wmqk.shape, const),
        pl.BlockSpec(w1.shape, const),
        pl.BlockSpec(wavt.shape, const),
        pl.BlockSpec(wgt.shape, const),
        pl.BlockSpec(convw.shape, const),
        pl.BlockSpec(convb.shape, const),
        pl.BlockSpec((tm, LANES), lambda i: (lax.rem(i, tps), 0)),
        pl.BlockSpec((tm, LANES), lambda i: (lax.rem(i, tps), 0)),
    ]
    bt = lambda i: (i // tps, 0, lax.rem(i, tps), 0)
    out_shape = [
        jax.ShapeDtypeStruct((B, M_HEADS, S, M_QK_DIM), BF16),
        jax.ShapeDtypeStruct((B, M_HEADS, S, M_QK_DIM), BF16),
        jax.ShapeDtypeStruct((B, M_HEADS, S, M_V_DIM), BF16),
        jax.ShapeDtypeStruct((N_GATE_ROWS, T), F32),
        jax.ShapeDtypeStruct((B, A_HEADS, S, LANES), BF16),
        jax.ShapeDtypeStruct((B, A_HEADS, S, LANES), BF16),
        jax.ShapeDtypeStruct((B, tps, A_HEADS, A_V_ROWS, tm), F32),
    ]
    out_specs = [
        pl.BlockSpec((1, M_HEADS, tm, M_QK_DIM), bt),
        pl.BlockSpec((1, M_HEADS, tm, M_QK_DIM), bt),
        pl.BlockSpec((1, M_HEADS, tm, M_V_DIM), bt),
        pl.BlockSpec((N_GATE_ROWS, tm), lambda i: (0, i)),
        pl.BlockSpec((1, A_HEADS, tm, LANES), bt),
        pl.BlockSpec((1, A_HEADS, tm, LANES), bt),
        pl.BlockSpec((1, 1, A_HEADS, A_V_ROWS, tm), lambda i: (i // tps, lax.rem(i, tps), 0, 0, 0)),
    ]
    return pl.pallas_call(
        functools.partial(_inproj_body, tps, tm),
        grid=(T // tm,),
        in_specs=in_specs,
        out_specs=out_specs,
        out_shape=out_shape,
        scratch_shapes=[pltpu.VMEM((tm + 2 * HALO, D_MODEL), BF16)],
        compiler_params=pltpu.CompilerParams(dimension_semantics=("arbitrary",),
                                             vmem_limit_bytes=VMEM_LIMIT),
        name="inproj",
    )(x2d, x2d, x2d, preg, wmqk, w1, wavt, wgt, convw, convb, rc, rs)


N_GATE_COLS = 5


def _gates_body(S, ct, f_ref, i_ref, fb_ref, ib_ref, rows_ref, cols_ref, scan_ref, stage_ref):
    h = pl.program_id(1)

    @pl.when(h == 0)
    def _():
        _gate_scans(S, f_ref, i_ref, fb_ref, ib_ref, scan_ref)

    rows_ref[...] = jnp.zeros_like(rows_ref)
    stage_ref[...] = jnp.zeros_like(stage_ref)
    for d in range(2):
        rows_ref[0, 0, d:d + 1, :] = scan_ref[N_GATE_COLS, pl.ds(d * M_HEADS + h, 1), :]
    for t in range(S // ct):
        sl = slice(t * ct, (t + 1) * ct)
        for k in range(N_GATE_COLS):
            for d in range(2):
                stage_ref[2 * k + d:2 * k + d + 1, :] = scan_ref[k, pl.ds(d * M_HEADS + h, 1), sl]
        cols_ref[0, 0, sl, :] = stage_ref[...].T


def _gate_scans(S, f_ref, i_ref, fb_ref, ib_ref, scan_ref):
    L = M_CHUNK
    f = f_ref[...] + fb_ref[...]
    ig = i_ref[...] + ib_ref[...]
    lf = jnp.minimum(f, 0.0) - jnp.log1p(jnp.exp(-jnp.abs(f)))
    lane = lax.broadcasted_iota(jnp.int32, f.shape, 1)
    pos = lane & (L - 1)
    fwd = lax.broadcasted_iota(jnp.int32, f.shape, 0) < M_HEADS

    def from_left(x, s):
        return pltpu.roll(x, s, axis=1)

    def from_right(x, s):
        return pltpu.roll(x, S - s, axis=1)

    def chunk_scans(x, op, ident):
        pre, suf = x, x
        s = 1
        while s < L:
            pre = op(pre, jnp.where(pos >= s, from_left(pre, s), ident))
            suf = op(suf, jnp.where(pos < L - s, from_right(suf, s), ident))
            s *= 2
        return pre, suf

    pre, suf = chunk_scans(lf, jnp.add, 0.0)
    a = jnp.where(fwd, pre, suf)
    a_tot = pre + suf - lf
    w = ig - a
    wpre, wsuf = chunk_scans(w, jnp.maximum, -jnp.inf)
    rowmax = a + jnp.where(fwd, wpre, wsuf)
    g_max = a_tot + jnp.maximum(wpre, wsuf)

    af, gf, ab, gb = a_tot, g_max, a_tot, g_max
    s = L
    while s < S:
        ok = lane >= s
        a_sh, g_sh = from_left(af, s), from_left(gf, s)
        gf = jnp.where(ok, jnp.maximum(g_sh + af, gf), gf)
        af = jnp.where(ok, af + a_sh, af)
        ok = lane < S - s
        a_sh, g_sh = from_right(ab, s), from_right(gb, s)
        gb = jnp.where(ok, jnp.maximum(g_sh + ab, gb), gb)
        ab = jnp.where(ok, ab + a_sh, ab)
        s *= 2
    m_f = jnp.maximum(af, gf)
    m_b = jnp.maximum(ab, gb)
    m_prev = jnp.where(fwd, jnp.where(lane >= L, from_left(m_f, L), 0.0),
                       jnp.where(lane < S - L, from_right(m_b, L), 0.0))

    inter = a + m_prev
    mj = jnp.maximum(inter, rowmax)
    m_new = jnp.maximum(a_tot + m_prev, g_max)
    scan_ref[0] = a - mj
    scan_ref[1] = jnp.exp(inter - mj)
    scan_ref[2] = jnp.exp(-mj)
    scan_ref[3] = jnp.exp(a_tot + w - m_new)
    scan_ref[4] = jnp.exp(a_tot + m_prev - m_new)
    scan_ref[N_GATE_COLS] = w


def _gates_call(gt, gb, B, S):
    ct = min(2048, S)
    return pl.pallas_call(
        functools.partial(_gates_body, S, ct),
        grid=(B, M_HEADS),
        in_specs=[pl.BlockSpec((8, S), lambda b, h: (0, b)),
                  pl.BlockSpec((8, S), lambda b, h: (1, b)),
                  pl.BlockSpec((8, 1), lambda b, h: (0, 0)),
                  pl.BlockSpec((8, 1), lambda b, h: (1, 0))],
        out_specs=[pl.BlockSpec((1, 1, 8, S), lambda b, h: (b, h, 0, 0)),
                   pl.BlockSpec((1, 1, S, LANES), lambda b, h: (b, h, 0, 0))],
        out_shape=[jax.ShapeDtypeStruct((B, M_HEADS, 8, S), F32),
                   jax.ShapeDtypeStruct((B, M_HEADS, S, LANES), F32)],
        scratch_shapes=[pltpu.VMEM((N_GATE_COLS + 1, 8, S), F32),
                        pltpu.VMEM((LANES, ct), F32)],
        compiler_params=pltpu.CompilerParams(dimension_semantics=("arbitrary", "arbitrary"),
                                             vmem_limit_bytes=VMEM_LIMIT),
        name="gates",
    )(gt, gt, gb, gb)


class _MlstmScan:
    def __init__(self, rev, nchunk, q_ref, k_ref, v_ref, rows_ref, cols_ref, h_ref, c_ref):
        self.rev, self.d = rev, (1 if rev else 0)
        self.q_ref, self.k_ref, self.v_ref = q_ref, k_ref, v_ref
        self.rows_ref, self.cols_ref, self.h_ref, self.c_ref = rows_ref, cols_ref, h_ref, c_ref
        self.order = list(range(nchunk - 1, -1, -1) if rev else range(nchunk))
        L = M_CHUNK
        row_i = lax.broadcasted_iota(jnp.int32, (L, L), 0)
        col_i = lax.broadcasted_iota(jnp.int32, (L, L), 1)
        self.visible = (col_i >= row_i) if rev else (col_i <= row_i)
        self.ones_col = (lax.broadcasted_iota(jnp.int32, (L, LANES), 1) == 0).astype(BF16)

    @staticmethod
    def rows_of(c):
        return slice(c * M_CHUNK, (c + 1) * M_CHUNK)

    def gate_col(self, c, k):
        return self.cols_ref[0, 0, self.rows_of(c), 2 * k + self.d:2 * k + self.d + 1]

    def v_ext(self, c):
        return jnp.concatenate([self.v_ref[0, 0, self.rows_of(c), :], self.ones_col], axis=1)

    def input_products(self):
        self.qk, self.kv = {}, {}
        for c in self.order:
            q, k = self.q_ref[0, 0, self.rows_of(c), :], self.k_ref[0, 0, self.rows_of(c), :]
            self.qk[c] = lax.dot_general(q, k, _NT, preferred_element_type=F32)
            wv = (self.gate_col(c, 3) * self.v_ext(c).astype(F32)).astype(BF16)
            self.kv[c] = lax.dot_general(k, wv, _TN, preferred_element_type=F32)

    def states(self):
        C = self.c_ref[...]
        self.c_in = {}
        for c in self.order:
            self.c_in[c] = C.astype(BF16)
            C = self.gate_col(c, 4)[0:1, :] * C + self.kv[c]
        self.c_ref[...] = C

    def read_out(self):
        for c in self.order:
            u, w_inter, e = self.gate_col(c, 0), self.gate_col(c, 1), self.gate_col(c, 2)
            w_row = self.rows_ref[0, 0, self.d:self.d + 1, self.rows_of(c)]
            p = jnp.exp(jnp.where(self.visible, u + w_row, -jnp.inf))
            s = self.qk[c] * p
            q = self.q_ref[0, 0, self.rows_of(c), :]
            num = (w_inter * jnp.dot(q, self.c_in[c], preferred_element_type=F32)
                   + jnp.dot(s.astype(BF16), self.v_ext(c), preferred_element_type=F32))
            den = num[:, M_V_DIM:M_V_DIM + 1]
            self.h_ref[0, self.rows_of(c), :] = num[:, 0:M_V_DIM] / jnp.maximum(jnp.abs(den), e)


def _mlstm_body(nchunk, qf_ref, kf_ref, vf_ref, rowsf_ref, colsf_ref, qb_ref, kb_ref, vb_ref, rowsb_ref,
                colsb_ref, hf_ref, hb_ref, cf_ref, cb_ref):
    @pl.when(pl.program_id(2) == 0)
    def _():
        cf_ref[...] = jnp.zeros_like(cf_ref)
        cb_ref[...] = jnp.zeros_like(cb_ref)

    fwd = _MlstmScan(False, nchunk, qf_ref, kf_ref, vf_ref, rowsf_ref, colsf_ref, hf_ref, cf_ref)
    bwd = _MlstmScan(True, nchunk, qb_ref, kb_ref, vb_ref, rowsb_ref, colsb_ref, hb_ref, cb_ref)
    fwd.input_products()
    bwd.input_products()
    fwd.states()
    bwd.states()
    fwd.read_out()
    bwd.read_out()


def _mlstm_call(mq, mk, mv, rows, cols, B, S, tr):
    ns = S // tr

    def specs(pos):
        return [pl.BlockSpec((1, 1, tr, M_QK_DIM), lambda b, h, s: (b, h, pos(s), 0)),
                pl.BlockSpec((1, 1, tr, M_QK_DIM), lambda b, h, s: (b, h, pos(s), 0)),
                pl.BlockSpec((1, 1, tr, M_V_DIM), lambda b, h, s: (b, h, pos(s), 0)),
                pl.BlockSpec((1, 1, 8, tr), lambda b, h, s: (b, h, 0, pos(s))),
                pl.BlockSpec((1, 1, tr, LANES), lambda b, h, s: (b, h, pos(s), 0))]

    fpos = lambda s: s
    bpos = lambda s: ns - 1 - s
    state = pltpu.VMEM((M_QK_DIM, M_V_DIM + LANES), F32)
    return pl.pallas_call(
        functools.partial(_mlstm_body, tr // M_CHUNK),
        grid=(B, M_HEADS, ns),
        in_specs=specs(fpos) + specs(bpos),
        out_specs=[pl.BlockSpec((1, tr, M_V_DIM), lambda b, h, s: (b, fpos(s), h)),
                   pl.BlockSpec((1, tr, M_V_DIM), lambda b, h, s: (b, bpos(s), h))],
        out_shape=[jax.ShapeDtypeStruct((B, S, M_WIDTH), F32),
                   jax.ShapeDtypeStruct((B, S, M_WIDTH), F32)],
        scratch_shapes=[state, state],
        compiler_params=pltpu.CompilerParams(dimension_semantics=("arbitrary", "arbitrary", "arbitrary"),
                                             vmem_limit_bytes=VMEM_LIMIT),
        name="mlstm",
    )(mq, mk, mv, rows, cols, mq, mk, mv, rows, cols)


def _attn_body(nkc, nt, tq, tk, tv, lam_init, q_ref, k_ref, vt_ref, lq_ref, g_ref, o_ref,
               qz_ref, st0_ref, st1_ref, m_ref, acc_ref):
    def tile_rows(t):
        return pl.ds(pl.multiple_of(t * tq, tq), tq)

    def load_queries(t):
        q = q_ref[0, 0, tile_rows(t), :]
        lane = lax.broadcasted_iota(jnp.int32, q.shape, 1)
        zero = jnp.zeros_like(q)
        qz_ref[0:tq, :] = jnp.where(lane < A_HEAD_DIM, q, zero)
        qz_ref[tq:2 * tq, :] = jnp.where(lane >= A_HEAD_DIM, q, zero)

    def reset_stats():
        m_ref[...] = jnp.full(m_ref.shape, -jnp.inf, F32)
        acc_ref[...] = jnp.zeros_like(acc_ref)

    strips = [slice(c * MXU_N, (c + 1) * MXU_N) for c in range(2 * tq // MXU_N)]
    nv = tk // tv

    def score_strip(j, st_ref, sl):
        kj = k_ref[0, 0, pl.ds(pl.multiple_of(j * tk, tk), tk), :]
        st = lax.dot_general(kj, qz_ref[sl, :], _NT, preferred_element_type=F32)
        st_ref[:, sl] = st
        return jnp.max(st, axis=0, keepdims=True)

    def accumulate_strip(j, st_ref, sl, cm):
        m_old = m_ref[:, sl]
        m_new = jnp.maximum(m_old, cm)
        alpha = jnp.exp2(m_old - m_new)
        p = jnp.exp2(st_ref[:, sl] - m_new)
        pv = jnp.dot(vt_ref[0, j * nv, 0], p[0:tv], preferred_element_type=F32)
        for u in range(1, nv):
            pv = pv + jnp.dot(vt_ref[0, j * nv + u, 0], p[u * tv:(u + 1) * tv],
                              preferred_element_type=F32)
        acc_ref[:, sl] = alpha * acc_ref[:, sl] + pv
        m_ref[:, sl] = m_new

    def step(j_next, st_next_ref, j_cur, st_cur_ref, cmax_cur):
        cmax_next = []
        for sl, cm in zip(strips, cmax_cur):
            cmax_next.append(score_strip(j_next, st_next_ref, sl))
            accumulate_strip(j_cur, st_cur_ref, sl, cm)
        return tuple(cmax_next)

    def pair(i, cmax0):
        j = 2 * i
        cmax1 = step(j + 1, st1_ref, j, st0_ref, cmax0)
        return step(j + 2, st0_ref, j + 1, st1_ref, cmax1)

    pairs_per_trip = 4

    def trip(i, cmax0):
        for u in range(pairs_per_trip):
            cmax0 = pair(pairs_per_trip * i + u, cmax0)
        return cmax0

    def all_but_last_chunk(cmax0):
        npairs = nkc // 2 - 1
        ntrips = npairs // pairs_per_trip
        cmax0 = lax.fori_loop(0, ntrips, trip, cmax0)
        for i in range(ntrips * pairs_per_trip, npairs):
            cmax0 = pair(i, cmax0)
        return step(nkc - 1, st1_ref, nkc - 2, st0_ref, cmax0)

    lq = lq_ref[...]
    lam = (jnp.exp(jnp.sum(lq[0:1] * lq[1:2], axis=1, keepdims=True))
           - jnp.exp(jnp.sum(lq[2:3] * lq[3:4], axis=1, keepdims=True)) + lam_init)

    def finalize(t):
        accn = acc_ref[0:A_V_DIM, :] / acc_ref[A_V_DIM:A_V_DIM + 1, :]
        ot = accn[:, 0:tq] - lam * accn[:, tq:2 * tq]
        otn = ot * lax.rsqrt(jnp.mean(ot * ot, axis=0, keepdims=True) + EPS)
        o_ref[tile_rows(t), :] = (otn.T * g_ref[...]) * (1.0 - lam_init)

    def tile(t, cmax0):
        cmax1 = all_but_last_chunk(cmax0)
        load_queries(t + 1)
        cmax_next = []
        for sl, cm in zip(strips, cmax1):
            cmax_next.append(score_strip(0, st0_ref, sl))
            accumulate_strip(nkc - 1, st1_ref, sl, cm)
        finalize(t)
        reset_stats()
        return tuple(cmax_next)

    load_queries(0)
    reset_stats()
    cmax0 = tuple(score_strip(0, st0_ref, sl) for sl in strips)
    cmax0 = lax.fori_loop(0, nt - 1, tile, cmax0)
    cmax1 = all_but_last_chunk(cmax0)
    for sl, cm in zip(strips, cmax1):
        accumulate_strip(nkc - 1, st1_ref, sl, cm)
    finalize(nt - 1)


def _attn_call(aq, ak, avt, lq, ag, B, S, tq, nt, tk, tv, lam_init):
    nkc = S // tk
    nq = S // (nt * tq)
    assert nkc >= 2 and nkc % 2 == 0, "key chunks are processed in pairs"
    assert tk % tv == 0 and S % (nt * tq) == 0
    return pl.pallas_call(
        functools.partial(_attn_body, nkc, nt, tq, tk, tv, lam_init),
        grid=(B, A_HEADS, nq),
        in_specs=[pl.BlockSpec((1, 1, nt * tq, LANES), lambda b, h, i: (b, h, i, 0)),
                  pl.BlockSpec((1, 1, S, LANES), lambda b, h, i: (b, h, 0, 0)),
                  pl.BlockSpec((1, S // tv, 1, A_V_ROWS, tv), lambda b, h, i: (b, 0, h, 0, 0)),
                  pl.BlockSpec(lq.shape, lambda b, h, i: (0, 0)),
                  pl.BlockSpec(ag.shape, lambda b, h, i: (0, 0))],
        out_specs=pl.BlockSpec((nt * tq, A_V_DIM), lambda b, h, i: (b * nq + i, h)),
        out_shape=jax.ShapeDtypeStruct((B * S, A_WIDTH), F32),
        scratch_shapes=[pltpu.VMEM((2 * tq, LANES), BF16),
                        pltpu.VMEM((tk, 2 * tq), F32),
                        pltpu.VMEM((tk, 2 * tq), F32),
                        pltpu.VMEM((1, 2 * tq), F32),
                        pltpu.VMEM((A_V_ROWS, 2 * tq), F32)],
        compiler_params=pltpu.CompilerParams(dimension_semantics=("arbitrary", "arbitrary", "arbitrary"),
                                             vmem_limit_bytes=VMEM_LIMIT),
        name="attn",
    )(aq, ak, avt, lq, ag)


def _epilogue_body(x_ref, hf_ref, hb_ref, o_ref, preg_ref, w4_ref, mng_ref, bm_ref, wbm_ref, wba_ref,
                   wout_ref, postg_ref, out_ref):
    x = x_ref[...]
    xn = (_rms(x) * preg_ref[...]).astype(BF16)

    def proj(lo, hi):
        return jnp.dot(xn, w4_ref[:, lo:hi], preferred_element_type=F32)

    h = hf_ref[...] + hb_ref[...]
    mng = mng_ref[...]
    hn = jnp.concatenate(
        [_rms(h[:, k * M_V_DIM:(k + 1) * M_V_DIM]) * mng[:, k * M_V_DIM:(k + 1) * M_V_DIM]
         for k in range(M_HEADS)], axis=1)
    h_a = jax.nn.sigmoid(proj(0, M_WIDTH)) * hn * _silu(proj(M_WIDTH, 2 * M_WIDTH))
    ya = jnp.dot(h_a.astype(BF16), wbm_ref[...], preferred_element_type=F32)
    h_b = o_ref[...] * _silu(proj(2 * M_WIDTH, 2 * M_WIDTH + A_WIDTH))
    yb = jnp.dot(h_b.astype(BF16), wba_ref[...], preferred_element_type=F32)
    g0 = 2 * M_WIDTH + A_WIDTH
    bm = bm_ref[...]
    g_a = jax.nn.sigmoid(proj(g0, g0 + D_MODEL) + bm[:, 0:D_MODEL])
    g_b = jax.nn.sigmoid(proj(g0 + D_MODEL, g0 + 2 * D_MODEL) + bm[:, D_MODEL:2 * D_MODEL])
    y = g_a * ya + g_b * yb
    out = jnp.dot(y.astype(BF16), wout_ref[...], preferred_element_type=F32)
    out_ref[...] = x + _rms(out) * postg_ref[...]


def _epilogue_call(x2d, hf, hb, o, preg, w4, mng, bm, wbm, wba, wout, postg, te):
    T = x2d.shape[0]
    row = lambda i: (i, 0)
    const = lambda i: (0, 0)
    single = pl.Buffered(1)
    return pl.pallas_call(
        _epilogue_body,
        grid=(T // te,),
        in_specs=[pl.BlockSpec((te, D_MODEL), row),
                  pl.BlockSpec((te, M_WIDTH), row),
                  pl.BlockSpec((te, M_WIDTH), row),
                  pl.BlockSpec((te, A_WIDTH), row),
                  pl.BlockSpec(preg.shape, const),
                  pl.BlockSpec(w4.shape, const, pipeline_mode=single),
                  pl.BlockSpec(mng.shape, const),
                  pl.BlockSpec(bm.shape, const),
                  pl.BlockSpec(wbm.shape, const, pipeline_mode=single),
                  pl.BlockSpec(wba.shape, const, pipeline_mode=single),
                  pl.BlockSpec(wout.shape, const, pipeline_mode=single),
                  pl.BlockSpec(postg.shape, const)],
        out_specs=pl.BlockSpec((te, D_MODEL), row),
        out_shape=jax.ShapeDtypeStruct((T, D_MODEL), F32),
        compiler_params=pltpu.CompilerParams(dimension_semantics=("arbitrary",),
                                             vmem_limit_bytes=VMEM_LIMIT),
        name="epilogue",
    )(x2d, hf, hb, o, preg, w4, mng, bm, wbm, wba, wout, postg)


def _rope_tables(S):
    f32 = np.float32
    inv = ROPE_THETA ** (-np.arange(0, ROPE_DIMS, 2, dtype=np.float64) / ROPE_DIMS)
    ang = np.arange(S, dtype=np.float64)[:, None] * inv[None, :]
    cos, sin = np.cos(ang).astype(f32), np.sin(ang).astype(f32)
    one = np.ones((S, A_HEAD_DIM - ROPE_DIMS), f32)
    zero = np.zeros((S, A_HEAD_DIM - ROPE_DIMS), f32)
    rc = np.concatenate([cos, cos, one], axis=1)
    rs = np.concatenate([-sin, sin, zero], axis=1)
    rep = LANES // A_HEAD_DIM
    return jnp.asarray(np.tile(rc, (1, rep))), jnp.asarray(np.tile(rs, (1, rep)))


def _gate_layout(m):
    return jnp.concatenate([m[..., 1, :], m[..., 3, :], m[..., 0, :], m[..., 2, :]], axis=-1)


def _layer(x, layer_idx, pre_g, w_in, conv_w, conv_b, gate_b, m_norm_g, lambda_qk, a_norm_g,
           w_branch_m, w_branch_a, b_merge, w_out, post_g):
    B, S, _ = x.shape
    T = B * S
    tm = min(512, S)
    tq = min(2048, S // 2)
    nt = min(2, S // tq)
    tk = min(512, S // 2)
    tr = min(512, S)
    te = min(256, T)
    lam_init = 0.8 - 0.6 * math.exp(-0.3 * layer_idx)

    cuts = [0]
    for sz in SPLIT_SIZES:
        cuts.append(cuts[-1] + sz)
    col = lambda k: w_in[:, cuts[k]:cuts[k + 1]]
    wmqk = col(0).astype(BF16)
    w1 = jnp.concatenate([col(1), col(5), col(6)], axis=1).astype(BF16)
    wavt = col(7).T.astype(BF16)
    wgt = _gate_layout(col(2).reshape(D_MODEL, 4, M_HEADS)).T.astype(BF16)
    gb = _gate_layout(gate_b).reshape(N_GATE_ROWS, 1).astype(F32)
    w4 = jnp.concatenate([col(3), col(4), col(8), col(9)], axis=1).astype(BF16)
    rc, rs = _rope_tables(S)

    x2d = x.reshape(T, D_MODEL)
    preg = pre_g.reshape(1, D_MODEL)
    mq, mk, mv, gt, aq, ak, avt = _inproj_call(
        x2d, preg, wmqk, w1, wavt, wgt, conv_w, conv_b.reshape(1, 2 * M_QK), rc, rs, B, S, tm)
    rows, cols = _gates_call(gt, gb, B, S)
    hf, hb = _mlstm_call(mq, mk, mv, rows, cols, B, S, tr)
    o = _attn_call(aq, ak, avt, lambda_qk.astype(F32), a_norm_g.reshape(1, A_V_DIM), B, S, tq, nt, tk, tm,
                   lam_init)
    out = _epilogue_call(
        x2d, hf.reshape(T, M_WIDTH), hb.reshape(T, M_WIDTH), o, preg, w4,
        m_norm_g.reshape(1, M_WIDTH), b_merge.reshape(1, 2 * D_MODEL),
        w_branch_m.astype(BF16), w_branch_a.astype(BF16), w_out.astype(BF16),
        post_g.reshape(1, D_MODEL), te)
    return out.reshape(B, S, D_MODEL)


def kernel(x, pre_norm_g, w_in, conv_w, conv_b, gate_b, m_norm_g, lambda_qk, a_norm_g, w_branch_m,
           w_branch_a, b_merge, w_out, post_norm_g):
    for l in range(pre_norm_g.shape[0]):
        x = _layer(x, l, pre_norm_g[l], w_in[l], conv_w[l], conv_b[l], gate_b[l], m_norm_g[l],
                   lambda_qk[l], a_norm_g[l], w_branch_m[l], w_branch_a[l], b_merge[l], w_out[l],
                   post_norm_g[l])
    return x
```
